```python
import jax, jax.numpy as jnp
from jax import lax
import numpy as np

D_MODEL = 2048
BATCH = 1
SEQ = 8192
DEPTH = 1

N_MLA_HEADS = D_MODEL // 256
MLA_NOPE_DIM = 128
ROPE_DIM = 64
MLA_QK_DIM = MLA_NOPE_DIM + ROPE_DIM
MLA_V_DIM = 128
Q_LORA = D_MODEL // 4
KV_LORA = D_MODEL // 8
ROPE_THETA = 10000.0
Q_BLOCK = 128
MLA_WIDTH = N_MLA_HEADS * MLA_V_DIM

N_MLSTM_HEADS = 4
MLSTM_HEAD_DIM = D_MODEL // 8
MLSTM_WIDTH = N_MLSTM_HEADS * MLSTM_HEAD_DIM
MLSTM_CHUNK = 128
CONV_WIDTH = 5
N_GATE_COLS = 4 * N_MLSTM_HEADS

D_MIX = MLA_WIDTH + MLSTM_WIDTH
D_FF = 4 * D_MODEL
IN_SIZES = (Q_LORA, KV_LORA, ROPE_DIM, MLSTM_WIDTH, MLSTM_WIDTH, MLSTM_WIDTH, MLSTM_WIDTH, N_GATE_COLS)
D_IN = sum(IN_SIZES)
IN_SPLIT_POINTS = tuple(int(v) for v in np.cumsum(IN_SIZES)[:-1])
EPS = 1e-6
M_INIT = -1e30

kernel_name = "hybrid_mla_mlstm_adaln_block"


def rmsnorm(x, g):
    xf = x.astype(jnp.float32)
    y = xf * lax.rsqrt(jnp.mean(xf * xf, axis=-1, keepdims=True) + EPS)
    return (y * g.astype(jnp.float32)).astype(x.dtype)


def modulate(h, shift, scale):
    return h * (1 + scale) + shift


def rope(xp, positions):
    half = ROPE_DIM // 2
    freqs = ROPE_THETA ** (-jnp.arange(half, dtype=jnp.float32) / half)
    ang = positions.astype(jnp.float32)[..., None] * freqs
    cos = jnp.cos(ang)[:, :, None, :]
    sin = jnp.sin(ang)[:, :, None, :]
    x1 = xp[..., :half].astype(jnp.float32)
    x2 = xp[..., half:].astype(jnp.float32)
    out = jnp.concatenate([x1 * cos - x2 * sin, x1 * sin + x2 * cos], axis=-1)
    return out.astype(xp.dtype)


def dense_attention_blocks(q, k, v):
    B, S, H, Dq = q.shape
    nb = S // Q_BLOCK
    qb = q.reshape(B, nb, Q_BLOCK, H, Dq).transpose(1, 0, 2, 3, 4)
    scale = Dq ** -0.5

    def one_block(qi):
        s = jnp.einsum('bqhd,bkhd->bhqk', qi, k).astype(jnp.float32) * scale
        p = jax.nn.softmax(s, axis=-1).astype(v.dtype)
        return jnp.einsum('bhqk,bkhd->bqhd', p, v)

    o = lax.map(one_block, qb)
    return o.transpose(1, 0, 2, 3, 4).reshape(B, S, H * v.shape[-1])


def mlstm_chunkwise(q, k, v, log_i, log_f):
    B, H, S, d = q.shape
    L = MLSTM_CHUNK
    nc = S // L
    chunk = lambda t: t.reshape(B, H, nc, L, *t.shape[3:]).swapaxes(0, 2).swapaxes(1, 2)
    qc, kc, vc = chunk(q), chunk(k), chunk(v)
    ic, fc = chunk(log_i), chunk(log_f)
    tril = jnp.tril(jnp.ones((L, L), dtype=bool))

    def step(carry, inp):
        C, n, m = carry
        qt, kt, vt, it, ft = inp
        b = jnp.cumsum(ft, axis=-1)
        log_inter = b + m[..., None]
        logD = jnp.where(tril, b[..., :, None] - b[..., None, :] + it[..., None, :], -jnp.inf)
        m_t = jnp.maximum(log_inter, jnp.max(logD, axis=-1))
        Dm = jnp.exp(logD - m_t[..., None])
        w_inter = jnp.exp(log_inter - m_t)
        scores = jnp.einsum('bhtd,bhsd->bhts', qt, kt) * Dm
        num = jnp.einsum('bhts,bhsd->bhtd', scores, vt) \
            + w_inter[..., None] * jnp.einsum('bhvk,bhtk->bhtv', C, qt)
        den = jnp.sum(scores, axis=-1) + w_inter * jnp.einsum('bhtk,bhk->bht', qt, n)
        h = num / jnp.maximum(jnp.abs(den), jnp.exp(-m_t))[..., None]
        bL = b[..., -1]
        log_w = bL[..., None] - b + it
        m_new = jnp.maximum(bL + m, jnp.max(log_w, axis=-1))
        decay = jnp.exp(bL + m - m_new)
        w = jnp.exp(log_w - m_new[..., None])
        C_new = decay[..., None, None] * C + jnp.einsum('bhs,bhsv,bhsk->bhvk', w, vt, kt)
        n_new = decay[..., None] * n + jnp.einsum('bhs,bhsk->bhk', w, kt)
        return (C_new, n_new, m_new), h

    init = (jnp.zeros((B, H, d, d), jnp.float32),
            jnp.zeros((B, H, d), jnp.float32),
            jnp.full((B, H), M_INIT, jnp.float32))
    _, hs = lax.scan(step, init, (qc, kc, vc, ic, fc))
    return hs.transpose(1, 2, 0, 3, 4).reshape(B, H, S, d)


def setup_inputs(seed: int = 0) -> dict:
    key = jax.random.key(seed)
    ks = jax.random.split(key, 24)
    f32 = jnp.float32
    nrm = lambda k, shape, fan_in, s=1.0: jax.random.normal(k, shape, f32) * (s * fan_in ** -0.5)
    gain = lambda k, shape: 1.0 + 0.05 * jax.random.normal(k, shape, f32)
    gate_base = jnp.concatenate([jnp.zeros((N_MLSTM_HEADS,), f32), jnp.linspace(3.0, 6.0, N_MLSTM_HEADS, dtype=f32),
                                 jnp.zeros((N_MLSTM_HEADS,), f32), jnp.linspace(3.0, 6.0, N_MLSTM_HEADS, dtype=f32)])
    return {
        "x": jax.random.normal(ks[0], (BATCH, SEQ, D_MODEL), f32),
        "c": jax.random.normal(ks[1], (BATCH, D_MODEL), f32),
        "positions": jnp.broadcast_to(jnp.arange(SEQ, dtype=jnp.int32), (BATCH, SEQ)),
        "w_ada": nrm(ks[2], (DEPTH, D_MODEL, 6 * D_MODEL), D_MODEL, 0.5),
        "b_ada": 0.02 * jax.random.normal(ks[3], (DEPTH, 6 * D_MODEL), f32),
        "norm_mix_g": gain(ks[4], (DEPTH, D_MODEL)),
        "w_in": nrm(ks[5], (DEPTH, D_MODEL, D_IN), D_MODEL),
        "b_gates": gate_base + 0.1 * jax.random.normal(ks[6], (DEPTH, N_GATE_COLS), f32),
        "conv_w": nrm(ks[7], (DEPTH, CONV_WIDTH, 2 * MLSTM_WIDTH), CONV_WIDTH),
        "conv_b": 0.02 * jax.random.normal(ks[8], (DEPTH, 2 * MLSTM_WIDTH), f32),
        "q_lora_g": gain(ks[9], (DEPTH, Q_LORA)),
        "w_uq": nrm(ks[10], (DEPTH, Q_LORA, N_MLA_HEADS * MLA_QK_DIM), Q_LORA),
        "kv_lora_g": gain(ks[11], (DEPTH, KV_LORA)),
        "w_ukv": nrm(ks[12], (DEPTH, KV_LORA, N_MLA_HEADS * (MLA_NOPE_DIM + MLA_V_DIM)), KV_LORA),
        "q_norm_g": gain(ks[13], (DEPTH, MLA_QK_DIM)),
        "k_norm_g": gain(ks[14], (DEPTH, MLA_QK_DIM)),
        "mlstm_norm_g": gain(ks[15], (DEPTH, N_MLSTM_HEADS, MLSTM_HEAD_DIM)),
        "w_out": nrm(ks[16], (DEPTH, D_MIX, D_MODEL), D_MIX),
        "norm_mlp_g": gain(ks[17], (DEPTH, D_MODEL)),
        "w_ff1": nrm(ks[18], (DEPTH, D_MODEL, D_FF), D_MODEL),
        "w_ff2": nrm(ks[19], (DEPTH, D_FF, D_MODEL), D_FF),
    }


def reference(x, c, positions, w_ada, b_ada, norm_mix_g, w_in, b_gates, conv_w, conv_b,
              q_lora_g, w_uq, kv_lora_g, w_ukv, q_norm_g, k_norm_g, mlstm_norm_g,
              w_out, norm_mlp_g, w_ff1, w_ff2):
    B, S, D = x.shape
    H, HM, DM = N_MLA_HEADS, N_MLSTM_HEADS, MLSTM_HEAD_DIM
    for l in range(DEPTH):
        mod = jax.nn.silu(c) @ w_ada[l] + b_ada[l]
        shift1, scale1, gate1, shift2, scale2, gate2 = jnp.split(mod[:, None, :], 6, axis=-1)

        h = modulate(rmsnorm(x, norm_mix_g[l]), shift1, scale1)
        proj = h @ w_in[l]
        c_q, c_kv, k_pe, q_m, k_m, v_m, o_m, g_m = jnp.split(proj, IN_SPLIT_POINTS, axis=-1)

        q = (rmsnorm(c_q, q_lora_g[l]) @ w_uq[l]).reshape(B, S, H, MLA_QK_DIM)
        kv = (rmsnorm(c_kv, kv_lora_g[l]) @ w_ukv[l]).reshape(B, S, H, MLA_NOPE_DIM + MLA_V_DIM)
        k_nope, v = kv[..., :MLA_NOPE_DIM], kv[..., MLA_NOPE_DIM:]
        k_full = jnp.concatenate([k_nope, jnp.broadcast_to(k_pe[:, :, None, :], (B, S, H, ROPE_DIM))], axis=-1)
        q = rmsnorm(q, q_norm_g[l])
        k_full = rmsnorm(k_full, k_norm_g[l])
        q = jnp.concatenate([q[..., :MLA_NOPE_DIM], rope(q[..., MLA_NOPE_DIM:], positions)], axis=-1)
        k_full = jnp.concatenate([k_full[..., :MLA_NOPE_DIM], rope(k_full[..., MLA_NOPE_DIM:], positions)], axis=-1)
        attn_out = dense_attention_blocks(q, k_full, v)

        qk = jnp.concatenate([q_m, k_m], axis=-1)
        qk = lax.conv_general_dilated(qk, conv_w[l][:, None, :].astype(qk.dtype), window_strides=(1,),
                                      padding='SAME', dimension_numbers=('NWC', 'WIO', 'NWC'),
                                      feature_group_count=2 * MLSTM_WIDTH)
        qk = jax.nn.silu(qk + conv_b[l])
        to_heads = lambda t: t.reshape(B, S, HM, DM).transpose(0, 2, 1, 3).astype(jnp.float32)
        qh = to_heads(qk[..., :MLSTM_WIDTH])
        kh = to_heads(qk[..., MLSTM_WIDTH:]) * (DM ** -0.5)
        vh = to_heads(v_m)
        gates = (g_m.astype(jnp.float32) + b_gates[l].astype(jnp.float32)).reshape(B, S, 4, HM)
        gates = gates.transpose(2, 0, 3, 1)
        i_fwd, f_fwd, i_bwd, f_bwd = gates[0], gates[1], gates[2], gates[3]
        h_fwd = mlstm_chunkwise(qh, kh, vh, i_fwd, jax.nn.log_sigmoid(f_fwd))
        flip = lambda t: jnp.flip(t, axis=2)
        h_bwd = flip(mlstm_chunkwise(flip(qh), flip(kh), flip(vh), flip(i_bwd), flip(jax.nn.log_sigmoid(f_bwd))))
        hm = (h_fwd + h_bwd).transpose(0, 2, 1, 3)
        hm = rmsnorm(hm, mlstm_norm_g[l]).astype(x.dtype)
        mlstm_out = (jax.nn.sigmoid(o_m).reshape(B, S, HM, DM) * hm).reshape(B, S, MLSTM_WIDTH)

        mixed = jnp.concatenate([attn_out.astype(x.dtype), mlstm_out], axis=-1) @ w_out[l]
        x = x + gate1 * mixed

        h2 = modulate(rmsnorm(x, norm_mlp_g[l]), shift2, scale2)
        y = jnp.square(jax.nn.relu(h2 @ w_ff1[l])) @ w_ff2[l]
        x = x + gate2 * y
    return x
```

```python
import functools

import jax
import jax.numpy as jnp
from jax import lax
from jax.experimental import pallas as pl
from jax.experimental.pallas import tpu as pltpu

F32 = jnp.float32
BF16 = jnp.bfloat16

LANES = 128
N_MLA_HEADS = 8
MLA_NOPE = 128
ROPE_DIM = 64
ROPE_HALF = ROPE_DIM // 2
MLA_QK = MLA_NOPE + ROPE_DIM
MLA_QK_PAD = 256
MLA_V = 128
ROPE_THETA = 10000.0
N_MLSTM_HEADS = 4
MLSTM_DH = 256
MLSTM_CHUNK = 128
MLSTM_AUG = MLSTM_DH + LANES
CONV_WIDTH = 5
CONV_HALO = 8
EPS = 1e-6
M_INIT = -1e30
VMEM_LIMIT = 56 * 1024 * 1024


def _cparams(sem):
    return pltpu.CompilerParams(dimension_semantics=sem, vmem_limit_bytes=VMEM_LIMIT)


def _mod_kernel(c_ref, w_ref, b_ref, o_ref, sb_ref, *, tn):
    @pl.when(pl.program_id(0) == 0)
    def _():
        cc = c_ref[...]
        sb_ref[...] = jnp.broadcast_to(cc * jax.nn.sigmoid(cc), sb_ref.shape)

    sb = sb_ref[...]
    for j in range(tn // LANES):
        sl = slice(j * LANES, (j + 1) * LANES)
        o_ref[:, sl] = jnp.sum(w_ref[:, sl] * sb, axis=0, keepdims=True) + b_ref[:, sl]


def _mod(c_col, w_ada, b_ada, tn=512):
    d, n = w_ada.shape
    return pl.pallas_call(
        functools.partial(_mod_kernel, tn=tn),
        grid=(n // tn,),
        in_specs=[pl.BlockSpec((d, 1), lambda j: (0, 0)),
                  pl.BlockSpec((d, tn), lambda j: (0, j)),
                  pl.BlockSpec((1, tn), lambda j: (0, j))],
        out_specs=pl.BlockSpec((1, tn), lambda j: (0, j)),
        out_shape=jax.ShapeDtypeStruct((1, n), F32),
        scratch_shapes=[pltpu.VMEM((d, LANES), F32)],
        compiler_params=_cparams(("arbitrary",)),
        name="mod",
    )(c_col, w_ada, b_ada)


def _modulated_norm(x, g, scale, shift):
    ms = jnp.mean(x * x, axis=-1, keepdims=True)
    return (x * lax.rsqrt(ms + EPS) * g) * (1.0 + scale) + shift


def _inproj_kernel(x_ref, g_ref, sc_ref, sh_ref, w_ref, o_ref, gate_ref, h_ref, *, n_last):
    n = pl.program_id(1)

    @pl.when(n == 0)
    def _():
        h_ref[...] = _modulated_norm(x_ref[...], g_ref[...], sc_ref[...], sh_ref[...]).astype(BF16)

    acc = jnp.dot(h_ref[...], w_ref[...], preferred_element_type=F32)
    o_ref[...] = acc.astype(BF16)

    @pl.when(n == n_last)
    def _():
        gate_ref[...] = acc[:, -LANES:]


def _inproj(x, g, mod, w, tm=1024, tn=1024):
    s, d = x.shape
    n_tot = w.shape[1]
    nb = d // d
    del nb
    return pl.pallas_call(
        functools.partial(_inproj_kernel, n_last=n_tot // tn - 1),
        grid=(s // tm, n_tot // tn),
        in_specs=[pl.BlockSpec((tm, d), lambda m, n: (m, 0)),
                  pl.BlockSpec((1, d), lambda m, n: (0, 0)),
                  pl.BlockSpec((1, d), lambda m, n: (0, 1)),
                  pl.BlockSpec((1, d), lambda m, n: (0, 0)),
                  pl.BlockSpec((d, tn), lambda m, n: (0, n))],
        out_specs=[pl.BlockSpec((tm, tn), lambda m, n: (m, n)),
                   pl.BlockSpec((tm, LANES), lambda m, n: (m, 0))],
        out_shape=[jax.ShapeDtypeStruct((s, n_tot), BF16),
                   jax.ShapeDtypeStruct((s, LANES), F32)],
        scratch_shapes=[pltpu.VMEM((tm, d), BF16)],
        compiler_params=_cparams(("arbitrary", "arbitrary")),
        name="inproj",
    )(x, g, mod, mod, w)


def _rms_scale(x, n):
    return lax.rsqrt(jnp.sum(x * x, axis=-1, keepdims=True) * (1.0 / n) + EPS)


def _mlaprep_kernel(cq_ref, ckv_ref, kpe_ref, pos_ref, freq_ref, gql_ref, gkvl_ref, wuq_ref, wukv_ref,
                    gq_ref, gk_ref, q_ref, k_ref, v_ref):
    cq = cq_ref[...].astype(F32)
    cqn = (cq * _rms_scale(cq, cq.shape[-1]) * gql_ref[...]).astype(BF16)
    qf = jnp.dot(cqn, wuq_ref[...], preferred_element_type=F32)
    ckv = ckv_ref[...].astype(F32)
    ckvn = (ckv * _rms_scale(ckv, ckv.shape[-1]) * gkvl_ref[...]).astype(BF16)
    kvf = jnp.dot(ckvn, wukv_ref[...], preferred_element_type=F32)

    ang = pos_ref[...].astype(F32) * freq_ref[...]
    cosv, sinv = jnp.cos(ang), jnp.sin(ang)
    lane = lax.broadcasted_iota(jnp.int32, ang.shape, 1)
    c_tab = jnp.where(lane < ROPE_DIM, cosv, 0.0)
    s_up = jnp.where((lane >= ROPE_HALF) & (lane < ROPE_DIM), sinv, 0.0)
    s_dn = jnp.where(lane < ROPE_HALF, -sinv, 0.0)

    def rope(t):
        return (t * c_tab + pltpu.roll(t, ROPE_HALF, 1) * s_up
                + pltpu.roll(t, LANES - ROPE_HALF, 1) * s_dn)

    gq = gq_ref[...] * (MLA_QK ** -0.5)
    gk = gk_ref[...]
    kpe = kpe_ref[...].astype(F32)
    kpe_ssq = jnp.sum(kpe * kpe, axis=-1, keepdims=True)
    kpe_rot = rope(kpe * gk[:, MLA_NOPE:])
    for h in range(N_MLA_HEADS):
        qh = qf[:, h * MLA_QK_PAD:(h + 1) * MLA_QK_PAD]
        qn = qh * _rms_scale(qh, MLA_QK) * gq
        q_ref[h] = jnp.concatenate([qn[:, :MLA_NOPE], rope(qn[:, MLA_NOPE:])], axis=-1).astype(BF16)
        kn = kvf[:, h * 256:h * 256 + MLA_NOPE]
        r = lax.rsqrt((jnp.sum(kn * kn, axis=-1, keepdims=True) + kpe_ssq) * (1.0 / MLA_QK) + EPS)
        k_ref[h] = jnp.concatenate([kn * r * gk[:, :MLA_NOPE], kpe_rot * r], axis=-1).astype(BF16)
        v_ref[h] = kvf[:, h * 256 + MLA_NOPE:(h + 1) * 256].astype(BF16)


def _mlaprep(proj, pos_col, freq, gql, gkvl, wuq, wukv, gq, gk, tm=512):
    s = proj.shape[0]
    hq = N_MLA_HEADS
    full = lambda shape: pl.BlockSpec(shape, lambda i: (0,) * len(shape))
    return pl.pallas_call(
        _mlaprep_kernel,
        grid=(s // tm,),
        in_specs=[pl.BlockSpec((tm, 512), lambda i: (i, 8)),
                  pl.BlockSpec((tm, 256), lambda i: (i, 18)),
                  pl.BlockSpec((tm, LANES), lambda i: (i, 38)),
                  pl.BlockSpec((tm, 1), lambda i: (i, 0)),
                  full((1, LANES)), full((1, 512)), full((1, 256)),
                  full(wuq.shape), full(wukv.shape), full((1, MLA_QK_PAD)), full((1, MLA_QK_PAD))],
        out_specs=[pl.BlockSpec((hq, tm, MLA_QK_PAD), lambda i: (0, i, 0)),
                   pl.BlockSpec((hq, tm, MLA_QK_PAD), lambda i: (0, i, 0)),
                   pl.BlockSpec((hq, tm, MLA_V), lambda i: (0, i, 0))],
        out_shape=[jax.ShapeDtypeStruct((hq, s, MLA_QK_PAD), BF16),
                   jax.ShapeDtypeStruct((hq, s, MLA_QK_PAD), BF16),
                   jax.ShapeDtypeStruct((hq, s, MLA_V), BF16)],
        compiler_params=_cparams(("arbitrary",)),
        name="mlaprep",
    )(proj, proj, proj, pos_col, freq, gql, gkvl, wuq, wukv, gq, gk)


def _attn_kernel(q_ref, k_ref, v_ref, o_ref, *, tk):
    q = q_ref[0]
    tq = q.shape[0]
    n_chunks = k_ref.shape[1] // tk

    def body(j, carry):
        m, l, acc = carry
        start = pl.multiple_of(j * tk, tk)
        kc = k_ref[0, pl.ds(start, tk), :]
        vc = v_ref[0, pl.ds(start, tk), :]
        s = lax.dot_general(q, kc, (((1,), (1,)), ((), ())), preferred_element_type=F32)
        m_new = jnp.maximum(m, jnp.max(s, axis=-1, keepdims=True))
        alpha = jnp.exp(m - m_new)
        p = jnp.exp(s - m_new)
        l = alpha * l + jnp.sum(p, axis=-1, keepdims=True)
        acc = alpha * acc + jnp.dot(p.astype(BF16), vc, preferred_element_type=F32)
        return m_new, l, acc

    init = (jnp.full((tq, 1), -jnp.inf, F32), jnp.zeros((tq, 1), F32), jnp.zeros((tq, MLA_V), F32))
    _, l, acc = lax.fori_loop(0, n_chunks, body, init)
    o_ref[...] = (acc / l).astype(BF16)


def _attention(q, k, v, tq=512, tk=1024):
    hq, s, _ = q.shape
    return pl.pallas_call(
        functools.partial(_attn_kernel, tk=tk),
        grid=(hq, s // tq),
        in_specs=[pl.BlockSpec((1, tq, MLA_QK_PAD), lambda h, i: (h, i, 0)),
                  pl.BlockSpec((1, s, MLA_QK_PAD), lambda h, i: (h, 0, 0)),
                  pl.BlockSpec((1, s, MLA_V), lambda h, i: (h, 0, 0))],
        out_specs=pl.BlockSpec((tq, MLA_V), lambda h, i: (i, h)),
        out_shape=jax.ShapeDtypeStruct((s, hq * MLA_V), BF16),
        compiler_params=_cparams(("arbitrary", "arbitrary")),
        name="attn",
    )(q, k, v)


def _conv_kernel(x_ref, w_ref, b_ref, o_ref, pad_ref, *, rows, n_q_tiles):
    s = x_ref.shape[0]
    zeros = jnp.zeros((CONV_HALO, LANES), F32)
    pad_ref[0:CONV_HALO, :] = zeros
    pad_ref[CONV_HALO + s:CONV_HALO + s + CONV_HALO, :] = zeros
    pad_ref[CONV_HALO:CONV_HALO + s, :] = x_ref[...].astype(F32)
    out_scale = jnp.where(pl.program_id(0) >= n_q_tiles, MLSTM_DH ** -0.5, 1.0).astype(F32)
    w = w_ref[...]
    b = b_ref[...]

    def body(r, carry):
        base = pl.multiple_of(r * rows, rows)
        acc = b
        for j in range(CONV_WIDTH):
            off = CONV_HALO + j - CONV_WIDTH // 2
            acc = acc + w[j:j + 1, :] * pad_ref[pl.ds(base + off, rows), :]
        o_ref[pl.ds(base, rows), :] = (acc * jax.nn.sigmoid(acc) * out_scale).astype(BF16)
        return carry

    lax.fori_loop(0, s // rows, body, 0)


def _conv(proj, conv_w, conv_b, rows=256):
    s = proj.shape[0]
    n_ch = conv_w.shape[1]
    return pl.pallas_call(
        functools.partial(_conv_kernel, rows=rows, n_q_tiles=n_ch // 2 // LANES),
        grid=(n_ch // LANES,),
        in_specs=[pl.BlockSpec((s, LANES), lambda j: (0, j)),
                  pl.BlockSpec((CONV_WIDTH, LANES), lambda j: (0, j)),
                  pl.BlockSpec((1, LANES), lambda j: (0, j))],
        out_specs=pl.BlockSpec((s, LANES), lambda j: (0, j)),
        out_shape=jax.ShapeDtypeStruct((s, n_ch), BF16),
        scratch_shapes=[pltpu.VMEM((s + 2 * CONV_HALO, LANES), F32)],
        compiler_params=_cparams(("arbitrary",)),
        name="conv",
    )(proj, conv_w, conv_b)


def _mlstm_chain(q, k, v_aug, g, g_t, lf, b_c, b_ct, mask, ic, fc, c_ref, m_ref, chain):
    b_col, b_row = b_c[:, fc:fc + 1], b_ct[fc:fc + 1, :]
    i_col, i_row = g[:, ic:ic + 1], g_t[ic:ic + 1, :]
    b_tot = jnp.sum(lf[:, fc:fc + 1], axis=0, keepdims=True)
    m_prev = m_ref[chain:chain + 1, 0:1]

    log_d = jnp.where(mask, b_col - b_row + i_row, -jnp.inf)
    log_inter = b_col + m_prev
    m_t = jnp.maximum(log_inter, jnp.max(log_d, axis=-1, keepdims=True))
    d_m = jnp.exp(log_d - m_t)
    w_inter = jnp.exp(log_inter - m_t)

    c_aug = c_ref[chain]
    scores = lax.dot_general(q, k, (((1,), (1,)), ((), ())), preferred_element_type=F32) * d_m
    r = (jnp.dot(scores.astype(BF16), v_aug, preferred_element_type=F32)
         + w_inter * jnp.dot(q, c_aug.astype(BF16), preferred_element_type=F32))
    num, den = r[:, :MLSTM_DH], r[:, MLSTM_DH:MLSTM_DH + 1]
    h = num / jnp.maximum(jnp.abs(den), jnp.exp(-m_t))

    log_w = b_tot - b_col + i_col
    m_new = jnp.maximum(b_tot + m_prev, jnp.max(log_w, axis=0, keepdims=True))
    decay = jnp.exp(b_tot + m_prev - m_new)
    kw = (k.astype(F32) * jnp.exp(log_w - m_new)).astype(BF16)
    upd = lax.dot_general(kw, v_aug, (((0,), (0,)), ((), ())), preferred_element_type=F32)
    c_ref[chain] = decay * c_aug + upd
    m_ref[chain:chain + 1, :] = jnp.broadcast_to(m_new, (1, LANES))
    return h


def _mlstm_kernel(qf_ref, kf_ref, vf_ref, gf_ref, qb_ref, kb_ref, vb_ref, gb_ref, bg_ref,
                  hf_ref, hb_ref, c_ref, m_ref):
    @pl.when(pl.program_id(0) == 0)
    def _():
        c_ref[...] = jnp.zeros(c_ref.shape, F32)
        m_ref[...] = jnp.full(m_ref.shape, M_INIT, F32)

    L = MLSTM_CHUNK
    row = lax.broadcasted_iota(jnp.int32, (L, L), 0)
    col = lax.broadcasted_iota(jnp.int32, (L, L), 1)
    ones_col = (lax.broadcasted_iota(jnp.int32, (L, LANES), 1) == 0).astype(BF16)
    hm = N_MLSTM_HEADS
    for d, (q_ref, k_ref, v_ref, g_ref, o_ref) in enumerate(
            ((qf_ref, kf_ref, vf_ref, gf_ref, hf_ref), (qb_ref, kb_ref, vb_ref, gb_ref, hb_ref))):
        mask = (row >= col) if d == 0 else (row <= col)
        g = g_ref[...] + bg_ref[...]
        lf = jax.nn.log_sigmoid(g)
        b_c = jnp.dot(mask.astype(F32), lf, preferred_element_type=F32,
                      precision=lax.Precision.HIGHEST)
        g_t, b_ct = g.T, b_c.T
        for h in range(hm):
            sl = slice(h * MLSTM_DH, (h + 1) * MLSTM_DH)
            v_aug = jnp.concatenate([v_ref[:, sl], ones_col], axis=-1)
            o_ref[:, sl] = _mlstm_chain(q_ref[:, sl], k_ref[:, sl], v_aug, g, g_t, lf, b_c, b_ct, mask,
                                        2 * d * hm + h, (2 * d + 1) * hm + h, c_ref, m_ref, d * hm + h)


def _mlstm(qk, proj, gates, bg):
    s = qk.shape[0]
    L = MLSTM_CHUNK
    nc = s // L
    w = N_MLSTM_HEADS * MLSTM_DH
    fwd = lambda blk: (lambda i: (i, blk))
    bwd = lambda blk: (lambda i: (nc - 1 - i, blk))
    specs = lambda ix: [pl.BlockSpec((L, w), ix(0)), pl.BlockSpec((L, w), ix(1)),
                        pl.BlockSpec((L, w), ix(2)), pl.BlockSpec((L, LANES), ix(0))]
    return pl.pallas_call(
        _mlstm_kernel,
        grid=(nc,),
        in_specs=specs(fwd) + specs(bwd) + [pl.BlockSpec((1, LANES), lambda i: (0, 0))],
        out_specs=[pl.BlockSpec((L, w), fwd(0)), pl.BlockSpec((L, w), bwd(0))],
        out_shape=[jax.ShapeDtypeStruct((s, w), F32), jax.ShapeDtypeStruct((s, w), F32)],
        scratch_shapes=[pltpu.VMEM((2 * N_MLSTM_HEADS, MLSTM_DH, MLSTM_AUG), F32),
                        pltpu.VMEM((2 * N_MLSTM_HEADS, LANES), F32)],
        compiler_params=_cparams(("arbitrary",)),
        name="mlstm",
    )(qk, qk, proj, gates, qk, qk, proj, gates, bg)


def _outproj_kernel(attn_ref, hf_ref, hb_ref, om_ref, x_ref, gm_ref, w_ref, gate_ref, g2_ref, sc_ref, sh_ref,
                    x1_ref, h2_ref):
    hsum = hf_ref[...] + hb_ref[...]
    gm = gm_ref[...]
    parts = []
    for h in range(N_MLSTM_HEADS):
        sl = slice(h * MLSTM_DH, (h + 1) * MLSTM_DH)
        seg = hsum[:, sl]
        parts.append(seg * _rms_scale(seg, MLSTM_DH) * gm[:, sl])
    ml = (jax.nn.sigmoid(om_ref[...].astype(F32)) * jnp.concatenate(parts, axis=-1)).astype(BF16)
    n_attn = attn_ref.shape[1]
    mixed = (jnp.dot(attn_ref[...], w_ref[:n_attn, :], preferred_element_type=F32)
             + jnp.dot(ml, w_ref[n_attn:, :], preferred_element_type=F32))
    x1 = x_ref[...] + gate_ref[...] * mixed
    x1_ref[...] = x1
    h2_ref[...] = _modulated_norm(x1, g2_ref[...], sc_ref[...], sh_ref[...]).astype(BF16)


def _outproj(attn, hf, hb, proj, x, gm, w_out, mod, g2, tm=256):
    s, d = x.shape
    wm = hf.shape[1]
    row = lambda blk: pl.BlockSpec((1, d), lambda i: (0, blk))
    return pl.pallas_call(
        _outproj_kernel,
        grid=(s // tm,),
        in_specs=[pl.BlockSpec((tm, attn.shape[1]), lambda i: (i, 0)),
                  pl.BlockSpec((tm, wm), lambda i: (i, 0)),
                  pl.BlockSpec((tm, wm), lambda i: (i, 0)),
                  pl.BlockSpec((tm, wm), lambda i: (i, 3)),
                  pl.BlockSpec((tm, d), lambda i: (i, 0)),
                  pl.BlockSpec((1, wm), lambda i: (0, 0)),
                  pl.BlockSpec(w_out.shape, lambda i: (0, 0)),
                  row(2), row(0), row(4), row(3)],
        out_specs=[pl.BlockSpec((tm, d), lambda i: (i, 0)), pl.BlockSpec((tm, d), lambda i: (i, 0))],
        out_shape=[jax.ShapeDtypeStruct((s, d), F32), jax.ShapeDtypeStruct((s, d), BF16)],
        compiler_params=_cparams(("arbitrary",)),
        name="outproj",
    )(attn, hf, hb, proj, x, gm, w_out, mod, g2, mod, mod)


def _ffn_kernel(h2_ref, w1_ref, w2_ref, x1_ref, gate_ref, o_ref, acc_ref, *, f_last):
    f = pl.program_id(1)

    @pl.when(f == 0)
    def _():
        acc_ref[...] = jnp.zeros(acc_ref.shape, F32)

    a = jnp.maximum(jnp.dot(h2_ref[...], w1_ref[...], preferred_element_type=F32), 0.0)
    acc_ref[...] += jnp.dot((a * a).astype(BF16), w2_ref[...], preferred_element_type=F32)

    @pl.when(f == f_last)
    def _():
        o_ref[...] = x1_ref[...] + gate_ref[...] * acc_ref[...]


def _ffn(h2, w1, w2, x1, mod, tm=512, tf=512):
    s, d = h2.shape
    dff = w1.shape[1]
    return pl.pallas_call(
        functools.partial(_ffn_kernel, f_last=dff // tf - 1),
        grid=(s // tm, dff // tf),
        in_specs=[pl.BlockSpec((tm, d), lambda m, f: (m, 0)),
                  pl.BlockSpec((d, tf), lambda m, f: (0, f)),
                  pl.BlockSpec((tf, d), lambda m, f: (f, 0)),
                  pl.BlockSpec((tm, d), lambda m, f: (m, 0)),
                  pl.BlockSpec((1, d), lambda m, f: (0, 5))],
        out_specs=pl.BlockSpec((tm, d), lambda m, f: (m, 0)),
        out_shape=jax.ShapeDtypeStruct((s, d), F32),
        scratch_shapes=[pltpu.VMEM((tm, d), F32)],
        compiler_params=_cparams(("arbitrary", "arbitrary")),
        name="ffn",
    )(h2, w1, w2, x1, mod)


def _pack_w_in(w_in):
    d = w_in.shape[0]
    c_q, c_kv, k_pe, q_m, k_m, v_m, o_m, g_m = jnp.split(
        w_in, (512, 768, 832, 1856, 2880, 3904, 4928), axis=1)
    z = lambda n: jnp.zeros((d, n), w_in.dtype)
    return jnp.concatenate([q_m, k_m, v_m, o_m, c_q, c_kv, k_pe, z(LANES - k_pe.shape[1]),
                            g_m, z(LANES - g_m.shape[1])], axis=1).astype(BF16)


def _pad_lanes(v, n):
    return jnp.pad(v, ((0, 0), (0, n - v.shape[1])))


def kernel(x, c, positions, w_ada, b_ada, norm_mix_g, w_in, b_gates, conv_w, conv_b, q_lora_g, w_uq,
           kv_lora_g, w_ukv, q_norm_g, k_norm_g, mlstm_norm_g, w_out, norm_mlp_g, w_ff1, w_ff2):
    bsz, s, d = x.shape
    assert bsz == 1, "kernels are written for a single sequence"
    xs = x[0]
    pos_col = positions.reshape(s, 1)
    half = jnp.arange(ROPE_HALF, dtype=F32)
    freqs = ROPE_THETA ** (-half / ROPE_HALF)
    freq = _pad_lanes(jnp.concatenate([freqs, freqs])[None, :], LANES)
    row = lambda v: v.reshape(1, -1).astype(F32)

    for l in range(w_ada.shape[0]):
        mod = _mod(c.reshape(d, 1), w_ada[l], row(b_ada[l]))
        proj, gates = _inproj(xs, row(norm_mix_g[l]), mod, _pack_w_in(w_in[l]))

        wuq = jnp.pad(w_uq[l].reshape(-1, N_MLA_HEADS, MLA_QK),
                      ((0, 0), (0, 0), (0, MLA_QK_PAD - MLA_QK))).reshape(-1, N_MLA_HEADS * MLA_QK_PAD)
        q, k, v = _mlaprep(proj, pos_col, freq, row(q_lora_g[l]), row(kv_lora_g[l]),
                           wuq.astype(BF16), w_ukv[l].astype(BF16),
                           _pad_lanes(row(q_norm_g[l]), MLA_QK_PAD), _pad_lanes(row(k_norm_g[l]), MLA_QK_PAD))
        attn = _attention(q, k, v)

        qk = _conv(proj, conv_w[l], row(conv_b[l]))
        hf, hb = _mlstm(qk, proj, gates, _pad_lanes(row(b_gates[l]), LANES))

        x1, h2 = _outproj(attn, hf, hb, proj, xs, row(mlstm_norm_g[l]), w_out[l].astype(BF16), mod,
                          row(norm_mlp_g[l]))
        xs = _ffn(h2, w_ff1[l].astype(BF16), w_ff2[l].astype(BF16), x1, mod)
    return xs[None]
```

```python
import functools

import jax
import jax.numpy as jnp
from jax import lax
from jax.experimental import pallas as pl
from jax.experimental.pallas import tpu as pltpu

F32 = jnp.float32
BF16 = jnp.bfloat16

LANES = 128
N_MLA_HEADS = 8
MLA_NOPE = 128
ROPE_DIM = 64
ROPE_HALF = ROPE_DIM // 2
MLA_QK = MLA_NOPE + ROPE_DIM
MLA_QK_PAD = 256
MLA_V = 128
ROPE_THETA = 10000.0
N_MLSTM_HEADS = 4
MLSTM_DH = 256
MLSTM_CHUNK = 128
MLSTM_AUG = MLSTM_DH + LANES
CONV_WIDTH = 5
CONV_HALO = 8
EPS = 1e-6
M_INIT = -1e30
LOG2_E = 1.4426950408889634
SAFE_LOG2 = 60.0
VMEM_LIMIT = 56 * 1024 * 1024


def _cparams(sem):
    return pltpu.CompilerParams(dimension_semantics=sem, vmem_limit_bytes=VMEM_LIMIT)


def _mod_kernel(c_ref, w_ref, b_ref, o_ref, sb_ref, *, tn):
    @pl.when(pl.program_id(0) == 0)
    def _():
        cc = c_ref[...]
        sb_ref[...] = jnp.broadcast_to(cc * jax.nn.sigmoid(cc), sb_ref.shape)

    sb = sb_ref[...]
    for j in range(tn // LANES):
        sl = slice(j * LANES, (j + 1) * LANES)
        o_ref[:, sl] = jnp.sum(w_ref[:, sl] * sb, axis=0, keepdims=True) + b_ref[:, sl]


def _mod(c_col, w_ada, b_ada, tn=512):
    d, n = w_ada.shape
    return pl.pallas_call(
        functools.partial(_mod_kernel, tn=tn),
        grid=(n // tn,),
        in_specs=[pl.BlockSpec((d, 1), lambda j: (0, 0)),
                  pl.BlockSpec((d, tn), lambda j: (0, j)),
                  pl.BlockSpec((1, tn), lambda j: (0, j))],
        out_specs=pl.BlockSpec((1, tn), lambda j: (0, j)),
        out_shape=jax.ShapeDtypeStruct((1, n), F32),
        scratch_shapes=[pltpu.VMEM((d, LANES), F32)],
        compiler_params=_cparams(("arbitrary",)),
        name="mod",
    )(c_col, w_ada, b_ada)


def _modulated_norm(x, g, scale, shift):
    ms = jnp.mean(x * x, axis=-1, keepdims=True)
    return (x * lax.rsqrt(ms + EPS) * g) * (1.0 + scale) + shift


def _inproj_kernel(x_ref, g_ref, sc_ref, sh_ref, w_ref, o_ref, gate_ref, h_ref, *, n_last):
    n = pl.program_id(1)

    @pl.when(n == 0)
    def _():
        h_ref[...] = _modulated_norm(x_ref[...], g_ref[...], sc_ref[...], sh_ref[...]).astype(BF16)

    acc = jnp.dot(h_ref[...], w_ref[...], preferred_element_type=F32)
    o_ref[...] = acc.astype(BF16)

    @pl.when(n == n_last)
    def _():
        gate_ref[...] = acc[:, -LANES:]


def _inproj(x, g, mod, w, tm=1024, tn=1024):
    s, d = x.shape
    n_tot = w.shape[1]
    nb = d // d
    del nb
    return pl.pallas_call(
        functools.partial(_inproj_kernel, n_last=n_tot // tn - 1),
        grid=(s // tm, n_tot // tn),
        in_specs=[pl.BlockSpec((tm, d), lambda m, n: (m, 0)),
                  pl.BlockSpec((1, d), lambda m, n: (0, 0)),
                  pl.BlockSpec((1, d), lambda m, n: (0, 1)),
                  pl.BlockSpec((1, d), lambda m, n: (0, 0)),
                  pl.BlockSpec((d, tn), lambda m, n: (0, n))],
        out_specs=[pl.BlockSpec((tm, tn), lambda m, n: (m, n)),
                   pl.BlockSpec((tm, LANES), lambda m, n: (m, 0))],
        out_shape=[jax.ShapeDtypeStruct((s, n_tot), BF16),
                   jax.ShapeDtypeStruct((s, LANES), F32)],
        scratch_shapes=[pltpu.VMEM((tm, d), BF16)],
        compiler_params=_cparams(("arbitrary", "arbitrary")),
        name="inproj",
    )(x, g, mod, mod, w)


def _rms_scale(x, n):
    return lax.rsqrt(jnp.sum(x * x, axis=-1, keepdims=True) * (1.0 / n) + EPS)


def _mlaprep_kernel(cq_ref, ckv_ref, kpe_ref, pos_ref, freq_ref, gql_ref, gkvl_ref, wuq_ref, wukv_ref,
                    gq_ref, gk_ref, q_ref, k_ref, v_ref):
    cq = cq_ref[...].astype(F32)
    cqn = (cq * _rms_scale(cq, cq.shape[-1]) * gql_ref[...]).astype(BF16)
    qf = jnp.dot(cqn, wuq_ref[...], preferred_element_type=F32)
    ckv = ckv_ref[...].astype(F32)
    ckvn = (ckv * _rms_scale(ckv, ckv.shape[-1]) * gkvl_ref[...]).astype(BF16)
    kvf = jnp.dot(ckvn, wukv_ref[...], preferred_element_type=F32)

    ang = pos_ref[...].astype(F32) * freq_ref[...]
    cosv, sinv = jnp.cos(ang), jnp.sin(ang)
    lane = lax.broadcasted_iota(jnp.int32, ang.shape, 1)
    c_tab = jnp.where(lane < ROPE_DIM, cosv, 0.0)
    s_up = jnp.where((lane >= ROPE_HALF) & (lane < ROPE_DIM), sinv, 0.0)
    s_dn = jnp.where(lane < ROPE_HALF, -sinv, 0.0)

    def rope(t):
        return (t * c_tab + pltpu.roll(t, ROPE_HALF, 1) * s_up
                + pltpu.roll(t, LANES - ROPE_HALF, 1) * s_dn)

    gq = gq_ref[...] * (LOG2_E * MLA_QK ** -0.5)
    ones_col = (lax.broadcasted_iota(jnp.int32, (kpe_ref.shape[0], MLA_V), 1) == 0).astype(BF16)
    gk = gk_ref[...]
    kpe = kpe_ref[...].astype(F32)
    kpe_ssq = jnp.sum(kpe * kpe, axis=-1, keepdims=True)
    kpe_rot = rope(kpe * gk[:, MLA_NOPE:])
    for h in range(N_MLA_HEADS):
        qh = qf[:, h * MLA_QK_PAD:(h + 1) * MLA_QK_PAD]
        qn = qh * _rms_scale(qh, MLA_QK) * gq
        q_ref[h] = jnp.concatenate([qn[:, :MLA_NOPE], rope(qn[:, MLA_NOPE:])], axis=-1).astype(BF16)
        kn = kvf[:, h * 256:h * 256 + MLA_NOPE]
        r = lax.rsqrt((jnp.sum(kn * kn, axis=-1, keepdims=True) + kpe_ssq) * (1.0 / MLA_QK) + EPS)
        k_ref[h] = jnp.concatenate([kn * r * gk[:, :MLA_NOPE], kpe_rot * r], axis=-1).astype(BF16)
        v_ref[h] = jnp.concatenate([kvf[:, h * 256 + MLA_NOPE:(h + 1) * 256].astype(BF16), ones_col], axis=-1)


def _mlaprep(proj, pos_col, freq, gql, gkvl, wuq, wukv, gq, gk, tm=512):
    s = proj.shape[0]
    hq = N_MLA_HEADS
    full = lambda shape: pl.BlockSpec(shape, lambda i: (0,) * len(shape))
    return pl.pallas_call(
        _mlaprep_kernel,
        grid=(s // tm,),
        in_specs=[pl.BlockSpec((tm, 512), lambda i: (i, 8)),
                  pl.BlockSpec((tm, 256), lambda i: (i, 18)),
                  pl.BlockSpec((tm, LANES), lambda i: (i, 38)),
                  pl.BlockSpec((tm, 1), lambda i: (i, 0)),
                  full((1, LANES)), full((1, 512)), full((1, 256)),
                  full(wuq.shape), full(wukv.shape), full((1, MLA_QK_PAD)), full((1, MLA_QK_PAD))],
        out_specs=[pl.BlockSpec((hq, tm, MLA_QK_PAD), lambda i: (0, i, 0)),
                   pl.BlockSpec((hq, tm, MLA_QK_PAD), lambda i: (0, i, 0)),
                   pl.BlockSpec((hq, tm, 2 * MLA_V), lambda i: (0, i, 0))],
        out_shape=[jax.ShapeDtypeStruct((hq, s, MLA_QK_PAD), BF16),
                   jax.ShapeDtypeStruct((hq, s, MLA_QK_PAD), BF16),
                   jax.ShapeDtypeStruct((hq, s, 2 * MLA_V), BF16)],
        compiler_params=_cparams(("arbitrary",)),
        name="mlaprep",
    )(proj, proj, proj, pos_col, freq, gql, gkvl, wuq, wukv, gq, gk)


def _max_sq_norm(x):
    xf = x.astype(F32)
    return jnp.max(jnp.sum(xf * xf, axis=-1, keepdims=True))


def _attn_kernel(q_ref, k_ref, v_ref, o_ref, kn_ref, *, tk, unroll, tk_slow):
    n_keys = k_ref.shape[1]

    @pl.when(pl.program_id(1) == 0)
    def _():
        def body(j, m):
            start = pl.multiple_of(j * tk, tk)
            return jnp.maximum(m, _max_sq_norm(k_ref[0, pl.ds(start, tk), :]))
        kn_ref[0] = lax.fori_loop(0, n_keys // tk, body, jnp.float32(0.0))

    q = q_ref[0]
    tq = q.shape[0]
    bound_sq = _max_sq_norm(q) * kn_ref[0]
    nt = (((1,), (1,)), ((), ()))

    def fast():
        def logits(j):
            start = pl.multiple_of(j * tk, tk)
            return lax.dot_general(q, k_ref[0, pl.ds(start, tk), :], nt, preferred_element_type=F32)

        def weighted(j, s2):
            start = pl.multiple_of(j * tk, tk)
            return jnp.dot(jnp.exp2(s2).astype(BF16), v_ref[0, pl.ds(start, tk), :],
                           preferred_element_type=F32)

        def body(j, acc):
            for u in range(unroll):
                acc = acc + weighted(j * unroll + u, logits(j * unroll + u))
            return acc

        return lax.fori_loop(0, n_keys // (tk * unroll), body, jnp.zeros((tq, 2 * MLA_V), F32))

    def slow():
        def body(j, carry):
            m, acc = carry
            start = pl.multiple_of(j * tk_slow, tk_slow)
            s2 = lax.dot_general(q, k_ref[0, pl.ds(start, tk_slow), :], nt, preferred_element_type=F32)
            m_new = jnp.maximum(m, jnp.max(s2, axis=-1, keepdims=True))
            p = jnp.exp2(s2 - m_new).astype(BF16)
            acc = jnp.exp2(m - m_new) * acc + jnp.dot(p, v_ref[0, pl.ds(start, tk_slow), :],
                                                      preferred_element_type=F32)
            return m_new, acc
        init = (jnp.full((tq, 1), -jnp.inf, F32), jnp.zeros((tq, 2 * MLA_V), F32))
        return lax.fori_loop(0, n_keys // tk_slow, body, init)[1]

    acc = lax.cond(bound_sq <= SAFE_LOG2 * SAFE_LOG2, fast, slow)
    o_ref[...] = (acc[:, :MLA_V] / acc[:, MLA_V:MLA_V + 1]).astype(BF16)


def _attention(q, k, v, tq=512, tk=256, unroll=32, tk_slow=1024):
    hq, s, _ = q.shape
    return pl.pallas_call(
        functools.partial(_attn_kernel, tk=tk, unroll=unroll, tk_slow=tk_slow),
        grid=(hq, s // tq),
        in_specs=[pl.BlockSpec((1, tq, MLA_QK_PAD), lambda h, i: (h, i, 0)),
                  pl.BlockSpec((1, s, MLA_QK_PAD), lambda h, i: (h, 0, 0)),
                  pl.BlockSpec((1, s, 2 * MLA_V), lambda h, i: (h, 0, 0))],
        out_specs=pl.BlockSpec((tq, MLA_V), lambda h, i: (i, h)),
        out_shape=jax.ShapeDtypeStruct((s, hq * MLA_V), BF16),
        scratch_shapes=[pltpu.SMEM((1,), F32)],
        compiler_params=_cparams(("arbitrary", "arbitrary")),
        name="attn",
    )(q, k, v)


def _conv_kernel(x_ref, w_ref, b_ref, o_ref, pad_ref, *, rows, n_q_tiles):
    s = x_ref.shape[0]
    zeros = jnp.zeros((CONV_HALO, LANES), F32)
    pad_ref[0:CONV_HALO, :] = zeros
    pad_ref[CONV_HALO + s:CONV_HALO + s + CONV_HALO, :] = zeros
    pad_ref[CONV_HALO:CONV_HALO + s, :] = x_ref[...].astype(F32)
    out_scale = jnp.where(pl.program_id(0) >= n_q_tiles, MLSTM_DH ** -0.5, 1.0).astype(F32)
    w = w_ref[...]
    b = b_ref[...]

    def body(r, carry):
        base = pl.multiple_of(r * rows, rows)
        acc = b
        for j in range(CONV_WIDTH):
            off = CONV_HALO + j - CONV_WIDTH // 2
            acc = acc + w[j:j + 1, :] * pad_ref[pl.ds(base + off, rows), :]
        o_ref[pl.ds(base, rows), :] = (acc * jax.nn.sigmoid(acc) * out_scale).astype(BF16)
        return carry

    lax.fori_loop(0, s // rows, body, 0)


def _conv(proj, conv_w, conv_b, rows=256):
    s = proj.shape[0]
    n_ch = conv_w.shape[1]
    return pl.pallas_call(
        functools.partial(_conv_kernel, rows=rows, n_q_tiles=n_ch // 2 // LANES),
        grid=(n_ch // LANES,),
        in_specs=[pl.BlockSpec((s, LANES), lambda j: (0, j)),
                  pl.BlockSpec((CONV_WIDTH, LANES), lambda j: (0, j)),
                  pl.BlockSpec((1, LANES), lambda j: (0, j))],
        out_specs=pl.BlockSpec((s, LANES), lambda j: (0, j)),
        out_shape=jax.ShapeDtypeStruct((s, n_ch), BF16),
        scratch_shapes=[pltpu.VMEM((s + 2 * CONV_HALO, LANES), F32)],
        compiler_params=_cparams(("arbitrary",)),
        name="conv",
    )(proj, conv_w, conv_b)


def _mlstm_chain(q, k, v_aug, g, g_t, lf, b_c, b_ct, mask, ic, fc, c_ref, m_ref, chain):
    b_col, b_row = b_c[:, fc:fc + 1], b_ct[fc:fc + 1, :]
    i_col, i_row = g[:, ic:ic + 1], g_t[ic:ic + 1, :]
    b_tot = jnp.sum(lf[:, fc:fc + 1], axis=0, keepdims=True)
    m_prev = m_ref[chain:chain + 1, 0:1]

    log_d = jnp.where(mask, b_col - b_row + i_row, -jnp.inf)
    log_inter = b_col + m_prev
    m_t = jnp.maximum(log_inter, jnp.max(log_d, axis=-1, keepdims=True))
    d_m = jnp.exp(log_d - m_t)
    w_inter = jnp.exp(log_inter - m_t)

    c_aug = c_ref[chain]
    scores = lax.dot_general(q, k, (((1,), (1,)), ((), ())), preferred_element_type=F32) * d_m
    r = (jnp.dot(scores.astype(BF16), v_aug, preferred_element_type=F32)
         + w_inter * jnp.dot(q, c_aug.astype(BF16), preferred_element_type=F32))
    num, den = r[:, :MLSTM_DH], r[:, MLSTM_DH:MLSTM_DH + 1]
    h = num / jnp.maximum(jnp.abs(den), jnp.exp(-m_t))

    log_w = b_tot - b_col + i_col
    m_new = jnp.maximum(b_tot + m_prev, jnp.max(log_w, axis=0, keepdims=True))
    decay = jnp.exp(b_tot + m_prev - m_new)
    kw = (k.astype(F32) * jnp.exp(log_w - m_new)).astype(BF16)
    upd = lax.dot_general(kw, v_aug, (((0,), (0,)), ((), ())), preferred_element_type=F32)
    c_ref[chain] = decay * c_aug + upd
    m_ref[chain:chain + 1, :] = jnp.broadcast_to(m_new, (1, LANES))
    return h


def _mlstm_kernel(qf_ref, kf_ref, vf_ref, gf_ref, qb_ref, kb_ref, vb_ref, gb_ref, bg_ref,
                  hf_ref, hb_ref, c_ref, m_ref):
    @pl.when(pl.program_id(0) == 0)
    def _():
        c_ref[...] = jnp.zeros(c_ref.shape, F32)
        m_ref[...] = jnp.full(m_ref.shape, M_INIT, F32)

    L = MLSTM_CHUNK
    row = lax.broadcasted_iota(jnp.int32, (L, L), 0)
    col = lax.broadcasted_iota(jnp.int32, (L, L), 1)
    ones_col = (lax.broadcasted_iota(jnp.int32, (L, LANES), 1) == 0).astype(BF16)
    hm = N_MLSTM_HEADS
    for d, (q_ref, k_ref, v_ref, g_ref, o_ref) in enumerate(
            ((qf_ref, kf_ref, vf_ref, gf_ref, hf_ref), (qb_ref, kb_ref, vb_ref, gb_ref, hb_ref))):
        mask = (row >= col) if d == 0 else (row <= col)
        g = g_ref[...] + bg_ref[...]
        lf = jax.nn.log_sigmoid(g)
        b_c = jnp.dot(mask.astype(F32), lf, preferred_element_type=F32,
                      precision=lax.Precision.HIGHEST)
        g_t, b_ct = g.T, b_c.T
        for h in range(hm):
            sl = slice(h * MLSTM_DH, (h + 1) * MLSTM_DH)
            v_aug = jnp.concatenate([v_ref[:, sl], ones_col], axis=-1)
            o_ref[:, sl] = _mlstm_chain(q_ref[:, sl], k_ref[:, sl], v_aug, g, g_t, lf, b_c, b_ct, mask,
                                        2 * d * hm + h, (2 * d + 1) * hm + h, c_ref, m_ref, d * hm + h)


def _mlstm(qk, proj, gates, bg):
    s = qk.shape[0]
    L = MLSTM_CHUNK
    nc = s // L
    w = N_MLSTM_HEADS * MLSTM_DH
    fwd = lambda blk: (lambda i: (i, blk))
    bwd = lambda blk: (lambda i: (nc - 1 - i, blk))
    specs = lambda ix: [pl.BlockSpec((L, w), ix(0)), pl.BlockSpec((L, w), ix(1)),
                        pl.BlockSpec((L, w), ix(2)), pl.BlockSpec((L, LANES), ix(0))]
    return pl.pallas_call(
        _mlstm_kernel,
        grid=(nc,),
        in_specs=specs(fwd) + specs(bwd) + [pl.BlockSpec((1, LANES), lambda i: (0, 0))],
        out_specs=[pl.BlockSpec((L, w), fwd(0)), pl.BlockSpec((L, w), bwd(0))],
        out_shape=[jax.ShapeDtypeStruct((s, w), F32), jax.ShapeDtypeStruct((s, w), F32)],
        scratch_shapes=[pltpu.VMEM((2 * N_MLSTM_HEADS, MLSTM_DH, MLSTM_AUG), F32),
                        pltpu.VMEM((2 * N_MLSTM_HEADS, LANES), F32)],
        compiler_params=_cparams(("arbitrary",)),
        name="mlstm",
    )(qk, qk, proj, gates, qk, qk, proj, gates, bg)


def _outproj_kernel(attn_ref, hf_ref, hb_ref, om_ref, x_ref, gm_ref, w_ref, gate_ref, g2_ref, sc_ref, sh_ref,
                    x1_ref, h2_ref):
    hsum = hf_ref[...] + hb_ref[...]
    gm = gm_ref[...]
    parts = []
    for h in range(N_MLSTM_HEADS):
        sl = slice(h * MLSTM_DH, (h + 1) * MLSTM_DH)
        seg = hsum[:, sl]
        parts.append(seg * _rms_scale(seg, MLSTM_DH) * gm[:, sl])
    ml = (jax.nn.sigmoid(om_ref[...].astype(F32)) * jnp.concatenate(parts, axis=-1)).astype(BF16)
    n_attn = attn_ref.shape[1]
    mixed = (jnp.dot(attn_ref[...], w_ref[:n_attn, :], preferred_element_type=F32)
             + jnp.dot(ml, w_ref[n_attn:, :], preferred_element_type=F32))
    x1 = x_ref[...] + gate_ref[...] * mixed
    x1_ref[...] = x1
    h2_ref[...] = _modulated_norm(x1, g2_ref[...], sc_ref[...], sh_ref[...]).astype(BF16)


def _outproj(attn, hf, hb, proj, x, gm, w_out, mod, g2, tm=256):
    s, d = x.shape
    wm = hf.shape[1]
    row = lambda blk: pl.BlockSpec((1, d), lambda i: (0, blk))
    return pl.pallas_call(
        _outproj_kernel,
        grid=(s // tm,),
        in_specs=[pl.BlockSpec((tm, attn.shape[1]), lambda i: (i, 0)),
                  pl.BlockSpec((tm, wm), lambda i: (i, 0)),
                  pl.BlockSpec((tm, wm), lambda i: (i, 0)),
                  pl.BlockSpec((tm, wm), lambda i: (i, 3)),
                  pl.BlockSpec((tm, d), lambda i: (i, 0)),
                  pl.BlockSpec((1, wm), lambda i: (0, 0)),
                  pl.BlockSpec(w_out.shape, lambda i: (0, 0)),
                  row(2), row(0), row(4), row(3)],
        out_specs=[pl.BlockSpec((tm, d), lambda i: (i, 0)), pl.BlockSpec((tm, d), lambda i: (i, 0))],
        out_shape=[jax.ShapeDtypeStruct((s, d), F32), jax.ShapeDtypeStruct((s, d), BF16)],
        compiler_params=_cparams(("arbitrary",)),
        name="outproj",
    )(attn, hf, hb, proj, x, gm, w_out, mod, g2, mod, mod)


def _ffn_kernel(h2_ref, w1_ref, w2_ref, x1_ref, gate_ref, o_ref, acc_ref, *, f_last):
    f = pl.program_id(1)

    @pl.when(f == 0)
    def _():
        acc_ref[...] = jnp.zeros(acc_ref.shape, F32)

    a = jnp.maximum(jnp.dot(h2_ref[...], w1_ref[...], preferred_element_type=F32), 0.0)
    acc_ref[...] += jnp.dot((a * a).astype(BF16), w2_ref[...], preferred_element_type=F32)

    @pl.when(f == f_last)
    def _():
        o_ref[...] = x1_ref[...] + gate_ref[...] * acc_ref[...]


def _ffn(h2, w1, w2, x1, mod, tm=512, tf=512):
    s, d = h2.shape
    dff = w1.shape[1]
    return pl.pallas_call(
        functools.partial(_ffn_kernel, f_last=dff // tf - 1),
        grid=(s // tm, dff // tf),
        in_specs=[pl.BlockSpec((tm, d), lambda m, f: (m, 0)),
                  pl.BlockSpec((d, tf), lambda m, f: (0, f)),
                  pl.BlockSpec((tf, d), lambda m, f: (f, 0)),
                  pl.BlockSpec((tm, d), lambda m, f: (m, 0)),
                  pl.BlockSpec((1, d), lambda m, f: (0, 5))],
        out_specs=pl.BlockSpec((tm, d), lambda m, f: (m, 0)),
        out_shape=jax.ShapeDtypeStruct((s, d), F32),
        scratch_shapes=[pltpu.VMEM((tm, d), F32)],
        compiler_params=_cparams(("arbitrary", "arbitrary")),
        name="ffn",
    )(h2, w1, w2, x1, mod)


def _pack_w_in(w_in):
    d = w_in.shape[0]
    c_q, c_kv, k_pe, q_m, k_m, v_m, o_m, g_m = jnp.split(
        w_in, (512, 768, 832, 1856, 2880, 3904, 4928), axis=1)
    z = lambda n: jnp.zeros((d, n), w_in.dtype)
    return jnp.concatenate([q_m, k_m, v_m, o_m, c_q, c_kv, k_pe, z(LANES - k_pe.shape[1]),
                            g_m, z(LANES - g_m.shape[1])], axis=1).astype(BF16)


def _pad_lanes(v, n):
    return jnp.pad(v, ((0, 0), (0, n - v.shape[1])))


def kernel(x, c, positions, w_ada, b_ada, norm_mix_g, w_in, b_gates, conv_w, conv_b, q_lora_g, w_uq,
           kv_lora_g, w_ukv, q_norm_g, k_norm_g, mlstm_norm_g, w_out, norm_mlp_g, w_ff1, w_ff2):
    bsz, s, d = x.shape
    assert bsz == 1, "kernels are written for a single sequence"
    xs = x[0]
    pos_col = positions.reshape(s, 1)
    half = jnp.arange(ROPE_HALF, dtype=F32)
    freqs = ROPE_THETA ** (-half / ROPE_HALF)
    freq = _pad_lanes(jnp.concatenate([freqs, freqs])[None, :], LANES)
    row = lambda v: v.reshape(1, -1).astype(F32)

    for l in range(w_ada.shape[0]):
        mod = _mod(c.reshape(d, 1), w_ada[l], row(b_ada[l]))
        proj, gates = _inproj(xs, row(norm_mix_g[l]), mod, _pack_w_in(w_in[l]))

        wuq = jnp.pad(w_uq[l].reshape(-1, N_MLA_HEADS, MLA_QK),
                      ((0, 0), (0, 0), (0, MLA_QK_PAD - MLA_QK))).reshape(-1, N_MLA_HEADS * MLA_QK_PAD)
        q, k, v = _mlaprep(proj, pos_col, freq, row(q_lora_g[l]), row(kv_lora_g[l]),
                           wuq.astype(BF16), w_ukv[l].astype(BF16),
                           _pad_lanes(row(q_norm_g[l]), MLA_QK_PAD), _pad_lanes(row(k_norm_g[l]), MLA_QK_PAD))
        attn = _attention(q, k, v)

        qk = _conv(proj, conv_w[l], row(conv_b[l]))
        hf, hb = _mlstm(qk, proj, gates, _pad_lanes(row(b_gates[l]), LANES))

        x1, h2 = _outproj(attn, hf, hb, proj, xs, row(mlstm_norm_g[l]), w_out[l].astype(BF16), mod,
                          row(norm_mlp_g[l]))
        xs = _ffn(h2, w_ff1[l].astype(BF16), w_ff2[l].astype(BF16), x1, mod)
    return xs[None]
```

```python
import functools

import jax
import jax.numpy as jnp
from jax import lax
from jax.experimental import pallas as pl
from jax.experimental.pallas import tpu as pltpu

F32 = jnp.float32
BF16 = jnp.bfloat16

LANES = 128
N_MLA_HEADS = 8
MLA_NOPE = 128
ROPE_DIM = 64
ROPE_HALF = ROPE_DIM // 2
MLA_QK = MLA_NOPE + ROPE_DIM
MLA_QK_PAD = 256
MLA_V = 128
ROPE_THETA = 10000.0
N_MLSTM_HEADS = 4
MLSTM_DH = 256
MLSTM_CHUNK = 128
MLSTM_AUG = MLSTM_DH + LANES
CONV_WIDTH = 5
CONV_HALO = 8
EPS = 1e-6
M_INIT = -1e30
LOG2_E = 1.4426950408889634
SAFE_LOG2 = 60.0
VMEM_LIMIT = 56 * 1024 * 1024


def _cparams(sem):
    return pltpu.CompilerParams(dimension_semantics=sem, vmem_limit_bytes=VMEM_LIMIT)


def _mod_kernel(c_ref, w_ref, b_ref, o_ref, sb_ref, *, tn):
    @pl.when(pl.program_id(0) == 0)
    def _():
        cc = c_ref[...]
        sb_ref[...] = jnp.broadcast_to(cc * jax.nn.sigmoid(cc), sb_ref.shape)

    sb = sb_ref[...]
    for j in range(tn // LANES):
        sl = slice(j * LANES, (j + 1) * LANES)
        o_ref[:, sl] = jnp.sum(w_ref[:, sl] * sb, axis=0, keepdims=True) + b_ref[:, sl]


def _mod(c_col, w_ada, b_ada, tn=512):
    d, n = w_ada.shape
    return pl.pallas_call(
        functools.partial(_mod_kernel, tn=tn),
        grid=(n // tn,),
        in_specs=[pl.BlockSpec((d, 1), lambda j: (0, 0)),
                  pl.BlockSpec((d, tn), lambda j: (0, j)),
                  pl.BlockSpec((1, tn), lambda j: (0, j))],
        out_specs=pl.BlockSpec((1, tn), lambda j: (0, j)),
        out_shape=jax.ShapeDtypeStruct((1, n), F32),
        scratch_shapes=[pltpu.VMEM((d, LANES), F32)],
        compiler_params=_cparams(("arbitrary",)),
        name="mod",
    )(c_col, w_ada, b_ada)


def _modulated_norm(x, g, scale, shift):
    ms = jnp.mean(x * x, axis=-1, keepdims=True)
    return (x * lax.rsqrt(ms + EPS) * g) * (1.0 + scale) + shift


def _inproj_kernel(x_ref, g_ref, sc_ref, sh_ref, w_ref, o_ref, gate_ref, h_ref, *, n_last):
    n = pl.program_id(1)

    @pl.when(n == 0)
    def _():
        h_ref[...] = _modulated_norm(x_ref[...], g_ref[...], sc_ref[...], sh_ref[...]).astype(BF16)

    acc = jnp.dot(h_ref[...], w_ref[...], preferred_element_type=F32)
    o_ref[...] = acc.astype(BF16)

    @pl.when(n == n_last)
    def _():
        gate_ref[...] = acc[:, -LANES:]


def _inproj(x, g, mod, w, tm=1024, tn=1024):
    s, d = x.shape
    n_tot = w.shape[1]
    nb = d // d
    del nb
    return pl.pallas_call(
        functools.partial(_inproj_kernel, n_last=n_tot // tn - 1),
        grid=(s // tm, n_tot // tn),
        in_specs=[pl.BlockSpec((tm, d), lambda m, n: (m, 0)),
                  pl.BlockSpec((1, d), lambda m, n: (0, 0)),
                  pl.BlockSpec((1, d), lambda m, n: (0, 1)),
                  pl.BlockSpec((1, d), lambda m, n: (0, 0)),
                  pl.BlockSpec((d, tn), lambda m, n: (0, n))],
        out_specs=[pl.BlockSpec((tm, tn), lambda m, n: (m, n)),
                   pl.BlockSpec((tm, LANES), lambda m, n: (m, 0))],
        out_shape=[jax.ShapeDtypeStruct((s, n_tot), BF16),
                   jax.ShapeDtypeStruct((s, LANES), F32)],
        scratch_shapes=[pltpu.VMEM((tm, d), BF16)],
        compiler_params=_cparams(("arbitrary", "arbitrary")),
        name="inproj",
    )(x, g, mod, mod, w)


def _rms_scale(x, n):
    return lax.rsqrt(jnp.sum(x * x, axis=-1, keepdims=True) * (1.0 / n) + EPS)


def _mlaprep_kernel(cq_ref, ckv_ref, kpe_ref, pos_ref, freq_ref, gql_ref, gkvl_ref, wuq_ref, wukv_ref,
                    gq_ref, gk_ref, q_ref, k_ref, v_ref):
    cq = cq_ref[...].astype(F32)
    cqn = (cq * _rms_scale(cq, cq.shape[-1]) * gql_ref[...]).astype(BF16)
    qf = jnp.dot(cqn, wuq_ref[...], preferred_element_type=F32)
    ckv = ckv_ref[...].astype(F32)
    ckvn = (ckv * _rms_scale(ckv, ckv.shape[-1]) * gkvl_ref[...]).astype(BF16)
    kvf = jnp.dot(ckvn, wukv_ref[...], preferred_element_type=F32)

    ang_t = freq_ref[...] * pos_ref[...].astype(F32)
    cos_t, sin_t = jnp.cos(ang_t), jnp.sin(ang_t)
    z_half, z_pad = jnp.zeros_like(cos_t), jnp.zeros((LANES - ROPE_DIM, cos_t.shape[1]), F32)
    c_tab = jnp.concatenate([cos_t, cos_t, z_pad], axis=0).T
    s_up = jnp.concatenate([z_half, sin_t, z_pad], axis=0).T
    s_dn = jnp.concatenate([-sin_t, z_half, z_pad], axis=0).T

    def rope(t):
        return (t * c_tab + pltpu.roll(t, ROPE_HALF, 1) * s_up
                + pltpu.roll(t, LANES - ROPE_HALF, 1) * s_dn)

    gq = gq_ref[...] * (LOG2_E * MLA_QK ** -0.5)
    ones_col = (lax.broadcasted_iota(jnp.int32, (kpe_ref.shape[0], MLA_V), 1) == 0).astype(BF16)
    gk = gk_ref[...]
    kpe = kpe_ref[...].astype(F32)
    kpe_ssq = jnp.sum(kpe * kpe, axis=-1, keepdims=True)
    kpe_rot = rope(kpe * gk[:, MLA_NOPE:])
    for h in range(N_MLA_HEADS):
        qh = qf[:, h * MLA_QK_PAD:(h + 1) * MLA_QK_PAD]
        qn = qh * _rms_scale(qh, MLA_QK) * gq
        q_ref[h] = jnp.concatenate([qn[:, :MLA_NOPE], rope(qn[:, MLA_NOPE:])], axis=-1).astype(BF16)
        kn = kvf[:, h * 256:h * 256 + MLA_NOPE]
        r = lax.rsqrt((jnp.sum(kn * kn, axis=-1, keepdims=True) + kpe_ssq) * (1.0 / MLA_QK) + EPS)
        k_ref[h] = jnp.concatenate([kn * r * gk[:, :MLA_NOPE], kpe_rot * r], axis=-1).astype(BF16)
        v_ref[h] = jnp.concatenate([kvf[:, h * 256 + MLA_NOPE:(h + 1) * 256].astype(BF16), ones_col], axis=-1)


def _mlaprep(proj, pos_row, freq, gql, gkvl, wuq, wukv, gq, gk, tm=512):
    s = proj.shape[0]
    hq = N_MLA_HEADS
    full = lambda shape: pl.BlockSpec(shape, lambda i: (0,) * len(shape))
    return pl.pallas_call(
        _mlaprep_kernel,
        grid=(s // tm,),
        in_specs=[pl.BlockSpec((tm, 512), lambda i: (i, 8)),
                  pl.BlockSpec((tm, 256), lambda i: (i, 18)),
                  pl.BlockSpec((tm, LANES), lambda i: (i, 38)),
                  pl.BlockSpec((1, tm), lambda i: (0, i)),
                  full((ROPE_HALF, 1)), full((1, 512)), full((1, 256)),
                  full(wuq.shape), full(wukv.shape), full((1, MLA_QK_PAD)), full((1, MLA_QK_PAD))],
        out_specs=[pl.BlockSpec((hq, tm, MLA_QK_PAD), lambda i: (0, i, 0)),
                   pl.BlockSpec((hq, tm, MLA_QK_PAD), lambda i: (0, i, 0)),
                   pl.BlockSpec((hq, tm, 2 * MLA_V), lambda i: (0, i, 0))],
        out_shape=[jax.ShapeDtypeStruct((hq, s, MLA_QK_PAD), BF16),
                   jax.ShapeDtypeStruct((hq, s, MLA_QK_PAD), BF16),
                   jax.ShapeDtypeStruct((hq, s, 2 * MLA_V), BF16)],
        compiler_params=_cparams(("arbitrary",)),
        name="mlaprep",
    )(proj, proj, proj, pos_row, freq, gql, gkvl, wuq, wukv, gq, gk)


def _max_sq_norm(x):
    xf = x.astype(F32)
    return jnp.max(jnp.sum(xf * xf, axis=-1, keepdims=True))


def _attn_kernel(q_ref, k_ref, v_ref, o_ref, kn_ref, *, tk, unroll, tk_slow):
    n_keys = k_ref.shape[1]

    @pl.when(pl.program_id(1) == 0)
    def _():
        def body(j, m):
            start = pl.multiple_of(j * tk, tk)
            return jnp.maximum(m, _max_sq_norm(k_ref[0, pl.ds(start, tk), :]))
        kn_ref[0] = lax.fori_loop(0, n_keys // tk, body, jnp.float32(0.0))

    q = q_ref[0]
    tq = q.shape[0]
    bound_sq = _max_sq_norm(q) * kn_ref[0]
    nt = (((1,), (1,)), ((), ()))

    def fast():
        def logits(j):
            start = pl.multiple_of(j * tk, tk)
            return lax.dot_general(q, k_ref[0, pl.ds(start, tk), :], nt, preferred_element_type=F32)

        def weighted(j, s2):
            start = pl.multiple_of(j * tk, tk)
            return jnp.dot(jnp.exp2(s2).astype(BF16), v_ref[0, pl.ds(start, tk), :],
                           preferred_element_type=F32)

        def body(j, acc):
            for u in range(unroll):
                acc = acc + weighted(j * unroll + u, logits(j * unroll + u))
            return acc

        return lax.fori_loop(0, n_keys // (tk * unroll), body, jnp.zeros((tq, 2 * MLA_V), F32))

    def slow():
        def body(j, carry):
            m, acc = carry
            start = pl.multiple_of(j * tk_slow, tk_slow)
            s2 = lax.dot_general(q, k_ref[0, pl.ds(start, tk_slow), :], nt, preferred_element_type=F32)
            m_new = jnp.maximum(m, jnp.max(s2, axis=-1, keepdims=True))
            p = jnp.exp2(s2 - m_new).astype(BF16)
            acc = jnp.exp2(m - m_new) * acc + jnp.dot(p, v_ref[0, pl.ds(start, tk_slow), :],
                                                      preferred_element_type=F32)
            return m_new, acc
        init = (jnp.full((tq, 1), -jnp.inf, F32), jnp.zeros((tq, 2 * MLA_V), F32))
        return lax.fori_loop(0, n_keys // tk_slow, body, init)[1]

    acc = lax.cond(bound_sq <= SAFE_LOG2 * SAFE_LOG2, fast, slow)
    o_ref[...] = (acc[:, :MLA_V] / acc[:, MLA_V:MLA_V + 1]).astype(BF16)


def _attention(q, k, v, tq=512, tk=256, unroll=32, tk_slow=1024):
    hq, s, _ = q.shape
    return pl.pallas_call(
        functools.partial(_attn_kernel, tk=tk, unroll=unroll, tk_slow=tk_slow),
        grid=(hq, s // tq),
        in_specs=[pl.BlockSpec((1, tq, MLA_QK_PAD), lambda h, i: (h, i, 0)),
                  pl.BlockSpec((1, s, MLA_QK_PAD), lambda h, i: (h, 0, 0)),
                  pl.BlockSpec((1, s, 2 * MLA_V), lambda h, i: (h, 0, 0))],
        out_specs=pl.BlockSpec((tq, MLA_V), lambda h, i: (i, h)),
        out_shape=jax.ShapeDtypeStruct((s, hq * MLA_V), BF16),
        scratch_shapes=[pltpu.SMEM((1,), F32)],
        compiler_params=_cparams(("arbitrary", "arbitrary")),
        name="attn",
    )(q, k, v)


def _conv_kernel(x_ref, w_ref, b_ref, o_ref, pad_ref, *, rows, n_q_tiles):
    s = x_ref.shape[0]
    zeros = jnp.zeros((CONV_HALO, LANES), F32)
    pad_ref[0:CONV_HALO, :] = zeros
    pad_ref[CONV_HALO + s:CONV_HALO + s + CONV_HALO, :] = zeros
    pad_ref[CONV_HALO:CONV_HALO + s, :] = x_ref[...].astype(F32)
    out_scale = jnp.where(pl.program_id(0) >= n_q_tiles, MLSTM_DH ** -0.5, 1.0).astype(F32)
    w = w_ref[...]
    b = b_ref[...]

    def body(r, carry):
        base = pl.multiple_of(r * rows, rows)
        acc = b
        for j in range(CONV_WIDTH):
            off = CONV_HALO + j - CONV_WIDTH // 2
            acc = acc + w[j:j + 1, :] * pad_ref[pl.ds(base + off, rows), :]
        o_ref[pl.ds(base, rows), :] = (acc * jax.nn.sigmoid(acc) * out_scale).astype(BF16)
        return carry

    lax.fori_loop(0, s // rows, body, 0)


def _conv(proj, conv_w, conv_b, rows=256):
    s = proj.shape[0]
    n_ch = conv_w.shape[1]
    return pl.pallas_call(
        functools.partial(_conv_kernel, rows=rows, n_q_tiles=n_ch // 2 // LANES),
        grid=(n_ch // LANES,),
        in_specs=[pl.BlockSpec((s, LANES), lambda j: (0, j)),
                  pl.BlockSpec((CONV_WIDTH, LANES), lambda j: (0, j)),
                  pl.BlockSpec((1, LANES), lambda j: (0, j))],
        out_specs=pl.BlockSpec((s, LANES), lambda j: (0, j)),
        out_shape=jax.ShapeDtypeStruct((s, n_ch), BF16),
        scratch_shapes=[pltpu.VMEM((s + 2 * CONV_HALO, LANES), F32)],
        compiler_params=_cparams(("arbitrary",)),
        name="conv",
    )(proj, conv_w, conv_b)


def _mlstm_chain(q, k, v_aug, g, g_t, lf, b_c, b_ct, mask, ic, fc, c_ref, m_ref, chain):
    b_col, b_row = b_c[:, fc:fc + 1], b_ct[fc:fc + 1, :]
    i_col, i_row = g[:, ic:ic + 1], g_t[ic:ic + 1, :]
    b_tot = jnp.sum(lf[:, fc:fc + 1], axis=0, keepdims=True)
    m_prev = m_ref[chain:chain + 1, 0:1]

    log_d = jnp.where(mask, b_col - b_row + i_row, -jnp.inf)
    log_inter = b_col + m_prev
    m_t = jnp.maximum(log_inter, jnp.max(log_d, axis=-1, keepdims=True))
    d_m = jnp.exp(log_d - m_t)
    w_inter = jnp.exp(log_inter - m_t)

    c_aug = c_ref[chain]
    scores = lax.dot_general(q, k, (((1,), (1,)), ((), ())), preferred_element_type=F32) * d_m
    r = (jnp.dot(scores.astype(BF16), v_aug, preferred_element_type=F32)
         + w_inter * jnp.dot(q, c_aug.astype(BF16), preferred_element_type=F32))
    num, den = r[:, :MLSTM_DH], r[:, MLSTM_DH:MLSTM_DH + 1]
    h = num / jnp.maximum(jnp.abs(den), jnp.exp(-m_t))

    log_w = b_tot - b_col + i_col
    m_new = jnp.maximum(b_tot + m_prev, jnp.max(log_w, axis=0, keepdims=True))
    decay = jnp.exp(b_tot + m_prev - m_new)
    kw = (k.astype(F32) * jnp.exp(log_w - m_new)).astype(BF16)
    upd = lax.dot_general(kw, v_aug, (((0,), (0,)), ((), ())), preferred_element_type=F32)
    c_ref[chain] = decay * c_aug + upd
    m_ref[chain:chain + 1, :] = jnp.broadcast_to(m_new, (1, LANES))
    return h


def _mlstm_kernel(qf_ref, kf_ref, vf_ref, gf_ref, qb_ref, kb_ref, vb_ref, gb_ref, bg_ref,
                  hf_ref, hb_ref, c_ref, m_ref):
    @pl.when(pl.program_id(0) == 0)
    def _():
        c_ref[...] = jnp.zeros(c_ref.shape, F32)
        m_ref[...] = jnp.full(m_ref.shape, M_INIT, F32)

    L = MLSTM_CHUNK
    row = lax.broadcasted_iota(jnp.int32, (L, L), 0)
    col = lax.broadcasted_iota(jnp.int32, (L, L), 1)
    ones_col = (lax.broadcasted_iota(jnp.int32, (L, LANES), 1) == 0).astype(BF16)
    hm = N_MLSTM_HEADS
    for d, (q_ref, k_ref, v_ref, g_ref, o_ref) in enumerate(
            ((qf_ref, kf_ref, vf_ref, gf_ref, hf_ref), (qb_ref, kb_ref, vb_ref, gb_ref, hb_ref))):
        mask = (row >= col) if d == 0 else (row <= col)
        g = g_ref[...] + bg_ref[...]
        lf = jax.nn.log_sigmoid(g)
        b_c = jnp.dot(mask.astype(F32), lf, preferred_element_type=F32,
                      precision=lax.Precision.HIGHEST)
        g_t, b_ct = g.T, b_c.T
        for h in range(hm):
            sl = slice(h * MLSTM_DH, (h + 1) * MLSTM_DH)
            v_aug = jnp.concatenate([v_ref[:, sl], ones_col], axis=-1)
            o_ref[:, sl] = _mlstm_chain(q_ref[:, sl], k_ref[:, sl], v_aug, g, g_t, lf, b_c, b_ct, mask,
                                        2 * d * hm + h, (2 * d + 1) * hm + h, c_ref, m_ref, d * hm + h)


def _mlstm(qk, proj, gates, bg):
    s = qk.shape[0]
    L = MLSTM_CHUNK
    nc = s // L
    w = N_MLSTM_HEADS * MLSTM_DH
    fwd = lambda blk: (lambda i: (i, blk))
    bwd = lambda blk: (lambda i: (nc - 1 - i, blk))
    specs = lambda ix: [pl.BlockSpec((L, w), ix(0)), pl.BlockSpec((L, w), ix(1)),
                        pl.BlockSpec((L, w), ix(2)), pl.BlockSpec((L, LANES), ix(0))]
    return pl.pallas_call(
        _mlstm_kernel,
        grid=(nc,),
        in_specs=specs(fwd) + specs(bwd) + [pl.BlockSpec((1, LANES), lambda i: (0, 0))],
        out_specs=[pl.BlockSpec((L, w), fwd(0)), pl.BlockSpec((L, w), bwd(0))],
        out_shape=[jax.ShapeDtypeStruct((s, w), F32), jax.ShapeDtypeStruct((s, w), F32)],
        scratch_shapes=[pltpu.VMEM((2 * N_MLSTM_HEADS, MLSTM_DH, MLSTM_AUG), F32),
                        pltpu.VMEM((2 * N_MLSTM_HEADS, LANES), F32)],
        compiler_params=_cparams(("arbitrary",)),
        name="mlstm",
    )(qk, qk, proj, gates, qk, qk, proj, gates, bg)


def _outproj_kernel(attn_ref, hf_ref, hb_ref, om_ref, x_ref, gm_ref, w_ref, gate_ref, g2_ref, sc_ref, sh_ref,
                    x1_ref, h2_ref):
    hsum = hf_ref[...] + hb_ref[...]
    gm = gm_ref[...]
    parts = []
    for h in range(N_MLSTM_HEADS):
        sl = slice(h * MLSTM_DH, (h + 1) * MLSTM_DH)
        seg = hsum[:, sl]
        parts.append(seg * _rms_scale(seg, MLSTM_DH) * gm[:, sl])
    ml = (jax.nn.sigmoid(om_ref[...].astype(F32)) * jnp.concatenate(parts, axis=-1)).astype(BF16)
    n_attn = attn_ref.shape[1]
    mixed = (jnp.dot(attn_ref[...], w_ref[:n_attn, :], preferred_element_type=F32)
             + jnp.dot(ml, w_ref[n_attn:, :], preferred_element_type=F32))
    x1 = x_ref[...] + gate_ref[...] * mixed
    x1_ref[...] = x1
    h2_ref[...] = _modulated_norm(x1, g2_ref[...], sc_ref[...], sh_ref[...]).astype(BF16)


def _outproj(attn, hf, hb, proj, x, gm, w_out, mod, g2, tm=512):
    s, d = x.shape
    wm = hf.shape[1]
    row = lambda blk: pl.BlockSpec((1, d), lambda i: (0, blk))
    return pl.pallas_call(
        _outproj_kernel,
        grid=(s // tm,),
        in_specs=[pl.BlockSpec((tm, attn.shape[1]), lambda i: (i, 0)),
                  pl.BlockSpec((tm, wm), lambda i: (i, 0)),
                  pl.BlockSpec((tm, wm), lambda i: (i, 0)),
                  pl.BlockSpec((tm, wm), lambda i: (i, 3)),
                  pl.BlockSpec((tm, d), lambda i: (i, 0)),
                  pl.BlockSpec((1, wm), lambda i: (0, 0)),
                  pl.BlockSpec(w_out.shape, lambda i: (0, 0), pipeline_mode=pl.Buffered(1)),
                  row(2), row(0), row(4), row(3)],
        out_specs=[pl.BlockSpec((tm, d), lambda i: (i, 0)), pl.BlockSpec((tm, d), lambda i: (i, 0))],
        out_shape=[jax.ShapeDtypeStruct((s, d), F32), jax.ShapeDtypeStruct((s, d), BF16)],
        compiler_params=_cparams(("arbitrary",)),
        name="outproj",
    )(attn, hf, hb, proj, x, gm, w_out, mod, g2, mod, mod)


def _ffn_kernel(h2_ref, w1_ref, w2_ref, x1_ref, gate_ref, o_ref, *, f_last):
    f = pl.program_id(1)

    @pl.when(f == 0)
    def _():
        o_ref[...] = jnp.zeros(o_ref.shape, F32)

    a = jnp.maximum(jnp.dot(h2_ref[...], w1_ref[...], preferred_element_type=F32), 0.0)
    o_ref[...] += jnp.dot((a * a).astype(BF16), w2_ref[...], preferred_element_type=F32)

    @pl.when(f == f_last)
    def _():
        o_ref[...] = x1_ref[...] + gate_ref[...] * o_ref[...]


def _ffn(h2, w1, w2, x1, mod, tm=1024, tf=512):
    s, d = h2.shape
    dff = w1.shape[1]
    return pl.pallas_call(
        functools.partial(_ffn_kernel, f_last=dff // tf - 1),
        grid=(s // tm, dff // tf),
        in_specs=[pl.BlockSpec((tm, d), lambda m, f: (m, 0)),
                  pl.BlockSpec((d, tf), lambda m, f: (0, f)),
                  pl.BlockSpec((tf, d), lambda m, f: (f, 0)),
                  pl.BlockSpec((tm, d), lambda m, f: (m, 0), pipeline_mode=pl.Buffered(1)),
                  pl.BlockSpec((1, d), lambda m, f: (0, 5))],
        out_specs=pl.BlockSpec((tm, d), lambda m, f: (m, 0)),
        out_shape=jax.ShapeDtypeStruct((s, d), F32),
        compiler_params=_cparams(("arbitrary", "arbitrary")),
        name="ffn",
    )(h2, w1, w2, x1, mod)


PACK_TILE = 512
PACK_SRC_TILES = PACK_TILE // LANES + 1
W_IN_MIX_START = 832
W_IN_GATES_START = 4928
N_GATE_COLS = 16


def _pack_src_tiles():
    rows = []
    n_mix = 4 * N_MLSTM_HEADS * MLSTM_DH // PACK_TILE
    for n in range(n_mix):
        first = (W_IN_MIX_START + n * PACK_TILE) // LANES
        rows.append([first + i for i in range(PACK_SRC_TILES)])
    rows.append([0, 1, 2, 3, 3])
    rows.append([4, 5, 6, W_IN_GATES_START // LANES, W_IN_GATES_START // LANES])
    return jnp.asarray(rows, jnp.int32).reshape(-1)


def _pack_kernel(tbl_ref, b0, b1, b2, b3, b4, o_ref, *, n_mix):
    del tbl_ref
    n = pl.program_id(0)
    half = LANES // 2
    lane = lax.broadcasted_iota(jnp.int32, b0.shape, 1)

    @pl.when(n < n_mix)
    def _():
        o_ref[...] = jnp.concatenate([b0[:, half:], b1[...], b2[...], b3[...], b4[:, :half]], axis=1).astype(BF16)

    @pl.when(n == n_mix)
    def _():
        o_ref[...] = jnp.concatenate([b0[...], b1[...], b2[...], b3[...]], axis=1).astype(BF16)

    @pl.when(n == n_mix + 1)
    def _():
        k_pe = jnp.where(lane < ROPE_DIM, b2[...], 0.0)
        gates = jnp.where(lane < N_GATE_COLS, jnp.concatenate([b3[:, half:], b3[:, :half]], axis=1), 0.0)
        o_ref[...] = jnp.concatenate([b0[...], b1[...], k_pe, gates], axis=1).astype(BF16)


def _pack_w_in(w_in):
    d = w_in.shape[0]
    tbl = _pack_src_tiles()
    n_tiles = tbl.shape[0] // PACK_SRC_TILES
    src = lambda i: pl.BlockSpec((d, LANES), lambda n, t: (0, t[n * PACK_SRC_TILES + i]))
    return pl.pallas_call(
        functools.partial(_pack_kernel, n_mix=n_tiles - 2),
        grid_spec=pltpu.PrefetchScalarGridSpec(
            num_scalar_prefetch=1, grid=(n_tiles,),
            in_specs=[src(i) for i in range(PACK_SRC_TILES)],
            out_specs=pl.BlockSpec((d, PACK_TILE), lambda n, t: (0, n))),
        out_shape=jax.ShapeDtypeStruct((d, n_tiles * PACK_TILE), BF16),
        compiler_params=_cparams(("arbitrary",)),
        name="packw",
    )(tbl, w_in, w_in, w_in, w_in, w_in)


def _pad_lanes(v, n):
    return jnp.pad(v, ((0, 0), (0, n - v.shape[1])))


def kernel(x, c, positions, w_ada, b_ada, norm_mix_g, w_in, b_gates, conv_w, conv_b, q_lora_g, w_uq,
           kv_lora_g, w_ukv, q_norm_g, k_norm_g, mlstm_norm_g, w_out, norm_mlp_g, w_ff1, w_ff2):
    bsz, s, d = x.shape
    assert bsz == 1, "kernels are written for a single sequence"
    xs = x[0]
    pos_row = positions.reshape(1, s)
    half = jnp.arange(ROPE_HALF, dtype=F32)
    freq = (ROPE_THETA ** (-half / ROPE_HALF)).reshape(ROPE_HALF, 1)
    row = lambda v: v.reshape(1, -1).astype(F32)

    for l in range(w_ada.shape[0]):
        mod = _mod(c.reshape(d, 1), w_ada[l], row(b_ada[l]))
        proj, gates = _inproj(xs, row(norm_mix_g[l]), mod, _pack_w_in(w_in[l]))

        wuq = jnp.pad(w_uq[l].reshape(-1, N_MLA_HEADS, MLA_QK),
                      ((0, 0), (0, 0), (0, MLA_QK_PAD - MLA_QK))).reshape(-1, N_MLA_HEADS * MLA_QK_PAD)
        q, k, v = _mlaprep(proj, pos_row, freq, row(q_lora_g[l]), row(kv_lora_g[l]),
                           wuq.astype(BF16), w_ukv[l].astype(BF16),
                           _pad_lanes(row(q_norm_g[l]), MLA_QK_PAD), _pad_lanes(row(k_norm_g[l]), MLA_QK_PAD))
        attn = _attention(q, k, v)

        qk = _conv(proj, conv_w[l], row(conv_b[l]))
        hf, hb = _mlstm(qk, proj, gates, _pad_lanes(row(b_gates[l]), LANES))

        x1, h2 = _outproj(attn, hf, hb, proj, xs, row(mlstm_norm_g[l]), w_out[l].astype(BF16), mod,
                          row(norm_mlp_g[l]))
        xs = _ffn(h2, w_ff1[l].astype(BF16), w_ff2[l].astype(BF16), x1, mod)
    return xs[None]
```

```python
import functools

import jax
import jax.numpy as jnp
from jax import lax
from jax.experimental import pallas as pl
from jax.experimental.pallas import tpu as pltpu

F32 = jnp.float32
BF16 = jnp.bfloat16

LANES = 128
N_MLA_HEADS = 8
MLA_NOPE = 128
ROPE_DIM = 64
ROPE_HALF = ROPE_DIM // 2
MLA_QK = MLA_NOPE + ROPE_DIM
MLA_QK_PAD = 256
MLA_V = 128
ROPE_THETA = 10000.0
N_MLSTM_HEADS = 4
MLSTM_DH = 256
MLSTM_CHUNK = 256
MLSTM_AUG = MLSTM_DH + LANES
CONV_WIDTH = 5
CONV_HALO = 8
EPS = 1e-6
M_INIT = -1e30
LOG2_E = 1.4426950408889634
SAFE_LOG2 = 60.0
VMEM_LIMIT = 56 * 1024 * 1024


def _cparams(sem):
    return pltpu.CompilerParams(dimension_semantics=sem, vmem_limit_bytes=VMEM_LIMIT)


def _mod_kernel(c_ref, w_ref, b_ref, o_ref, sb_ref, *, tn):
    @pl.when(pl.program_id(0) == 0)
    def _():
        cc = c_ref[...]
        sb_ref[...] = jnp.broadcast_to(cc * jax.nn.sigmoid(cc), sb_ref.shape)

    sb = sb_ref[...]
    for j in range(tn // LANES):
        sl = slice(j * LANES, (j + 1) * LANES)
        o_ref[:, sl] = jnp.sum(w_ref[:, sl] * sb, axis=0, keepdims=True) + b_ref[:, sl]


def _mod(c_col, w_ada, b_ada, tn=512):
    d, n = w_ada.shape
    return pl.pallas_call(
        functools.partial(_mod_kernel, tn=tn),
        grid=(n // tn,),
        in_specs=[pl.BlockSpec((d, 1), lambda j: (0, 0)),
                  pl.BlockSpec((d, tn), lambda j: (0, j)),
                  pl.BlockSpec((1, tn), lambda j: (0, j))],
        out_specs=pl.BlockSpec((1, tn), lambda j: (0, j)),
        out_shape=jax.ShapeDtypeStruct((1, n), F32),
        scratch_shapes=[pltpu.VMEM((d, LANES), F32)],
        compiler_params=_cparams(("arbitrary",)),
        name="mod",
    )(c_col, w_ada, b_ada)


def _modulated_norm(x, g, scale, shift):
    ms = jnp.mean(x * x, axis=-1, keepdims=True)
    return (x * lax.rsqrt(ms + EPS) * g) * (1.0 + scale) + shift


def _inproj_kernel(x_ref, g_ref, sc_ref, sh_ref, w_ref, o_ref, gate_ref, h_ref, *, n_last):
    n = pl.program_id(1)

    @pl.when(n == 0)
    def _():
        h_ref[...] = _modulated_norm(x_ref[...], g_ref[...], sc_ref[...], sh_ref[...]).astype(BF16)

    acc = lax.dot_general(h_ref[...], w_ref[...], (((1,), (1,)), ((), ())), preferred_element_type=F32)
    o_ref[...] = acc.astype(BF16)

    @pl.when(n == n_last)
    def _():
        gate_ref[...] = acc[:, -LANES:]


def _inproj(x, g, mod, w_t, tm=1024, tn=1024):
    s, d = x.shape
    n_tot = w_t.shape[0]
    return pl.pallas_call(
        functools.partial(_inproj_kernel, n_last=n_tot // tn - 1),
        grid=(s // tm, n_tot // tn),
        in_specs=[pl.BlockSpec((tm, d), lambda m, n: (m, 0)),
                  pl.BlockSpec((1, d), lambda m, n: (0, 0)),
                  pl.BlockSpec((1, d), lambda m, n: (0, 1)),
                  pl.BlockSpec((1, d), lambda m, n: (0, 0)),
                  pl.BlockSpec((tn, d), lambda m, n: (n, 0))],
        out_specs=[pl.BlockSpec((tm, tn), lambda m, n: (m, n)),
                   pl.BlockSpec((tm, LANES), lambda m, n: (m, 0))],
        out_shape=[jax.ShapeDtypeStruct((s, n_tot), BF16),
                   jax.ShapeDtypeStruct((s, LANES), F32)],
        scratch_shapes=[pltpu.VMEM((tm, d), BF16)],
        compiler_params=_cparams(("arbitrary", "arbitrary")),
        name="inproj",
    )(x, g, mod, mod, w_t)


def _rms_scale(x, n):
    return lax.rsqrt(jnp.sum(x * x, axis=-1, keepdims=True) * (1.0 / n) + EPS)


def _mlaprep_kernel(cq_ref, ckv_ref, kpe_ref, pos_ref, freq_ref, gql_ref, gkvl_ref, wuq_ref, wukv_ref,
                    gq_ref, gk_ref, q_ref, k_ref, v_ref):
    cq = cq_ref[...].astype(F32)
    cqn = (cq * _rms_scale(cq, cq.shape[-1]) * gql_ref[...]).astype(BF16)
    qf = jnp.dot(cqn, wuq_ref[...], preferred_element_type=F32)
    ckv = ckv_ref[...].astype(F32)
    ckvn = (ckv * _rms_scale(ckv, ckv.shape[-1]) * gkvl_ref[...]).astype(BF16)
    kvf = jnp.dot(ckvn, wukv_ref[...], preferred_element_type=F32)

    ang_t = freq_ref[...] * pos_ref[...].astype(F32)
    cos_t, sin_t = jnp.cos(ang_t), jnp.sin(ang_t)
    z_half, z_pad = jnp.zeros_like(cos_t), jnp.zeros((LANES - ROPE_DIM, cos_t.shape[1]), F32)
    c_tab = jnp.concatenate([cos_t, cos_t, z_pad], axis=0).T
    s_up = jnp.concatenate([z_half, sin_t, z_pad], axis=0).T
    s_dn = jnp.concatenate([-sin_t, z_half, z_pad], axis=0).T

    def rope(t):
        return (t * c_tab + pltpu.roll(t, ROPE_HALF, 1) * s_up
                + pltpu.roll(t, LANES - ROPE_HALF, 1) * s_dn)

    gq = gq_ref[...] * (LOG2_E * MLA_QK ** -0.5)
    ones_col = (lax.broadcasted_iota(jnp.int32, (kpe_ref.shape[0], MLA_V), 1) == 0).astype(BF16)
    gk = gk_ref[...]
    kpe = kpe_ref[...].astype(F32)
    kpe_ssq = jnp.sum(kpe * kpe, axis=-1, keepdims=True)
    kpe_rot = rope(kpe * gk[:, MLA_NOPE:])
    for h in range(N_MLA_HEADS):
        qh = qf[:, h * MLA_QK_PAD:(h + 1) * MLA_QK_PAD]
        qn = qh * _rms_scale(qh, MLA_QK) * gq
        q_ref[h] = jnp.concatenate([qn[:, :MLA_NOPE], rope(qn[:, MLA_NOPE:])], axis=-1).astype(BF16)
        kn = kvf[:, h * 256:h * 256 + MLA_NOPE]
        r = lax.rsqrt((jnp.sum(kn * kn, axis=-1, keepdims=True) + kpe_ssq) * (1.0 / MLA_QK) + EPS)
        k_ref[h] = jnp.concatenate([kn * r * gk[:, :MLA_NOPE], kpe_rot * r], axis=-1).astype(BF16)
        v_ref[h] = jnp.concatenate([kvf[:, h * 256 + MLA_NOPE:(h + 1) * 256].astype(BF16), ones_col], axis=-1)


def _mlaprep(proj, pos_row, freq, gql, gkvl, wuq, wukv, gq, gk, tm=512):
    s = proj.shape[0]
    hq = N_MLA_HEADS
    full = lambda shape: pl.BlockSpec(shape, lambda i: (0,) * len(shape))
    return pl.pallas_call(
        _mlaprep_kernel,
        grid=(s // tm,),
        in_specs=[pl.BlockSpec((tm, 512), lambda i: (i, 8)),
                  pl.BlockSpec((tm, 256), lambda i: (i, 18)),
                  pl.BlockSpec((tm, LANES), lambda i: (i, 38)),
                  pl.BlockSpec((1, tm), lambda i: (0, i)),
                  full((ROPE_HALF, 1)), full((1, 512)), full((1, 256)),
                  full(wuq.shape), full(wukv.shape), full((1, MLA_QK_PAD)), full((1, MLA_QK_PAD))],
        out_specs=[pl.BlockSpec((hq, tm, MLA_QK_PAD), lambda i: (0, i, 0)),
                   pl.BlockSpec((hq, tm, MLA_QK_PAD), lambda i: (0, i, 0)),
                   pl.BlockSpec((hq, tm, 2 * MLA_V), lambda i: (0, i, 0))],
        out_shape=[jax.ShapeDtypeStruct((hq, s, MLA_QK_PAD), BF16),
                   jax.ShapeDtypeStruct((hq, s, MLA_QK_PAD), BF16),
                   jax.ShapeDtypeStruct((hq, s, 2 * MLA_V), BF16)],
        compiler_params=_cparams(("arbitrary",)),
        name="mlaprep",
    )(proj, proj, proj, pos_row, freq, gql, gkvl, wuq, wukv, gq, gk)


def _max_sq_norm(x):
    xf = x.astype(F32)
    return jnp.max(jnp.sum(xf * xf, axis=-1, keepdims=True))


def _attn_kernel(q_ref, k_ref, v_ref, *refs, n_side, tk, unroll, tk_slow):
    side_in, o_ref, side_out, kn_ref = refs[:n_side], refs[n_side], refs[n_side + 1:-1], refs[-1]
    for src, dst in zip(side_in, side_out):
        dst[...] = src[...].astype(BF16)
    n_keys = k_ref.shape[1]

    @pl.when(pl.program_id(1) == 0)
    def _():
        def body(j, m):
            start = pl.multiple_of(j * tk, tk)
            return jnp.maximum(m, _max_sq_norm(k_ref[0, pl.ds(start, tk), :]))
        kn_ref[0] = lax.fori_loop(0, n_keys // tk, body, jnp.float32(0.0))

    q = q_ref[0]
    tq = q.shape[0]
    bound_sq = _max_sq_norm(q) * kn_ref[0]
    nt = (((1,), (1,)), ((), ()))

    def fast():
        def logits(j):
            start = pl.multiple_of(j * tk, tk)
            return lax.dot_general(q, k_ref[0, pl.ds(start, tk), :], nt, preferred_element_type=F32)

        def weighted(j, s2):
            start = pl.multiple_of(j * tk, tk)
            return jnp.dot(jnp.exp2(s2).astype(BF16), v_ref[0, pl.ds(start, tk), :],
                           preferred_element_type=F32)

        def body(j, acc):
            for u in range(unroll):
                acc = acc + weighted(j * unroll + u, logits(j * unroll + u))
            return acc

        return lax.fori_loop(0, n_keys // (tk * unroll), body, jnp.zeros((tq, 2 * MLA_V), F32))

    def slow():
        def body(j, carry):
            m, acc = carry
            start = pl.multiple_of(j * tk_slow, tk_slow)
            s2 = lax.dot_general(q, k_ref[0, pl.ds(start, tk_slow), :], nt, preferred_element_type=F32)
            m_new = jnp.maximum(m, jnp.max(s2, axis=-1, keepdims=True))
            p = jnp.exp2(s2 - m_new).astype(BF16)
            acc = jnp.exp2(m - m_new) * acc + jnp.dot(p, v_ref[0, pl.ds(start, tk_slow), :],
                                                      preferred_element_type=F32)
            return m_new, acc
        init = (jnp.full((tq, 1), -jnp.inf, F32), jnp.zeros((tq, 2 * MLA_V), F32))
        return lax.fori_loop(0, n_keys // tk_slow, body, init)[1]

    acc = lax.cond(bound_sq <= SAFE_LOG2 * SAFE_LOG2, fast, slow)
    o_ref[...] = (acc[:, :MLA_V] / acc[:, MLA_V:MLA_V + 1]).astype(BF16)


def _attention(q, k, v, side_weights, tq=512, tk=256, unroll=32, tk_slow=1024):
    hq, s, _ = q.shape
    n_steps = hq * (s // tq)
    slab = lambda w: pl.BlockSpec((w.shape[0] // n_steps, w.shape[1]), lambda h, i: (h * (s // tq) + i, 0))
    outs = pl.pallas_call(
        functools.partial(_attn_kernel, n_side=len(side_weights), tk=tk, unroll=unroll, tk_slow=tk_slow),
        grid=(hq, s // tq),
        in_specs=[pl.BlockSpec((1, tq, MLA_QK_PAD), lambda h, i: (h, i, 0)),
                  pl.BlockSpec((1, s, MLA_QK_PAD), lambda h, i: (h, 0, 0)),
                  pl.BlockSpec((1, s, 2 * MLA_V), lambda h, i: (h, 0, 0))] + [slab(w) for w in side_weights],
        out_specs=[pl.BlockSpec((tq, MLA_V), lambda h, i: (i, h))] + [slab(w) for w in side_weights],
        out_shape=[jax.ShapeDtypeStruct((s, hq * MLA_V), BF16)]
                  + [jax.ShapeDtypeStruct(w.shape, BF16) for w in side_weights],
        scratch_shapes=[pltpu.SMEM((1,), F32)],
        compiler_params=_cparams(("arbitrary", "arbitrary")),
        name="attn",
    )(q, k, v, *side_weights)
    return outs[0], outs[1:]


def _conv_kernel(x_ref, w_ref, b_ref, o_ref, pad_ref, *, rows, n_q_tiles):
    s = x_ref.shape[0]
    zeros = jnp.zeros((CONV_HALO, LANES), F32)
    pad_ref[0:CONV_HALO, :] = zeros
    pad_ref[CONV_HALO + s:CONV_HALO + s + CONV_HALO, :] = zeros
    pad_ref[CONV_HALO:CONV_HALO + s, :] = x_ref[...].astype(F32)
    out_scale = jnp.where(pl.program_id(0) >= n_q_tiles, MLSTM_DH ** -0.5, 1.0).astype(F32)
    w = w_ref[...]
    b = b_ref[...]

    def body(r, carry):
        base = pl.multiple_of(r * rows, rows)
        acc = b
        for j in range(CONV_WIDTH):
            off = CONV_HALO + j - CONV_WIDTH // 2
            acc = acc + w[j:j + 1, :] * pad_ref[pl.ds(base + off, rows), :]
        o_ref[pl.ds(base, rows), :] = (acc * jax.nn.sigmoid(acc) * out_scale).astype(BF16)
        return carry

    lax.fori_loop(0, s // rows, body, 0)


def _conv(proj, conv_w, conv_b, rows=256):
    s = proj.shape[0]
    n_ch = conv_w.shape[1]
    return pl.pallas_call(
        functools.partial(_conv_kernel, rows=rows, n_q_tiles=n_ch // 2 // LANES),
        grid=(n_ch // LANES,),
        in_specs=[pl.BlockSpec((s, LANES), lambda j: (0, j)),
                  pl.BlockSpec((CONV_WIDTH, LANES), lambda j: (0, j)),
                  pl.BlockSpec((1, LANES), lambda j: (0, j))],
        out_specs=pl.BlockSpec((s, LANES), lambda j: (0, j)),
        out_shape=jax.ShapeDtypeStruct((s, n_ch), BF16),
        scratch_shapes=[pltpu.VMEM((s + 2 * CONV_HALO, LANES), F32)],
        compiler_params=_cparams(("arbitrary",)),
        name="conv",
    )(proj, conv_w, conv_b)


def _mlstm_chain(q, k, v_aug, g, g_t, lf, b_c, b_ct, mask, ic, fc, c_ref, m_ref, chain):
    b_col, b_row = b_c[:, fc:fc + 1], b_ct[fc:fc + 1, :]
    i_col, i_row = g[:, ic:ic + 1], g_t[ic:ic + 1, :]
    b_tot = jnp.sum(lf[:, fc:fc + 1], axis=0, keepdims=True)
    m_prev = m_ref[chain:chain + 1, 0:1]

    log_d = jnp.where(mask, b_col - b_row + i_row, -jnp.inf)
    log_inter = b_col + m_prev
    m_t = jnp.maximum(log_inter, jnp.max(log_d, axis=-1, keepdims=True))
    d_m = jnp.exp(log_d - m_t)
    w_inter = jnp.exp(log_inter - m_t)

    c_aug = c_ref[chain]
    scores = lax.dot_general(q, k, (((1,), (1,)), ((), ())), preferred_element_type=F32) * d_m
    r = (jnp.dot(scores.astype(BF16), v_aug, preferred_element_type=F32)
         + w_inter * jnp.dot(q, c_aug.astype(BF16), preferred_element_type=F32))
    num, den = r[:, :MLSTM_DH], r[:, MLSTM_DH:MLSTM_DH + 1]
    h = num / jnp.maximum(jnp.abs(den), jnp.exp(-m_t))

    log_w = b_tot - b_col + i_col
    m_new = jnp.maximum(b_tot + m_prev, jnp.max(log_w, axis=0, keepdims=True))
    decay = jnp.exp(b_tot + m_prev - m_new)
    kw = (k.astype(F32) * jnp.exp(log_w - m_new)).astype(BF16)
    upd = lax.dot_general(kw, v_aug, (((0,), (0,)), ((), ())), preferred_element_type=F32)
    c_ref[chain] = decay * c_aug + upd
    m_ref[chain:chain + 1, :] = jnp.broadcast_to(m_new, (1, LANES))
    return h


def _mlstm_kernel(qf_ref, kf_ref, vf_ref, gf_ref, qb_ref, kb_ref, vb_ref, gb_ref, bg_ref,
                  hf_ref, hb_ref, c_ref, m_ref):
    @pl.when(pl.program_id(0) == 0)
    def _():
        c_ref[...] = jnp.zeros(c_ref.shape, F32)
        m_ref[...] = jnp.full(m_ref.shape, M_INIT, F32)

    L = MLSTM_CHUNK
    row = lax.broadcasted_iota(jnp.int32, (L, L), 0)
    col = lax.broadcasted_iota(jnp.int32, (L, L), 1)
    ones_col = (lax.broadcasted_iota(jnp.int32, (L, LANES), 1) == 0).astype(BF16)
    hm = N_MLSTM_HEADS
    for d, (q_ref, k_ref, v_ref, g_ref, o_ref) in enumerate(
            ((qf_ref, kf_ref, vf_ref, gf_ref, hf_ref), (qb_ref, kb_ref, vb_ref, gb_ref, hb_ref))):
        mask = (row >= col) if d == 0 else (row <= col)
        g = g_ref[...] + bg_ref[...]
        lf = jax.nn.log_sigmoid(g)
        b_c = jnp.dot(mask.astype(F32), lf, preferred_element_type=F32,
                      precision=lax.Precision.HIGHEST)
        g_t, b_ct = g.T, b_c.T
        for h in range(hm):
            sl = slice(h * MLSTM_DH, (h + 1) * MLSTM_DH)
            v_aug = jnp.concatenate([v_ref[:, sl], ones_col], axis=-1)
            o_ref[:, sl] = _mlstm_chain(q_ref[:, sl], k_ref[:, sl], v_aug, g, g_t, lf, b_c, b_ct, mask,
                                        2 * d * hm + h, (2 * d + 1) * hm + h, c_ref, m_ref, d * hm + h)


def _mlstm(qk, proj, gates, bg):
    s = qk.shape[0]
    L = MLSTM_CHUNK
    nc = s // L
    w = N_MLSTM_HEADS * MLSTM_DH
    fwd = lambda blk: (lambda i: (i, blk))
    bwd = lambda blk: (lambda i: (nc - 1 - i, blk))
    specs = lambda ix: [pl.BlockSpec((L, w), ix(0)), pl.BlockSpec((L, w), ix(1)),
                        pl.BlockSpec((L, w), ix(2)), pl.BlockSpec((L, LANES), ix(0))]
    return pl.pallas_call(
        _mlstm_kernel,
        grid=(nc,),
        in_specs=specs(fwd) + specs(bwd) + [pl.BlockSpec((1, LANES), lambda i: (0, 0))],
        out_specs=[pl.BlockSpec((L, w), fwd(0)), pl.BlockSpec((L, w), bwd(0))],
        out_shape=[jax.ShapeDtypeStruct((s, w), F32), jax.ShapeDtypeStruct((s, w), F32)],
        scratch_shapes=[pltpu.VMEM((2 * N_MLSTM_HEADS, MLSTM_DH, MLSTM_AUG), F32),
                        pltpu.VMEM((2 * N_MLSTM_HEADS, LANES), F32)],
        compiler_params=_cparams(("arbitrary",)),
        name="mlstm",
    )(qk, qk, proj, gates, qk, qk, proj, gates, bg)


def _outproj_kernel(attn_ref, hf_ref, hb_ref, om_ref, x_ref, gm_ref, w_ref, gate_ref, g2_ref, sc_ref, sh_ref,
                    x1_ref, h2_ref):
    hsum = hf_ref[...] + hb_ref[...]
    gm = gm_ref[...]
    parts = []
    for h in range(N_MLSTM_HEADS):
        sl = slice(h * MLSTM_DH, (h + 1) * MLSTM_DH)
        seg = hsum[:, sl]
        parts.append(seg * _rms_scale(seg, MLSTM_DH) * gm[:, sl])
    ml = (jax.nn.sigmoid(om_ref[...].astype(F32)) * jnp.concatenate(parts, axis=-1)).astype(BF16)
    n_attn = attn_ref.shape[1]
    mixed = (jnp.dot(attn_ref[...], w_ref[:n_attn, :], preferred_element_type=F32)
             + jnp.dot(ml, w_ref[n_attn:, :], preferred_element_type=F32))
    x1 = x_ref[...] + gate_ref[...] * mixed
    x1_ref[...] = x1
    h2_ref[...] = _modulated_norm(x1, g2_ref[...], sc_ref[...], sh_ref[...]).astype(BF16)


def _outproj(attn, hf, hb, proj, x, gm, w_out, mod, g2, tm=512):
    s, d = x.shape
    wm = hf.shape[1]
    row = lambda blk: pl.BlockSpec((1, d), lambda i: (0, blk))
    return pl.pallas_call(
        _outproj_kernel,
        grid=(s // tm,),
        in_specs=[pl.BlockSpec((tm, attn.shape[1]), lambda i: (i, 0)),
                  pl.BlockSpec((tm, wm), lambda i: (i, 0)),
                  pl.BlockSpec((tm, wm), lambda i: (i, 0)),
                  pl.BlockSpec((tm, wm), lambda i: (i, 3)),
                  pl.BlockSpec((tm, d), lambda i: (i, 0)),
                  pl.BlockSpec((1, wm), lambda i: (0, 0)),
                  pl.BlockSpec(w_out.shape, lambda i: (0, 0), pipeline_mode=pl.Buffered(1)),
                  row(2), row(0), row(4), row(3)],
        out_specs=[pl.BlockSpec((tm, d), lambda i: (i, 0)), pl.BlockSpec((tm, d), lambda i: (i, 0))],
        out_shape=[jax.ShapeDtypeStruct((s, d), F32), jax.ShapeDtypeStruct((s, d), BF16)],
        compiler_params=_cparams(("arbitrary",)),
        name="outproj",
    )(attn, hf, hb, proj, x, gm, w_out, mod, g2, mod, mod)


def _ffn_kernel(h2_ref, w1_ref, w2_ref, x1_ref, gate_ref, o_ref, *, f_last):
    f = pl.program_id(1)

    @pl.when(f == 0)
    def _():
        o_ref[...] = jnp.zeros(o_ref.shape, F32)

    a = jnp.maximum(jnp.dot(h2_ref[...], w1_ref[...], preferred_element_type=F32), 0.0)
    o_ref[...] += jnp.dot((a * a).astype(BF16), w2_ref[...], preferred_element_type=F32)

    @pl.when(f == f_last)
    def _():
        o_ref[...] = x1_ref[...] + gate_ref[...] * o_ref[...]


def _ffn(h2, w1, w2, x1, mod, tm=1024, tf=512):
    s, d = h2.shape
    dff = w1.shape[1]
    return pl.pallas_call(
        functools.partial(_ffn_kernel, f_last=dff // tf - 1),
        grid=(s // tm, dff // tf),
        in_specs=[pl.BlockSpec((tm, d), lambda m, f: (m, 0)),
                  pl.BlockSpec((d, tf), lambda m, f: (0, f)),
                  pl.BlockSpec((tf, d), lambda m, f: (f, 0)),
                  pl.BlockSpec((tm, d), lambda m, f: (m, 0), pipeline_mode=pl.Buffered(1)),
                  pl.BlockSpec((1, d), lambda m, f: (0, 5))],
        out_specs=pl.BlockSpec((tm, d), lambda m, f: (m, 0)),
        out_shape=jax.ShapeDtypeStruct((s, d), F32),
        compiler_params=_cparams(("arbitrary", "arbitrary")),
        name="ffn",
    )(h2, w1, w2, x1, mod)


PACK_TILE = 512
PACK_SRC = 64
PACK_NSRC = PACK_TILE // PACK_SRC
W_IN_MIX_START = 832
W_IN_GATES_START = 4928
N_GATE_COLS = 16


def _pack_src_blocks():
    rows = []
    n_mix = 4 * N_MLSTM_HEADS * MLSTM_DH // PACK_TILE
    for n in range(n_mix):
        first = (W_IN_MIX_START + n * PACK_TILE) // PACK_SRC
        rows.append([first + i for i in range(PACK_NSRC)])
    rows.append(list(range(PACK_NSRC)))
    gates_blk = W_IN_GATES_START // PACK_SRC
    rows.append([8, 9, 10, 11, 12, 12, gates_blk, gates_blk])
    return jnp.asarray(rows, jnp.int32).reshape(-1)


def _pack_kernel(tbl_ref, *refs, n_mix):
    del tbl_ref
    srcs, o_ref = refs[:-1], refs[-1]
    n = pl.program_id(0)
    blk = lambda i: slice(i * PACK_SRC, (i + 1) * PACK_SRC)

    @pl.when(n <= n_mix)
    def _():
        for i, b in enumerate(srcs):
            o_ref[blk(i), :] = b[...].astype(BF16)

    @pl.when(n == n_mix + 1)
    def _():
        zeros = jnp.zeros(srcs[0].shape, BF16)
        for i in range(5):
            o_ref[blk(i), :] = srcs[i][...].astype(BF16)
        o_ref[blk(5), :] = zeros
        rows = lax.broadcasted_iota(jnp.int32, srcs[6].shape, 0)
        o_ref[blk(6), :] = jnp.where(rows < N_GATE_COLS, srcs[6][...], 0.0).astype(BF16)
        o_ref[blk(7), :] = zeros


def _pack_w_in(w_in_t):
    d = w_in_t.shape[1]
    tbl = _pack_src_blocks()
    n_tiles = tbl.shape[0] // PACK_NSRC
    src = lambda i: pl.BlockSpec((PACK_SRC, d), lambda n, t: (t[n * PACK_NSRC + i], 0))
    return pl.pallas_call(
        functools.partial(_pack_kernel, n_mix=n_tiles - 2),
        grid_spec=pltpu.PrefetchScalarGridSpec(
            num_scalar_prefetch=1, grid=(n_tiles,),
            in_specs=[src(i) for i in range(PACK_NSRC)],
            out_specs=pl.BlockSpec((PACK_TILE, d), lambda n, t: (n, 0))),
        out_shape=jax.ShapeDtypeStruct((n_tiles * PACK_TILE, d), BF16),
        compiler_params=_cparams(("arbitrary",)),
        name="packw",
    )(tbl, *([w_in_t] * PACK_NSRC))


def _pad_lanes(v, n):
    return jnp.pad(v, ((0, 0), (0, n - v.shape[1])))


def kernel(x, c, positions, w_ada, b_ada, norm_mix_g, w_in, b_gates, conv_w, conv_b, q_lora_g, w_uq,
           kv_lora_g, w_ukv, q_norm_g, k_norm_g, mlstm_norm_g, w_out, norm_mlp_g, w_ff1, w_ff2):
    bsz, s, d = x.shape
    assert bsz == 1, "kernels are written for a single sequence"
    xs = x[0]
    pos_row = positions.reshape(1, s)
    half = jnp.arange(ROPE_HALF, dtype=F32)
    freq = (ROPE_THETA ** (-half / ROPE_HALF)).reshape(ROPE_HALF, 1)
    row = lambda v: v.reshape(1, -1).astype(F32)

    for l in range(w_ada.shape[0]):
        mod = _mod(c.reshape(d, 1), w_ada[l], row(b_ada[l]))
        proj, gates = _inproj(xs, row(norm_mix_g[l]), mod, _pack_w_in(w_in[l].T))

        wuq = jnp.pad(w_uq[l].reshape(-1, N_MLA_HEADS, MLA_QK),
                      ((0, 0), (0, 0), (0, MLA_QK_PAD - MLA_QK))).reshape(-1, N_MLA_HEADS * MLA_QK_PAD)
        q, k, v = _mlaprep(proj, pos_row, freq, row(q_lora_g[l]), row(kv_lora_g[l]),
                           wuq.astype(BF16), w_ukv[l].astype(BF16),
                           _pad_lanes(row(q_norm_g[l]), MLA_QK_PAD), _pad_lanes(row(k_norm_g[l]), MLA_QK_PAD))
        attn, (w_out_b, w_ff1_b, w_ff2_b) = _attention(q, k, v, (w_out[l], w_ff1[l], w_ff2[l]))

        qk = _conv(proj, conv_w[l], row(conv_b[l]))
        hf, hb = _mlstm(qk, proj, gates, _pad_lanes(row(b_gates[l]), LANES))

        x1, h2 = _outproj(attn, hf, hb, proj, xs, row(mlstm_norm_g[l]), w_out_b, mod, row(norm_mlp_g[l]))
        xs = _ffn(h2, w_ff1_b, w_ff2_b, x1, mod)
    return xs[None]
```

```python
import functools

import jax
import jax.numpy as jnp
from jax import lax
from jax.experimental import pallas as pl
from jax.experimental.pallas import tpu as pltpu

F32 = jnp.float32
BF16 = jnp.bfloat16

LANES = 128
N_MLA_HEADS = 8
MLA_NOPE = 128
ROPE_DIM = 64
ROPE_HALF = ROPE_DIM // 2
MLA_QK = MLA_NOPE + ROPE_DIM
MLA_QK_PAD = 256
MLA_V = 128
ROPE_THETA = 10000.0
N_MLSTM_HEADS = 4
MLSTM_DH = 256
MLSTM_CHUNK = 256
MLSTM_AUG = MLSTM_DH + LANES
CONV_WIDTH = 5
CONV_HALO = 8
EPS = 1e-6
M_INIT = -1e30
LOG2_E = 1.4426950408889634
SAFE_LOG2 = 60.0
VMEM_LIMIT = 56 * 1024 * 1024


def _cparams(sem):
    return pltpu.CompilerParams(dimension_semantics=sem, vmem_limit_bytes=VMEM_LIMIT)


def _mod_kernel(c_ref, w_ref, b_ref, o_ref, sb_ref, *, tn):
    @pl.when(pl.program_id(0) == 0)
    def _():
        cc = c_ref[...]
        sb_ref[...] = jnp.broadcast_to(cc * jax.nn.sigmoid(cc), sb_ref.shape)

    sb = sb_ref[...]
    for j in range(tn // LANES):
        sl = slice(j * LANES, (j + 1) * LANES)
        o_ref[:, sl] = jnp.sum(w_ref[:, sl] * sb, axis=0, keepdims=True) + b_ref[:, sl]


def _mod(c_col, w_ada, b_ada, tn=512):
    d, n = w_ada.shape
    return pl.pallas_call(
        functools.partial(_mod_kernel, tn=tn),
        grid=(n // tn,),
        in_specs=[pl.BlockSpec((d, 1), lambda j: (0, 0)),
                  pl.BlockSpec((d, tn), lambda j: (0, j)),
                  pl.BlockSpec((1, tn), lambda j: (0, j))],
        out_specs=pl.BlockSpec((1, tn), lambda j: (0, j)),
        out_shape=jax.ShapeDtypeStruct((1, n), F32),
        scratch_shapes=[pltpu.VMEM((d, LANES), F32)],
        compiler_params=_cparams(("arbitrary",)),
        name="mod",
    )(c_col, w_ada, b_ada)


def _modulated_norm(x, g, scale, shift):
    ms = jnp.mean(x * x, axis=-1, keepdims=True)
    return (x * lax.rsqrt(ms + EPS) * g) * (1.0 + scale) + shift


def _inproj_kernel(x_ref, g_ref, sc_ref, sh_ref, w_ref, o_ref, gate_ref, h_ref, *, n_last):
    n = pl.program_id(1)

    @pl.when(n == 0)
    def _():
        h_ref[...] = _modulated_norm(x_ref[...], g_ref[...], sc_ref[...], sh_ref[...]).astype(BF16)

    acc = lax.dot_general(h_ref[...], w_ref[...], (((1,), (1,)), ((), ())), preferred_element_type=F32)
    o_ref[...] = acc.astype(BF16)

    @pl.when(n == n_last)
    def _():
        gate_ref[...] = acc[:, -LANES:]


def _inproj(x, g, mod, w_t, tm=1024, tn=1024):
    s, d = x.shape
    n_tot = w_t.shape[0]
    return pl.pallas_call(
        functools.partial(_inproj_kernel, n_last=n_tot // tn - 1),
        grid=(s // tm, n_tot // tn),
        in_specs=[pl.BlockSpec((tm, d), lambda m, n: (m, 0)),
                  pl.BlockSpec((1, d), lambda m, n: (0, 0)),
                  pl.BlockSpec((1, d), lambda m, n: (0, 1)),
                  pl.BlockSpec((1, d), lambda m, n: (0, 0)),
                  pl.BlockSpec((tn, d), lambda m, n: (n, 0))],
        out_specs=[pl.BlockSpec((tm, tn), lambda m, n: (m, n)),
                   pl.BlockSpec((tm, LANES), lambda m, n: (m, 0))],
        out_shape=[jax.ShapeDtypeStruct((s, n_tot), BF16),
                   jax.ShapeDtypeStruct((s, LANES), F32)],
        scratch_shapes=[pltpu.VMEM((tm, d), BF16)],
        compiler_params=_cparams(("arbitrary", "arbitrary")),
        name="inproj",
    )(x, g, mod, mod, w_t)


def _rms_scale(x, n):
    return lax.rsqrt(jnp.sum(x * x, axis=-1, keepdims=True) * (1.0 / n) + EPS)


def _mlaprep_kernel(cq_ref, ckv_ref, kpe_ref, pos_ref, freq_ref, gql_ref, gkvl_ref, wuq_ref, wukv_ref,
                    gq_ref, gk_ref, q_ref, k_ref, v_ref):
    cq = cq_ref[...].astype(F32)
    cqn = (cq * _rms_scale(cq, cq.shape[-1]) * gql_ref[...]).astype(BF16)
    qf = jnp.dot(cqn, wuq_ref[...], preferred_element_type=F32)
    ckv = ckv_ref[...].astype(F32)
    ckvn = (ckv * _rms_scale(ckv, ckv.shape[-1]) * gkvl_ref[...]).astype(BF16)
    kvf = jnp.dot(ckvn, wukv_ref[...], preferred_element_type=F32)

    ang_t = freq_ref[...] * pos_ref[...].astype(F32)
    cos_t, sin_t = jnp.cos(ang_t), jnp.sin(ang_t)
    z_half, z_pad = jnp.zeros_like(cos_t), jnp.zeros((LANES - ROPE_DIM, cos_t.shape[1]), F32)
    c_tab = jnp.concatenate([cos_t, cos_t, z_pad], axis=0).T
    s_up = jnp.concatenate([z_half, sin_t, z_pad], axis=0).T
    s_dn = jnp.concatenate([-sin_t, z_half, z_pad], axis=0).T

    def rope(t):
        return (t * c_tab + pltpu.roll(t, ROPE_HALF, 1) * s_up
                + pltpu.roll(t, LANES - ROPE_HALF, 1) * s_dn)

    gq = gq_ref[...] * (LOG2_E * MLA_QK ** -0.5)
    ones_col = (lax.broadcasted_iota(jnp.int32, (kpe_ref.shape[0], MLA_V), 1) == 0).astype(BF16)
    gk = gk_ref[...]
    kpe = kpe_ref[...].astype(F32)
    kpe_ssq = jnp.sum(kpe * kpe, axis=-1, keepdims=True)
    kpe_rot = rope(kpe * gk[:, MLA_NOPE:])
    for h in range(N_MLA_HEADS):
        qh = qf[:, h * MLA_QK_PAD:(h + 1) * MLA_QK_PAD]
        qn = qh * _rms_scale(qh, MLA_QK) * gq
        q_ref[h] = jnp.concatenate([qn[:, :MLA_NOPE], rope(qn[:, MLA_NOPE:])], axis=-1).astype(BF16)
        kn = kvf[:, h * 256:h * 256 + MLA_NOPE]
        r = lax.rsqrt((jnp.sum(kn * kn, axis=-1, keepdims=True) + kpe_ssq) * (1.0 / MLA_QK) + EPS)
        k_ref[h] = jnp.concatenate([kn * r * gk[:, :MLA_NOPE], kpe_rot * r], axis=-1).astype(BF16)
        v_ref[h] = jnp.concatenate([kvf[:, h * 256 + MLA_NOPE:(h + 1) * 256].astype(BF16), ones_col], axis=-1)


def _mlaprep(proj, pos_row, freq, gql, gkvl, wuq, wukv, gq, gk, tm=512):
    s = proj.shape[0]
    hq = N_MLA_HEADS
    full = lambda shape: pl.BlockSpec(shape, lambda i: (0,) * len(shape))
    return pl.pallas_call(
        _mlaprep_kernel,
        grid=(s // tm,),
        in_specs=[pl.BlockSpec((tm, 512), lambda i: (i, 8)),
                  pl.BlockSpec((tm, 256), lambda i: (i, 18)),
                  pl.BlockSpec((tm, LANES), lambda i: (i, 38)),
                  pl.BlockSpec((1, tm), lambda i: (0, i)),
                  full((ROPE_HALF, 1)), full((1, 512)), full((1, 256)),
                  full(wuq.shape), full(wukv.shape), full((1, MLA_QK_PAD)), full((1, MLA_QK_PAD))],
        out_specs=[pl.BlockSpec((hq, tm, MLA_QK_PAD), lambda i: (0, i, 0)),
                   pl.BlockSpec((hq, tm, MLA_QK_PAD), lambda i: (0, i, 0)),
                   pl.BlockSpec((hq, tm, 2 * MLA_V), lambda i: (0, i, 0))],
        out_shape=[jax.ShapeDtypeStruct((hq, s, MLA_QK_PAD), BF16),
                   jax.ShapeDtypeStruct((hq, s, MLA_QK_PAD), BF16),
                   jax.ShapeDtypeStruct((hq, s, 2 * MLA_V), BF16)],
        compiler_params=_cparams(("arbitrary",)),
        name="mlaprep",
    )(proj, proj, proj, pos_row, freq, gql, gkvl, wuq, wukv, gq, gk)


def _attn_kernel(q_ref, k_ref, v_ref, *refs, n_side, shifted, tk, tk_shifted):
    side_in, o_ref, side_out = refs[:n_side], refs[n_side], refs[n_side + 1:]
    for src, dst in zip(side_in, side_out):
        dst[...] = src[...].astype(BF16)

    n_keys = k_ref.shape[1]
    q = q_ref[0]
    tq = q.shape[0]
    nt = (((1,), (1,)), ((), ()))

    if not shifted:
        acc = jnp.zeros((tq, 2 * MLA_V), F32)
        for j in range(n_keys // tk):
            s2 = lax.dot_general(q, k_ref[0, j * tk:(j + 1) * tk, :], nt, preferred_element_type=F32)
            acc = acc + jnp.dot(jnp.exp2(s2).astype(BF16), v_ref[0, j * tk:(j + 1) * tk, :],
                                preferred_element_type=F32)
    else:
        def body(j, carry):
            m, acc = carry
            start = pl.multiple_of(j * tk_shifted, tk_shifted)
            s2 = lax.dot_general(q, k_ref[0, pl.ds(start, tk_shifted), :], nt, preferred_element_type=F32)
            m_new = jnp.maximum(m, jnp.max(s2, axis=-1, keepdims=True))
            p = jnp.exp2(s2 - m_new).astype(BF16)
            acc = jnp.exp2(m - m_new) * acc + jnp.dot(p, v_ref[0, pl.ds(start, tk_shifted), :],
                                                      preferred_element_type=F32)
            return m_new, acc
        init = (jnp.full((tq, 1), -jnp.inf, F32), jnp.zeros((tq, 2 * MLA_V), F32))
        acc = lax.fori_loop(0, n_keys // tk_shifted, body, init)[1]
    o_ref[...] = (acc[:, :MLA_V] / acc[:, MLA_V:MLA_V + 1]).astype(BF16)


def _attention(q, k, v, side_weights, *, shifted, tq=512, tk=256, tk_shifted=1024):
    hq, s, _ = q.shape
    n_q = s // tq
    n_steps = hq * n_q
    slab = lambda w: pl.BlockSpec((w.shape[0] // n_steps, w.shape[1]), lambda h, i: (h * n_q + i, 0))
    outs = pl.pallas_call(
        functools.partial(_attn_kernel, n_side=len(side_weights), shifted=shifted, tk=tk, tk_shifted=tk_shifted),
        grid=(hq, n_q),
        in_specs=[pl.BlockSpec((1, tq, MLA_QK_PAD), lambda h, i: (h, i, 0)),
                  pl.BlockSpec((1, s, MLA_QK_PAD), lambda h, i: (h, 0, 0)),
                  pl.BlockSpec((1, s, 2 * MLA_V), lambda h, i: (h, 0, 0))] + [slab(w) for w in side_weights],
        out_specs=[pl.BlockSpec((tq, MLA_V), lambda h, i: (i, h))] + [slab(w) for w in side_weights],
        out_shape=[jax.ShapeDtypeStruct((s, hq * MLA_V), BF16)]
                  + [jax.ShapeDtypeStruct(w.shape, BF16) for w in side_weights],
        compiler_params=_cparams(("arbitrary", "arbitrary")),
        name="attn_shifted" if shifted else "attn",
    )(q, k, v, *side_weights)
    return outs[0], tuple(outs[1:])


def _conv_kernel(x_ref, w_ref, b_ref, o_ref, pad_ref, *, rows, n_q_tiles):
    s = x_ref.shape[0]
    zeros = jnp.zeros((CONV_HALO, LANES), F32)
    pad_ref[0:CONV_HALO, :] = zeros
    pad_ref[CONV_HALO + s:CONV_HALO + s + CONV_HALO, :] = zeros
    pad_ref[CONV_HALO:CONV_HALO + s, :] = x_ref[...].astype(F32)
    out_scale = jnp.where(pl.program_id(0) >= n_q_tiles, MLSTM_DH ** -0.5, 1.0).astype(F32)
    w = w_ref[...]
    b = b_ref[...]

    def body(r, carry):
        base = pl.multiple_of(r * rows, rows)
        acc = b
        for j in range(CONV_WIDTH):
            off = CONV_HALO + j - CONV_WIDTH // 2
            acc = acc + w[j:j + 1, :] * pad_ref[pl.ds(base + off, rows), :]
        o_ref[pl.ds(base, rows), :] = (acc * jax.nn.sigmoid(acc) * out_scale).astype(BF16)
        return carry

    lax.fori_loop(0, s // rows, body, 0)


def _conv(proj, conv_w, conv_b, rows=256):
    s = proj.shape[0]
    n_ch = conv_w.shape[1]
    return pl.pallas_call(
        functools.partial(_conv_kernel, rows=rows, n_q_tiles=n_ch // 2 // LANES),
        grid=(n_ch // LANES,),
        in_specs=[pl.BlockSpec((s, LANES), lambda j: (0, j)),
                  pl.BlockSpec((CONV_WIDTH, LANES), lambda j: (0, j)),
                  pl.BlockSpec((1, LANES), lambda j: (0, j))],
        out_specs=pl.BlockSpec((s, LANES), lambda j: (0, j)),
        out_shape=jax.ShapeDtypeStruct((s, n_ch), BF16),
        scratch_shapes=[pltpu.VMEM((s + 2 * CONV_HALO, LANES), F32)],
        compiler_params=_cparams(("arbitrary",)),
        name="conv",
    )(proj, conv_w, conv_b)


def _mlstm_chain(q, k, v_aug, g, g_t, lf, b_c, b_ct, mask, ic, fc, c_ref, m_ref, chain):
    b_col, b_row = b_c[:, fc:fc + 1], b_ct[fc:fc + 1, :]
    i_col, i_row = g[:, ic:ic + 1], g_t[ic:ic + 1, :]
    b_tot = jnp.sum(lf[:, fc:fc + 1], axis=0, keepdims=True)
    m_prev = m_ref[chain:chain + 1, 0:1]

    log_d = jnp.where(mask, b_col - b_row + i_row, -jnp.inf)
    log_inter = b_col + m_prev
    m_t = jnp.maximum(log_inter, jnp.max(log_d, axis=-1, keepdims=True))
    d_m = jnp.exp(log_d - m_t)
    w_inter = jnp.exp(log_inter - m_t)

    c_aug = c_ref[chain]
    scores = lax.dot_general(q, k, (((1,), (1,)), ((), ())), preferred_element_type=F32) * d_m
    r = (jnp.dot(scores.astype(BF16), v_aug, preferred_element_type=F32)
         + w_inter * jnp.dot(q, c_aug.astype(BF16), preferred_element_type=F32))
    num, den = r[:, :MLSTM_DH], r[:, MLSTM_DH:MLSTM_DH + 1]
    h = num / jnp.maximum(jnp.abs(den), jnp.exp(-m_t))

    log_w = b_tot - b_col + i_col
    m_new = jnp.maximum(b_tot + m_prev, jnp.max(log_w, axis=0, keepdims=True))
    decay = jnp.exp(b_tot + m_prev - m_new)
    kw = (k.astype(F32) * jnp.exp(log_w - m_new)).astype(BF16)
    upd = lax.dot_general(kw, v_aug, (((0,), (0,)), ((), ())), preferred_element_type=F32)
    c_ref[chain] = decay * c_aug + upd
    m_ref[chain:chain + 1, :] = jnp.broadcast_to(m_new, (1, LANES))
    return h


def _mlstm_kernel(qf_ref, kf_ref, vf_ref, gf_ref, qb_ref, kb_ref, vb_ref, gb_ref, bg_ref,
                  hf_ref, hb_ref, c_ref, m_ref):
    @pl.when(pl.program_id(0) == 0)
    def _():
        c_ref[...] = jnp.zeros(c_ref.shape, F32)
        m_ref[...] = jnp.full(m_ref.shape, M_INIT, F32)

    L = MLSTM_CHUNK
    row = lax.broadcasted_iota(jnp.int32, (L, L), 0)
    col = lax.broadcasted_iota(jnp.int32, (L, L), 1)
    ones_col = (lax.broadcasted_iota(jnp.int32, (L, LANES), 1) == 0).astype(BF16)
    hm = N_MLSTM_HEADS
    for d, (q_ref, k_ref, v_ref, g_ref, o_ref) in enumerate(
            ((qf_ref, kf_ref, vf_ref, gf_ref, hf_ref), (qb_ref, kb_ref, vb_ref, gb_ref, hb_ref))):
        mask = (row >= col) if d == 0 else (row <= col)
        g = g_ref[...] + bg_ref[...]
        lf = jax.nn.log_sigmoid(g)
        b_c = jnp.dot(mask.astype(F32), lf, preferred_element_type=F32,
                      precision=lax.Precision.HIGHEST)
        g_t, b_ct = g.T, b_c.T
        for h in range(hm):
            sl = slice(h * MLSTM_DH, (h + 1) * MLSTM_DH)
            v_aug = jnp.concatenate([v_ref[:, sl], ones_col], axis=-1)
            o_ref[:, sl] = _mlstm_chain(q_ref[:, sl], k_ref[:, sl], v_aug, g, g_t, lf, b_c, b_ct, mask,
                                        2 * d * hm + h, (2 * d + 1) * hm + h, c_ref, m_ref, d * hm + h)


def _mlstm(qk, proj, gates, bg):
    s = qk.shape[0]
    L = MLSTM_CHUNK
    nc = s // L
    w = N_MLSTM_HEADS * MLSTM_DH
    fwd = lambda blk: (lambda i: (i, blk))
    bwd = lambda blk: (lambda i: (nc - 1 - i, blk))
    specs = lambda ix: [pl.BlockSpec((L, w), ix(0)), pl.BlockSpec((L, w), ix(1)),
                        pl.BlockSpec((L, w), ix(2)), pl.BlockSpec((L, LANES), ix(0))]
    return pl.pallas_call(
        _mlstm_kernel,
        grid=(nc,),
        in_specs=specs(fwd) + specs(bwd) + [pl.BlockSpec((1, LANES), lambda i: (0, 0))],
        out_specs=[pl.BlockSpec((L, w), fwd(0)), pl.BlockSpec((L, w), bwd(0))],
        out_shape=[jax.ShapeDtypeStruct((s, w), F32), jax.ShapeDtypeStruct((s, w), F32)],
        scratch_shapes=[pltpu.VMEM((2 * N_MLSTM_HEADS, MLSTM_DH, MLSTM_AUG), F32),
                        pltpu.VMEM((2 * N_MLSTM_HEADS, LANES), F32)],
        compiler_params=_cparams(("arbitrary",)),
        name="mlstm",
    )(qk, qk, proj, gates, qk, qk, proj, gates, bg)


def _outproj_kernel(attn_ref, hf_ref, hb_ref, om_ref, x_ref, gm_ref, w_ref, gate_ref, g2_ref, sc_ref, sh_ref,
                    x1_ref, h2_ref):
    hsum = hf_ref[...] + hb_ref[...]
    gm = gm_ref[...]
    parts = []
    for h in range(N_MLSTM_HEADS):
        sl = slice(h * MLSTM_DH, (h + 1) * MLSTM_DH)
        seg = hsum[:, sl]
        parts.append(seg * _rms_scale(seg, MLSTM_DH) * gm[:, sl])
    ml = (jax.nn.sigmoid(om_ref[...].astype(F32)) * jnp.concatenate(parts, axis=-1)).astype(BF16)
    n_attn = attn_ref.shape[1]
    mixed = (jnp.dot(attn_ref[...], w_ref[:n_attn, :], preferred_element_type=F32)
             + jnp.dot(ml, w_ref[n_attn:, :], preferred_element_type=F32))
    x1 = x_ref[...] + gate_ref[...] * mixed
    x1_ref[...] = x1
    h2_ref[...] = _modulated_norm(x1, g2_ref[...], sc_ref[...], sh_ref[...]).astype(BF16)


def _outproj(attn, hf, hb, proj, x, gm, w_out, mod, g2, tm=512):
    s, d = x.shape
    wm = hf.shape[1]
    row = lambda blk: pl.BlockSpec((1, d), lambda i: (0, blk))
    return pl.pallas_call(
        _outproj_kernel,
        grid=(s // tm,),
        in_specs=[pl.BlockSpec((tm, attn.shape[1]), lambda i: (i, 0)),
                  pl.BlockSpec((tm, wm), lambda i: (i, 0)),
                  pl.BlockSpec((tm, wm), lambda i: (i, 0)),
                  pl.BlockSpec((tm, wm), lambda i: (i, 3)),
                  pl.BlockSpec((tm, d), lambda i: (i, 0)),
                  pl.BlockSpec((1, wm), lambda i: (0, 0)),
                  pl.BlockSpec(w_out.shape, lambda i: (0, 0), pipeline_mode=pl.Buffered(1)),
                  row(2), row(0), row(4), row(3)],
        out_specs=[pl.BlockSpec((tm, d), lambda i: (i, 0)), pl.BlockSpec((tm, d), lambda i: (i, 0))],
        out_shape=[jax.ShapeDtypeStruct((s, d), F32), jax.ShapeDtypeStruct((s, d), BF16)],
        compiler_params=_cparams(("arbitrary",)),
        name="outproj",
    )(attn, hf, hb, proj, x, gm, w_out, mod, g2, mod, mod)


def _ffn_kernel(h2_ref, w1_ref, w2_ref, x1_ref, gate_ref, o_ref, *, f_last):
    f = pl.program_id(1)

    @pl.when(f == 0)
    def _():
        o_ref[...] = jnp.zeros(o_ref.shape, F32)

    a = jnp.maximum(jnp.dot(h2_ref[...], w1_ref[...], preferred_element_type=F32), 0.0)
    o_ref[...] += jnp.dot((a * a).astype(BF16), w2_ref[...], preferred_element_type=F32)

    @pl.when(f == f_last)
    def _():
        o_ref[...] = x1_ref[...] + gate_ref[...] * o_ref[...]


def _ffn(h2, w1, w2, x1, mod, tm=1024, tf=512):
    s, d = h2.shape
    dff = w1.shape[1]
    return pl.pallas_call(
        functools.partial(_ffn_kernel, f_last=dff // tf - 1),
        grid=(s // tm, dff // tf),
        in_specs=[pl.BlockSpec((tm, d), lambda m, f: (m, 0)),
                  pl.BlockSpec((d, tf), lambda m, f: (0, f)),
                  pl.BlockSpec((tf, d), lambda m, f: (f, 0)),
                  pl.BlockSpec((tm, d), lambda m, f: (m, 0), pipeline_mode=pl.Buffered(1)),
                  pl.BlockSpec((1, d), lambda m, f: (0, 5))],
        out_specs=pl.BlockSpec((tm, d), lambda m, f: (m, 0)),
        out_shape=jax.ShapeDtypeStruct((s, d), F32),
        compiler_params=_cparams(("arbitrary", "arbitrary")),
        name="ffn",
    )(h2, w1, w2, x1, mod)


PACK_TILE = 512
PACK_SRC = 64
PACK_NSRC = PACK_TILE // PACK_SRC
W_IN_MIX_START = 832
W_IN_GATES_START = 4928
N_GATE_COLS = 16


def _pack_src_blocks():
    rows = []
    n_mix = 4 * N_MLSTM_HEADS * MLSTM_DH // PACK_TILE
    for n in range(n_mix):
        first = (W_IN_MIX_START + n * PACK_TILE) // PACK_SRC
        rows.append([first + i for i in range(PACK_NSRC)])
    rows.append(list(range(PACK_NSRC)))
    gates_blk = W_IN_GATES_START // PACK_SRC
    rows.append([8, 9, 10, 11, 12, 12, gates_blk, gates_blk])
    return jnp.asarray(rows, jnp.int32).reshape(-1)


def _pack_kernel(tbl_ref, *refs, n_mix):
    del tbl_ref
    srcs, o_ref = refs[:-1], refs[-1]
    n = pl.program_id(0)
    blk = lambda i: slice(i * PACK_SRC, (i + 1) * PACK_SRC)

    @pl.when(n <= n_mix)
    def _():
        for i, b in enumerate(srcs):
            o_ref[blk(i), :] = b[...].astype(BF16)

    @pl.when(n == n_mix + 1)
    def _():
        zeros = jnp.zeros(srcs[0].shape, BF16)
        for i in range(5):
            o_ref[blk(i), :] = srcs[i][...].astype(BF16)
        o_ref[blk(5), :] = zeros
        rows = lax.broadcasted_iota(jnp.int32, srcs[6].shape, 0)
        o_ref[blk(6), :] = jnp.where(rows < N_GATE_COLS, srcs[6][...], 0.0).astype(BF16)
        o_ref[blk(7), :] = zeros


def _pack_w_in(w_in_t):
    d = w_in_t.shape[1]
    tbl = _pack_src_blocks()
    n_tiles = tbl.shape[0] // PACK_NSRC
    src = lambda i: pl.BlockSpec((PACK_SRC, d), lambda n, t: (t[n * PACK_NSRC + i], 0))
    return pl.pallas_call(
        functools.partial(_pack_kernel, n_mix=n_tiles - 2),
        grid_spec=pltpu.PrefetchScalarGridSpec(
            num_scalar_prefetch=1, grid=(n_tiles,),
            in_specs=[src(i) for i in range(PACK_NSRC)],
            out_specs=pl.BlockSpec((PACK_TILE, d), lambda n, t: (n, 0))),
        out_shape=jax.ShapeDtypeStruct((n_tiles * PACK_TILE, d), BF16),
        compiler_params=_cparams(("arbitrary",)),
        name="packw",
    )(tbl, *([w_in_t] * PACK_NSRC))


def _pad_lanes(v, n):
    return jnp.pad(v, ((0, 0), (0, n - v.shape[1])))


def kernel(x, c, positions, w_ada, b_ada, norm_mix_g, w_in, b_gates, conv_w, conv_b, q_lora_g, w_uq,
           kv_lora_g, w_ukv, q_norm_g, k_norm_g, mlstm_norm_g, w_out, norm_mlp_g, w_ff1, w_ff2):
    bsz, s, d = x.shape
    assert bsz == 1, "kernels are written for a single sequence"
    xs = x[0]
    pos_row = positions.reshape(1, s)
    half = jnp.arange(ROPE_HALF, dtype=F32)
    freq = (ROPE_THETA ** (-half / ROPE_HALF)).reshape(ROPE_HALF, 1)
    row = lambda v: v.reshape(1, -1).astype(F32)

    for l in range(w_ada.shape[0]):
        mod = _mod(c.reshape(d, 1), w_ada[l], row(b_ada[l]))
        proj, gates = _inproj(xs, row(norm_mix_g[l]), mod, _pack_w_in(w_in[l].T))

        wuq = jnp.pad(w_uq[l].reshape(-1, N_MLA_HEADS, MLA_QK),
                      ((0, 0), (0, 0), (0, MLA_QK_PAD - MLA_QK))).reshape(-1, N_MLA_HEADS * MLA_QK_PAD)
        g_q, g_k = row(q_norm_g[l]), row(k_norm_g[l])
        q, k, v = _mlaprep(proj, pos_row, freq, row(q_lora_g[l]), row(kv_lora_g[l]),
                           wuq.astype(BF16), w_ukv[l].astype(BF16),
                           _pad_lanes(g_q, MLA_QK_PAD), _pad_lanes(g_k, MLA_QK_PAD))
        logit_bound = 1.02 * LOG2_E * MLA_QK ** 0.5 * jnp.max(jnp.abs(g_q)) * jnp.max(jnp.abs(g_k))
        attn, (w_out_b, w_ff1_b, w_ff2_b) = lax.cond(
            logit_bound <= SAFE_LOG2,
            functools.partial(_attention, shifted=False), functools.partial(_attention, shifted=True),
            q, k, v, (w_out[l], w_ff1[l], w_ff2[l]))

        qk = _conv(proj, conv_w[l], row(conv_b[l]))
        hf, hb = _mlstm(qk, proj, gates, _pad_lanes(row(b_gates[l]), LANES))

        x1, h2 = _outproj(attn, hf, hb, proj, xs, row(mlstm_norm_g[l]), w_out_b, mod, row(norm_mlp_g[l]))
        xs = _ffn(h2, w_ff1_b, w_ff2_b, x1, mod)
    return xs[None]
```

```python
import functools

import jax
import jax.numpy as jnp
from jax import lax
from jax.experimental import pallas as pl
from jax.experimental.pallas import tpu as pltpu

F32 = jnp.float32
BF16 = jnp.bfloat16

LANES = 128
N_MLA_HEADS = 8
MLA_NOPE = 128
ROPE_DIM = 64
ROPE_HALF = ROPE_DIM // 2
MLA_QK = MLA_NOPE + ROPE_DIM
MLA_QK_PAD = 256
MLA_V = 128
MLA_V_AUG = MLA_V + 16
ATTN_TK = 256
ROPE_THETA = 10000.0
N_MLSTM_HEADS = 4
MLSTM_DH = 256
MLSTM_CHUNK = 256
MLSTM_AUG = MLSTM_DH + LANES
CONV_WIDTH = 5
CONV_HALO = 8
EPS = 1e-6
M_INIT = -1e30
LOG2_E = 1.4426950408889634
SAFE_LOG2 = 60.0
VMEM_LIMIT = 56 * 1024 * 1024


def _cparams(sem):
    return pltpu.CompilerParams(dimension_semantics=sem, vmem_limit_bytes=VMEM_LIMIT)


def _mod_kernel(c_ref, w_ref, b_ref, o_ref, sb_ref, *, tn):
    @pl.when(pl.program_id(0) == 0)
    def _():
        cc = c_ref[...]
        sb_ref[...] = jnp.broadcast_to(cc * jax.nn.sigmoid(cc), sb_ref.shape)

    sb = sb_ref[...]
    for j in range(tn // LANES):
        sl = slice(j * LANES, (j + 1) * LANES)
        o_ref[:, sl] = jnp.sum(w_ref[:, sl] * sb, axis=0, keepdims=True) + b_ref[:, sl]


def _mod(c_col, w_ada, b_ada, tn=512):
    d, n = w_ada.shape
    return pl.pallas_call(
        functools.partial(_mod_kernel, tn=tn),
        grid=(n // tn,),
        in_specs=[pl.BlockSpec((d, 1), lambda j: (0, 0)),
                  pl.BlockSpec((d, tn), lambda j: (0, j)),
                  pl.BlockSpec((1, tn), lambda j: (0, j))],
        out_specs=pl.BlockSpec((1, tn), lambda j: (0, j)),
        out_shape=jax.ShapeDtypeStruct((1, n), F32),
        scratch_shapes=[pltpu.VMEM((d, LANES), F32)],
        compiler_params=_cparams(("arbitrary",)),
        name="mod",
    )(c_col, w_ada, b_ada)


def _modulated_norm(x, g, scale, shift):
    ms = jnp.mean(x * x, axis=-1, keepdims=True)
    return (x * lax.rsqrt(ms + EPS) * g) * (1.0 + scale) + shift


def _inproj_kernel(x_ref, g_ref, sc_ref, sh_ref, w_ref, o_ref, gate_ref, h_ref, *, n_last):
    n = pl.program_id(1)

    @pl.when(n == 0)
    def _():
        h_ref[...] = _modulated_norm(x_ref[...], g_ref[...], sc_ref[...], sh_ref[...]).astype(BF16)

    acc = lax.dot_general(h_ref[...], w_ref[...], (((1,), (1,)), ((), ())), preferred_element_type=F32)
    o_ref[...] = acc.astype(BF16)

    @pl.when(n == n_last)
    def _():
        gate_ref[...] = acc[:, -LANES:]


def _inproj(x, g, mod, w_t, tm=1024, tn=1024):
    s, d = x.shape
    n_tot = w_t.shape[0]
    return pl.pallas_call(
        functools.partial(_inproj_kernel, n_last=n_tot // tn - 1),
        grid=(s // tm, n_tot // tn),
        in_specs=[pl.BlockSpec((tm, d), lambda m, n: (m, 0)),
                  pl.BlockSpec((1, d), lambda m, n: (0, 0)),
                  pl.BlockSpec((1, d), lambda m, n: (0, 1)),
                  pl.BlockSpec((1, d), lambda m, n: (0, 0)),
                  pl.BlockSpec((tn, d), lambda m, n: (n, 0))],
        out_specs=[pl.BlockSpec((tm, tn), lambda m, n: (m, n)),
                   pl.BlockSpec((tm, LANES), lambda m, n: (m, 0))],
        out_shape=[jax.ShapeDtypeStruct((s, n_tot), BF16),
                   jax.ShapeDtypeStruct((s, LANES), F32)],
        scratch_shapes=[pltpu.VMEM((tm, d), BF16)],
        compiler_params=_cparams(("arbitrary", "arbitrary")),
        name="inproj",
    )(x, g, mod, mod, w_t)


def _rms_scale(x, n):
    return lax.rsqrt(jnp.sum(x * x, axis=-1, keepdims=True) * (1.0 / n) + EPS)


def _mlaprep_kernel(cq_ref, ckv_ref, kpe_ref, pos_ref, freq_ref, gql_ref, gkvl_ref, wuqt_ref, wk_ref, wvt_ref,
                    gqc_ref, gk_ref, q_ref, k_ref, v_ref):
    tm = cq_ref.shape[0]
    nt = (((1,), (1,)), ((), ()))
    cq = cq_ref[...].astype(F32)
    cqn = (cq * _rms_scale(cq, cq.shape[-1]) * gql_ref[...]).astype(BF16)
    ckv = ckv_ref[...].astype(F32)
    ckvn = (ckv * _rms_scale(ckv, ckv.shape[-1]) * gkvl_ref[...]).astype(BF16)
    qf_t = lax.dot_general(wuqt_ref[...], cqn, nt, preferred_element_type=F32)
    kf = jnp.dot(ckvn, wk_ref[...], preferred_element_type=F32)
    vf_t = lax.dot_general(wvt_ref[...], ckvn, nt, preferred_element_type=F32)

    ang_t = freq_ref[...] * pos_ref[...].astype(F32)
    cos_t, sin_t = jnp.cos(ang_t), jnp.sin(ang_t)

    gq_b = jnp.broadcast_to(gqc_ref[...] * (LOG2_E * MLA_QK ** -0.5), (MLA_QK, tm))
    q_pad = jnp.zeros((MLA_QK_PAD - MLA_QK, tm), F32)
    for h in range(N_MLA_HEADS):
        qh = qf_t[h * MLA_QK_PAD:h * MLA_QK_PAD + MLA_QK]
        r = lax.rsqrt(jnp.sum(qh * qh, axis=0, keepdims=True) * (1.0 / MLA_QK) + EPS)
        qn = qh * r * gq_b
        x1, x2 = qn[MLA_NOPE:MLA_NOPE + ROPE_HALF], qn[MLA_NOPE + ROPE_HALF:]
        q_ref[h] = jnp.concatenate([qn[:MLA_NOPE], x1 * cos_t - x2 * sin_t, x1 * sin_t + x2 * cos_t, q_pad],
                                   axis=0).astype(BF16)

    ones_rows = (lax.broadcasted_iota(jnp.int32, (MLA_V_AUG - MLA_V, ATTN_TK), 0) == 0).astype(BF16)
    for h in range(N_MLA_HEADS):
        for c in range(tm // ATTN_TK):
            vc = vf_t[h * MLA_V:(h + 1) * MLA_V, c * ATTN_TK:(c + 1) * ATTN_TK]
            v_ref[h, c] = jnp.concatenate([vc.astype(BF16), ones_rows], axis=0)

    z_half, z_pad = jnp.zeros_like(cos_t), jnp.zeros((LANES - ROPE_DIM, tm), F32)
    c_tab = jnp.concatenate([cos_t, cos_t, z_pad], axis=0).T
    s_up = jnp.concatenate([z_half, sin_t, z_pad], axis=0).T
    s_dn = jnp.concatenate([-sin_t, z_half, z_pad], axis=0).T
    gk = gk_ref[...]
    kpe = kpe_ref[...].astype(F32)
    kpe_g = kpe * gk[:, MLA_NOPE:]
    kpe_rot = (kpe_g * c_tab + pltpu.roll(kpe_g, ROPE_HALF, 1) * s_up
               + pltpu.roll(kpe_g, LANES - ROPE_HALF, 1) * s_dn)
    kpe_ssq = jnp.sum(kpe * kpe, axis=-1, keepdims=True)
    for h in range(N_MLA_HEADS):
        kn = kf[:, h * MLA_NOPE:(h + 1) * MLA_NOPE]
        r = lax.rsqrt((jnp.sum(kn * kn, axis=-1, keepdims=True) + kpe_ssq) * (1.0 / MLA_QK) + EPS)
        k_ref[h] = jnp.concatenate([kn * r * gk[:, :MLA_NOPE], kpe_rot * r], axis=-1).astype(BF16)


def _mlaprep(proj, pos_row, freq, gql, gkvl, wuq_t, wk, wv_t, gq_col, gk, tm=512):
    s = proj.shape[0]
    hq = N_MLA_HEADS
    full = lambda a: pl.BlockSpec(a.shape, lambda i: (0,) * a.ndim)
    return pl.pallas_call(
        _mlaprep_kernel,
        grid=(s // tm,),
        in_specs=[pl.BlockSpec((tm, 512), lambda i: (i, 8)),
                  pl.BlockSpec((tm, 256), lambda i: (i, 18)),
                  pl.BlockSpec((tm, LANES), lambda i: (i, 38)),
                  pl.BlockSpec((1, tm), lambda i: (0, i)),
                  full(freq), full(gql), full(gkvl), full(wuq_t), full(wk), full(wv_t), full(gq_col), full(gk)],
        out_specs=[pl.BlockSpec((hq, MLA_QK_PAD, tm), lambda i: (0, 0, i)),
                   pl.BlockSpec((hq, tm, MLA_QK_PAD), lambda i: (0, i, 0)),
                   pl.BlockSpec((hq, tm // ATTN_TK, MLA_V_AUG, ATTN_TK), lambda i: (0, i, 0, 0))],
        out_shape=[jax.ShapeDtypeStruct((hq, MLA_QK_PAD, s), BF16),
                   jax.ShapeDtypeStruct((hq, s, MLA_QK_PAD), BF16),
                   jax.ShapeDtypeStruct((hq, s // ATTN_TK, MLA_V_AUG, ATTN_TK), BF16)],
        compiler_params=_cparams(("arbitrary",)),
        name="mlaprep",
    )(proj, proj, proj, pos_row, freq, gql, gkvl, wuq_t, wk, wv_t, gq_col, gk)


def _attn_kernel(qt_ref, k_ref, vt_ref, *refs, n_side, shifted, tk, ahead):
    side_in, o_ref, side_out = refs[:n_side], refs[n_side], refs[n_side + 1:]
    for src, dst in zip(side_in, side_out):
        dst[...] = src[...].astype(BF16)

    qt = qt_ref[0]
    tq = qt.shape[1]
    n_chunks = k_ref.shape[1] // tk

    def logits_t(j):
        return jnp.dot(k_ref[0, j * tk:(j + 1) * tk, :], qt, preferred_element_type=F32)

    def values_t(j, p_t):
        return jnp.dot(vt_ref[0, j], p_t, preferred_element_type=F32)

    m = jnp.full((1, tq), -jnp.inf, F32)
    acc = None
    pending = [logits_t(j) for j in range(ahead)]
    for j in range(n_chunks):
        if j + ahead < n_chunks:
            pending.append(logits_t(j + ahead))
        s2 = pending.pop(0)
        if shifted:
            m_new = jnp.maximum(m, jnp.max(s2, axis=0, keepdims=True))
            pv = values_t(j, jnp.exp2(s2 - m_new).astype(BF16))
            acc = pv if acc is None else acc * jnp.exp2(m - m_new) + pv
            m = m_new
        else:
            pv = values_t(j, jnp.exp2(s2).astype(BF16))
            acc = pv if acc is None else acc + pv
    o_ref[...] = (acc[:MLA_V] / acc[MLA_V:MLA_V + 1]).T.astype(BF16)


def _attention(q_t, k, v_t, side_weights, *, shifted, tq=512, tk=ATTN_TK, ahead=2):
    hq, s, _ = k.shape
    n_q = s // tq
    n_steps = hq * n_q
    slab = lambda w: pl.BlockSpec((w.shape[0] // n_steps, w.shape[1]), lambda h, i: (h * n_q + i, 0))
    outs = pl.pallas_call(
        functools.partial(_attn_kernel, n_side=len(side_weights), shifted=shifted, tk=tk, ahead=ahead),
        grid=(hq, n_q),
        in_specs=[pl.BlockSpec((1, MLA_QK_PAD, tq), lambda h, i: (h, 0, i)),
                  pl.BlockSpec((1, s, MLA_QK_PAD), lambda h, i: (h, 0, 0)),
                  pl.BlockSpec((1, s // tk, MLA_V_AUG, tk), lambda h, i: (h, 0, 0, 0))]
                 + [slab(w) for w in side_weights],
        out_specs=[pl.BlockSpec((tq, MLA_V), lambda h, i: (i, h))] + [slab(w) for w in side_weights],
        out_shape=[jax.ShapeDtypeStruct((s, hq * MLA_V), BF16)]
                  + [jax.ShapeDtypeStruct(w.shape, BF16) for w in side_weights],
        compiler_params=_cparams(("arbitrary", "arbitrary")),
        name="attn_shifted" if shifted else "attn",
    )(q_t, k, v_t, *side_weights)
    return outs[0], tuple(outs[1:])


def _conv_kernel(x_ref, w_ref, b_ref, o_ref, pad_ref, *, rows, n_q_tiles):
    s = x_ref.shape[0]
    zeros = jnp.zeros((CONV_HALO, LANES), F32)
    pad_ref[0:CONV_HALO, :] = zeros
    pad_ref[CONV_HALO + s:CONV_HALO + s + CONV_HALO, :] = zeros
    pad_ref[CONV_HALO:CONV_HALO + s, :] = x_ref[...].astype(F32)
    out_scale = jnp.where(pl.program_id(0) >= n_q_tiles, MLSTM_DH ** -0.5, 1.0).astype(F32)
    w = w_ref[...]
    b = b_ref[...]

    def body(r, carry):
        base = pl.multiple_of(r * rows, rows)
        acc = b
        for j in range(CONV_WIDTH):
            off = CONV_HALO + j - CONV_WIDTH // 2
            acc = acc + w[j:j + 1, :] * pad_ref[pl.ds(base + off, rows), :]
        o_ref[pl.ds(base, rows), :] = (acc * jax.nn.sigmoid(acc) * out_scale).astype(BF16)
        return carry

    lax.fori_loop(0, s // rows, body, 0)


def _conv(proj, conv_w, conv_b, rows=256):
    s = proj.shape[0]
    n_ch = conv_w.shape[1]
    return pl.pallas_call(
        functools.partial(_conv_kernel, rows=rows, n_q_tiles=n_ch // 2 // LANES),
        grid=(n_ch // LANES,),
        in_specs=[pl.BlockSpec((s, LANES), lambda j: (0, j)),
                  pl.BlockSpec((CONV_WIDTH, LANES), lambda j: (0, j)),
                  pl.BlockSpec((1, LANES), lambda j: (0, j))],
        out_specs=pl.BlockSpec((s, LANES), lambda j: (0, j)),
        out_shape=jax.ShapeDtypeStruct((s, n_ch), BF16),
        scratch_shapes=[pltpu.VMEM((s + 2 * CONV_HALO, LANES), F32)],
        compiler_params=_cparams(("arbitrary",)),
        name="conv",
    )(proj, conv_w, conv_b)


def _mlstm_chain(q, k, v_aug, g, g_t, lf, b_c, b_ct, mask, ic, fc, c_ref, m_ref, chain):
    b_col, b_row = b_c[:, fc:fc + 1], b_ct[fc:fc + 1, :]
    i_col, i_row = g[:, ic:ic + 1], g_t[ic:ic + 1, :]
    b_tot = jnp.sum(lf[:, fc:fc + 1], axis=0, keepdims=True)
    m_prev = m_ref[chain:chain + 1, 0:1]

    log_d = jnp.where(mask, b_col - b_row + i_row, -jnp.inf)
    log_inter = b_col + m_prev
    m_t = jnp.maximum(log_inter, jnp.max(log_d, axis=-1, keepdims=True))
    d_m = jnp.exp(log_d - m_t)
    w_inter = jnp.exp(log_inter - m_t)

    c_aug = c_ref[chain]
    scores = lax.dot_general(q, k, (((1,), (1,)), ((), ())), preferred_element_type=F32) * d_m
    r = (jnp.dot(scores.astype(BF16), v_aug, preferred_element_type=F32)
         + w_inter * jnp.dot(q, c_aug.astype(BF16), preferred_element_type=F32))
    num, den = r[:, :MLSTM_DH], r[:, MLSTM_DH:MLSTM_DH + 1]
    h = num / jnp.maximum(jnp.abs(den), jnp.exp(-m_t))

    log_w = b_tot - b_col + i_col
    m_new = jnp.maximum(b_tot + m_prev, jnp.max(log_w, axis=0, keepdims=True))
    decay = jnp.exp(b_tot + m_prev - m_new)
    kw = (k.astype(F32) * jnp.exp(log_w - m_new)).astype(BF16)
    upd = lax.dot_general(kw, v_aug, (((0,), (0,)), ((), ())), preferred_element_type=F32)
    c_ref[chain] = decay * c_aug + upd
    m_ref[chain:chain + 1, :] = jnp.broadcast_to(m_new, (1, LANES))
    return h


def _mlstm_kernel(qf_ref, kf_ref, vf_ref, gf_ref, qb_ref, kb_ref, vb_ref, gb_ref, bg_ref,
                  hf_ref, hb_ref, c_ref, m_ref):
    @pl.when(pl.program_id(0) == 0)
    def _():
        c_ref[...] = jnp.zeros(c_ref.shape, F32)
        m_ref[...] = jnp.full(m_ref.shape, M_INIT, F32)

    L = MLSTM_CHUNK
    row = lax.broadcasted_iota(jnp.int32, (L, L), 0)
    col = lax.broadcasted_iota(jnp.int32, (L, L), 1)
    ones_col = (lax.broadcasted_iota(jnp.int32, (L, LANES), 1) == 0).astype(BF16)
    hm = N_MLSTM_HEADS
    for d, (q_ref, k_ref, v_ref, g_ref, o_ref) in enumerate(
            ((qf_ref, kf_ref, vf_ref, gf_ref, hf_ref), (qb_ref, kb_ref, vb_ref, gb_ref, hb_ref))):
        mask = (row >= col) if d == 0 else (row <= col)
        g = g_ref[...] + bg_ref[...]
        lf = jax.nn.log_sigmoid(g)
        b_c = jnp.dot(mask.astype(F32), lf, preferred_element_type=F32,
                      precision=lax.Precision.HIGHEST)
        g_t, b_ct = g.T, b_c.T
        for h in range(hm):
            sl = slice(h * MLSTM_DH, (h + 1) * MLSTM_DH)
            v_aug = jnp.concatenate([v_ref[:, sl], ones_col], axis=-1)
            o_ref[:, sl] = _mlstm_chain(q_ref[:, sl], k_ref[:, sl], v_aug, g, g_t, lf, b_c, b_ct, mask,
                                        2 * d * hm + h, (2 * d + 1) * hm + h, c_ref, m_ref, d * hm + h)


def _mlstm(qk, proj, gates, bg):
    s = qk.shape[0]
    L = MLSTM_CHUNK
    nc = s // L
    w = N_MLSTM_HEADS * MLSTM_DH
    fwd = lambda blk: (lambda i: (i, blk))
    bwd = lambda blk: (lambda i: (nc - 1 - i, blk))
    specs = lambda ix: [pl.BlockSpec((L, w), ix(0)), pl.BlockSpec((L, w), ix(1)),
                        pl.BlockSpec((L, w), ix(2)), pl.BlockSpec((L, LANES), ix(0))]
    return pl.pallas_call(
        _mlstm_kernel,
        grid=(nc,),
        in_specs=specs(fwd) + specs(bwd) + [pl.BlockSpec((1, LANES), lambda i: (0, 0))],
        out_specs=[pl.BlockSpec((L, w), fwd(0)), pl.BlockSpec((L, w), bwd(0))],
        out_shape=[jax.ShapeDtypeStruct((s, w), F32), jax.ShapeDtypeStruct((s, w), F32)],
        scratch_shapes=[pltpu.VMEM((2 * N_MLSTM_HEADS, MLSTM_DH, MLSTM_AUG), F32),
                        pltpu.VMEM((2 * N_MLSTM_HEADS, LANES), F32)],
        compiler_params=_cparams(("arbitrary",)),
        name="mlstm",
    )(qk, qk, proj, gates, qk, qk, proj, gates, bg)


def _outproj_kernel(attn_ref, hf_ref, hb_ref, om_ref, x_ref, gm_ref, w_ref, gate_ref, g2_ref, sc_ref, sh_ref,
                    x1_ref, h2_ref):
    hsum = hf_ref[...] + hb_ref[...]
    gm = gm_ref[...]
    parts = []
    for h in range(N_MLSTM_HEADS):
        sl = slice(h * MLSTM_DH, (h + 1) * MLSTM_DH)
        seg = hsum[:, sl]
        parts.append(seg * _rms_scale(seg, MLSTM_DH) * gm[:, sl])
    ml = (jax.nn.sigmoid(om_ref[...].astype(F32)) * jnp.concatenate(parts, axis=-1)).astype(BF16)
    n_attn = attn_ref.shape[1]
    mixed = (jnp.dot(attn_ref[...], w_ref[:n_attn, :], preferred_element_type=F32)
             + jnp.dot(ml, w_ref[n_attn:, :], preferred_element_type=F32))
    x1 = x_ref[...] + gate_ref[...] * mixed
    x1_ref[...] = x1
    h2_ref[...] = _modulated_norm(x1, g2_ref[...], sc_ref[...], sh_ref[...]).astype(BF16)


def _outproj(attn, hf, hb, proj, x, gm, w_out, mod, g2, tm=512):
    s, d = x.shape
    wm = hf.shape[1]
    row = lambda blk: pl.BlockSpec((1, d), lambda i: (0, blk))
    return pl.pallas_call(
        _outproj_kernel,
        grid=(s // tm,),
        in_specs=[pl.BlockSpec((tm, attn.shape[1]), lambda i: (i, 0)),
                  pl.BlockSpec((tm, wm), lambda i: (i, 0)),
                  pl.BlockSpec((tm, wm), lambda i: (i, 0)),
                  pl.BlockSpec((tm, wm), lambda i: (i, 3)),
                  pl.BlockSpec((tm, d), lambda i: (i, 0)),
                  pl.BlockSpec((1, wm), lambda i: (0, 0)),
                  pl.BlockSpec(w_out.shape, lambda i: (0, 0), pipeline_mode=pl.Buffered(1)),
                  row(2), row(0), row(4), row(3)],
        out_specs=[pl.BlockSpec((tm, d), lambda i: (i, 0)), pl.BlockSpec((tm, d), lambda i: (i, 0))],
        out_shape=[jax.ShapeDtypeStruct((s, d), F32), jax.ShapeDtypeStruct((s, d), BF16)],
        compiler_params=_cparams(("arbitrary",)),
        name="outproj",
    )(attn, hf, hb, proj, x, gm, w_out, mod, g2, mod, mod)


def _ffn_kernel(h2_ref, w1_ref, w2_ref, x1_ref, gate_ref, o_ref, *, f_last):
    f = pl.program_id(1)

    @pl.when(f == 0)
    def _():
        o_ref[...] = jnp.zeros(o_ref.shape, F32)

    a = jnp.maximum(jnp.dot(h2_ref[...], w1_ref[...], preferred_element_type=F32), 0.0)
    o_ref[...] += jnp.dot((a * a).astype(BF16), w2_ref[...], preferred_element_type=F32)

    @pl.when(f == f_last)
    def _():
        o_ref[...] = x1_ref[...] + gate_ref[...] * o_ref[...]


def _ffn(h2, w1, w2, x1, mod, tm=1024, tf=512):
    s, d = h2.shape
    dff = w1.shape[1]
    return pl.pallas_call(
        functools.partial(_ffn_kernel, f_last=dff // tf - 1),
        grid=(s // tm, dff // tf),
        in_specs=[pl.BlockSpec((tm, d), lambda m, f: (m, 0)),
                  pl.BlockSpec((d, tf), lambda m, f: (0, f)),
                  pl.BlockSpec((tf, d), lambda m, f: (f, 0)),
                  pl.BlockSpec((tm, d), lambda m, f: (m, 0), pipeline_mode=pl.Buffered(1)),
                  pl.BlockSpec((1, d), lambda m, f: (0, 5))],
        out_specs=pl.BlockSpec((tm, d), lambda m, f: (m, 0)),
        out_shape=jax.ShapeDtypeStruct((s, d), F32),
        compiler_params=_cparams(("arbitrary", "arbitrary")),
        name="ffn",
    )(h2, w1, w2, x1, mod)


PACK_TILE = 512
PACK_SRC = 64
PACK_NSRC = PACK_TILE // PACK_SRC
W_IN_MIX_START = 832
W_IN_GATES_START = 4928
N_GATE_COLS = 16


def _pack_src_blocks():
    rows = []
    n_mix = 4 * N_MLSTM_HEADS * MLSTM_DH // PACK_TILE
    for n in range(n_mix):
        first = (W_IN_MIX_START + n * PACK_TILE) // PACK_SRC
        rows.append([first + i for i in range(PACK_NSRC)])
    rows.append(list(range(PACK_NSRC)))
    gates_blk = W_IN_GATES_START // PACK_SRC
    rows.append([8, 9, 10, 11, 12, 12, gates_blk, gates_blk])
    return jnp.asarray(rows, jnp.int32).reshape(-1)


def _pack_kernel(tbl_ref, *refs, n_mix):
    del tbl_ref
    srcs, o_ref = refs[:-1], refs[-1]
    n = pl.program_id(0)
    blk = lambda i: slice(i * PACK_SRC, (i + 1) * PACK_SRC)

    @pl.when(n <= n_mix)
    def _():
        for i, b in enumerate(srcs):
            o_ref[blk(i), :] = b[...].astype(BF16)

    @pl.when(n == n_mix + 1)
    def _():
        zeros = jnp.zeros(srcs[0].shape, BF16)
        for i in range(5):
            o_ref[blk(i), :] = srcs[i][...].astype(BF16)
        o_ref[blk(5), :] = zeros
        rows = lax.broadcasted_iota(jnp.int32, srcs[6].shape, 0)
        o_ref[blk(6), :] = jnp.where(rows < N_GATE_COLS, srcs[6][...], 0.0).astype(BF16)
        o_ref[blk(7), :] = zeros


def _pack_w_in(w_in_t):
    d = w_in_t.shape[1]
    tbl = _pack_src_blocks()
    n_tiles = tbl.shape[0] // PACK_NSRC
    src = lambda i: pl.BlockSpec((PACK_SRC, d), lambda n, t: (t[n * PACK_NSRC + i], 0))
    return pl.pallas_call(
        functools.partial(_pack_kernel, n_mix=n_tiles - 2),
        grid_spec=pltpu.PrefetchScalarGridSpec(
            num_scalar_prefetch=1, grid=(n_tiles,),
            in_specs=[src(i) for i in range(PACK_NSRC)],
            out_specs=pl.BlockSpec((PACK_TILE, d), lambda n, t: (n, 0))),
        out_shape=jax.ShapeDtypeStruct((n_tiles * PACK_TILE, d), BF16),
        compiler_params=_cparams(("arbitrary",)),
        name="packw",
    )(tbl, *([w_in_t] * PACK_NSRC))


def _pad_lanes(v, n):
    return jnp.pad(v, ((0, 0), (0, n - v.shape[1])))


def kernel(x, c, positions, w_ada, b_ada, norm_mix_g, w_in, b_gates, conv_w, conv_b, q_lora_g, w_uq,
           kv_lora_g, w_ukv, q_norm_g, k_norm_g, mlstm_norm_g, w_out, norm_mlp_g, w_ff1, w_ff2):
    bsz, s, d = x.shape
    assert bsz == 1, "kernels are written for a single sequence"
    xs = x[0]
    pos_row = positions.reshape(1, s)
    half = jnp.arange(ROPE_HALF, dtype=F32)
    freq = (ROPE_THETA ** (-half / ROPE_HALF)).reshape(ROPE_HALF, 1)
    row = lambda v: v.reshape(1, -1).astype(F32)

    for l in range(w_ada.shape[0]):
        mod = _mod(c.reshape(d, 1), w_ada[l], row(b_ada[l]))
        proj, gates = _inproj(xs, row(norm_mix_g[l]), mod, _pack_w_in(w_in[l].T))

        wuq_t = jnp.pad(w_uq[l].reshape(-1, N_MLA_HEADS, MLA_QK), ((0, 0), (0, 0), (0, MLA_QK_PAD - MLA_QK)))
        wuq_t = wuq_t.reshape(-1, N_MLA_HEADS * MLA_QK_PAD).T.astype(BF16)
        wkv = w_ukv[l].reshape(-1, N_MLA_HEADS, MLA_NOPE + MLA_V)
        wk = wkv[:, :, :MLA_NOPE].reshape(-1, N_MLA_HEADS * MLA_NOPE).astype(BF16)
        wv_t = wkv[:, :, MLA_NOPE:].reshape(-1, N_MLA_HEADS * MLA_V).T.astype(BF16)
        g_q, g_k = row(q_norm_g[l]), row(k_norm_g[l])
        q_t, k, v_t = _mlaprep(proj, pos_row, freq, row(q_lora_g[l]), row(kv_lora_g[l]), wuq_t, wk, wv_t,
                               g_q.reshape(-1, 1), _pad_lanes(g_k, MLA_QK_PAD))
        logit_bound = 1.02 * LOG2_E * MLA_QK ** 0.5 * jnp.max(jnp.abs(g_q)) * jnp.max(jnp.abs(g_k))
        attn, (w_out_b, w_ff1_b, w_ff2_b) = lax.cond(
            logit_bound <= SAFE_LOG2,
            functools.partial(_attention, shifted=False), functools.partial(_attention, shifted=True),
            q_t, k, v_t, (w_out[l], w_ff1[l], w_ff2[l]))

        qk = _conv(proj, conv_w[l], row(conv_b[l]))
        hf, hb = _mlstm(qk, proj, gates, _pad_lanes(row(b_gates[l]), LANES))

        x1, h2 = _outproj(attn, hf, hb, proj, xs, row(mlstm_norm_g[l]), w_out_b, mod, row(norm_mlp_g[l]))
        xs = _ffn(h2, w_ff1_b, w_ff2_b, x1, mod)
    return xs[None]
```

```python
import functools

import jax
import jax.numpy as jnp
from jax import lax
from jax.experimental import pallas as pl
from jax.experimental.pallas import tpu as pltpu

F32 = jnp.float32
BF16 = jnp.bfloat16

LANES = 128
N_MLA_HEADS = 8
MLA_NOPE = 128
ROPE_DIM = 64
ROPE_HALF = ROPE_DIM // 2
MLA_QK = MLA_NOPE + ROPE_DIM
MLA_QK_PAD = 256
MLA_V = 128
MLA_V_AUG = MLA_V + 16
ATTN_TK = 256
ROPE_THETA = 10000.0
N_MLSTM_HEADS = 4
MLSTM_DH = 256
MLSTM_CHUNK = 256
MLSTM_AUG = MLSTM_DH + LANES
CONV_WIDTH = 5
CONV_HALO = 8
EPS = 1e-6
M_INIT = -1e30
LOG2_E = 1.4426950408889634
SAFE_LOG2 = 60.0
VMEM_LIMIT = 56 * 1024 * 1024


def _cparams(sem):
    return pltpu.CompilerParams(dimension_semantics=sem, vmem_limit_bytes=VMEM_LIMIT)


def _mod_kernel(c_ref, w_ref, b_ref, o_ref, sb_ref, *, tn):
    @pl.when(pl.program_id(0) == 0)
    def _():
        cc = c_ref[...]
        sb_ref[...] = jnp.broadcast_to(cc * jax.nn.sigmoid(cc), sb_ref.shape)

    sb = sb_ref[...]
    for j in range(tn // LANES):
        sl = slice(j * LANES, (j + 1) * LANES)
        o_ref[:, sl] = jnp.sum(w_ref[:, sl] * sb, axis=0, keepdims=True) + b_ref[:, sl]


def _mod(c_col, w_ada, b_ada, tn=1024):
    d, n = w_ada.shape
    return pl.pallas_call(
        functools.partial(_mod_kernel, tn=tn),
        grid=(n // tn,),
        in_specs=[pl.BlockSpec((d, 1), lambda j: (0, 0)),
                  pl.BlockSpec((d, tn), lambda j: (0, j)),
                  pl.BlockSpec((1, tn), lambda j: (0, j))],
        out_specs=pl.BlockSpec((1, tn), lambda j: (0, j)),
        out_shape=jax.ShapeDtypeStruct((1, n), F32),
        scratch_shapes=[pltpu.VMEM((d, LANES), F32)],
        compiler_params=_cparams(("arbitrary",)),
        name="mod",
    )(c_col, w_ada, b_ada)


def _modulated_norm(x, g, scale, shift):
    ms = jnp.mean(x * x, axis=-1, keepdims=True)
    return (x * lax.rsqrt(ms + EPS) * g) * (1.0 + scale) + shift


def _inproj_kernel(x_ref, g_ref, sc_ref, sh_ref, w_ref, o_ref, gate_ref, h_ref, *, n_last):
    n = pl.program_id(1)

    @pl.when(n == 0)
    def _():
        h_ref[...] = _modulated_norm(x_ref[...], g_ref[...], sc_ref[...], sh_ref[...]).astype(BF16)

    acc = lax.dot_general(h_ref[...], w_ref[...], (((1,), (1,)), ((), ())), preferred_element_type=F32)
    o_ref[...] = acc.astype(BF16)

    @pl.when(n == n_last)
    def _():
        gate_ref[...] = acc[:, -LANES:]


def _inproj(x, g, mod, w_t, tm=1024, tn=1024):
    s, d = x.shape
    n_tot = w_t.shape[0]
    return pl.pallas_call(
        functools.partial(_inproj_kernel, n_last=n_tot // tn - 1),
        grid=(s // tm, n_tot // tn),
        in_specs=[pl.BlockSpec((tm, d), lambda m, n: (m, 0)),
                  pl.BlockSpec((1, d), lambda m, n: (0, 0)),
                  pl.BlockSpec((1, d), lambda m, n: (0, 1)),
                  pl.BlockSpec((1, d), lambda m, n: (0, 0)),
                  pl.BlockSpec((tn, d), lambda m, n: (n, 0))],
        out_specs=[pl.BlockSpec((tm, tn), lambda m, n: (m, n)),
                   pl.BlockSpec((tm, LANES), lambda m, n: (m, 0))],
        out_shape=[jax.ShapeDtypeStruct((s, n_tot), BF16),
                   jax.ShapeDtypeStruct((s, LANES), F32)],
        scratch_shapes=[pltpu.VMEM((tm, d), BF16)],
        compiler_params=_cparams(("arbitrary", "arbitrary")),
        name="inproj",
    )(x, g, mod, mod, w_t)


def _rms_scale(x, n):
    return lax.rsqrt(jnp.sum(x * x, axis=-1, keepdims=True) * (1.0 / n) + EPS)


def _mlaprep_kernel(cq_ref, ckv_ref, kpe_ref, pos_ref, freq_ref, gql_ref, gkvl_ref, wuqt_ref, wk_ref, wvt_ref,
                    gqc_ref, gk_ref, q_ref, k_ref, v_ref):
    tm = cq_ref.shape[0]
    nt = (((1,), (1,)), ((), ()))
    cq = cq_ref[...].astype(F32)
    cqn = (cq * _rms_scale(cq, cq.shape[-1]) * gql_ref[...]).astype(BF16)
    ckv = ckv_ref[...].astype(F32)
    ckvn = (ckv * _rms_scale(ckv, ckv.shape[-1]) * gkvl_ref[...]).astype(BF16)
    qf_t = lax.dot_general(wuqt_ref[...], cqn, nt, preferred_element_type=F32)
    kf = jnp.dot(ckvn, wk_ref[...], preferred_element_type=F32)
    vf_t = lax.dot_general(wvt_ref[...], ckvn, nt, preferred_element_type=F32)

    ang_t = freq_ref[...] * pos_ref[...].astype(F32)
    cos_t, sin_t = jnp.cos(ang_t), jnp.sin(ang_t)

    gq_b = jnp.broadcast_to(gqc_ref[...] * (LOG2_E * MLA_QK ** -0.5), (MLA_QK, tm))
    q_pad = jnp.zeros((MLA_QK_PAD - MLA_QK, tm), F32)
    for h in range(N_MLA_HEADS):
        qh = qf_t[h * MLA_QK_PAD:h * MLA_QK_PAD + MLA_QK]
        r = lax.rsqrt(jnp.sum(qh * qh, axis=0, keepdims=True) * (1.0 / MLA_QK) + EPS)
        qn = qh * r * gq_b
        x1, x2 = qn[MLA_NOPE:MLA_NOPE + ROPE_HALF], qn[MLA_NOPE + ROPE_HALF:]
        q_ref[h] = jnp.concatenate([qn[:MLA_NOPE], x1 * cos_t - x2 * sin_t, x1 * sin_t + x2 * cos_t, q_pad],
                                   axis=0).astype(BF16)

    ones_rows = (lax.broadcasted_iota(jnp.int32, (MLA_V_AUG - MLA_V, ATTN_TK), 0) == 0).astype(BF16)
    for h in range(N_MLA_HEADS):
        for c in range(tm // ATTN_TK):
            vc = vf_t[h * MLA_V:(h + 1) * MLA_V, c * ATTN_TK:(c + 1) * ATTN_TK]
            v_ref[h, c] = jnp.concatenate([vc.astype(BF16), ones_rows], axis=0)

    z_half, z_pad = jnp.zeros_like(cos_t), jnp.zeros((LANES - ROPE_DIM, tm), F32)
    c_tab = jnp.concatenate([cos_t, cos_t, z_pad], axis=0).T
    s_up = jnp.concatenate([z_half, sin_t, z_pad], axis=0).T
    s_dn = jnp.concatenate([-sin_t, z_half, z_pad], axis=0).T
    gk = gk_ref[...]
    kpe = kpe_ref[...].astype(F32)
    kpe_g = kpe * gk[:, MLA_NOPE:]
    kpe_rot = (kpe_g * c_tab + pltpu.roll(kpe_g, ROPE_HALF, 1) * s_up
               + pltpu.roll(kpe_g, LANES - ROPE_HALF, 1) * s_dn)
    kpe_ssq = jnp.sum(kpe * kpe, axis=-1, keepdims=True)
    for h in range(N_MLA_HEADS):
        kn = kf[:, h * MLA_NOPE:(h + 1) * MLA_NOPE]
        r = lax.rsqrt((jnp.sum(kn * kn, axis=-1, keepdims=True) + kpe_ssq) * (1.0 / MLA_QK) + EPS)
        k_ref[h] = jnp.concatenate([kn * r * gk[:, :MLA_NOPE], kpe_rot * r], axis=-1).astype(BF16)


def _mlaprep(proj, pos_row, freq, gql, gkvl, wuq_t, wk, wv_t, gq_col, gk, tm=512):
    s = proj.shape[0]
    hq = N_MLA_HEADS
    full = lambda a: pl.BlockSpec(a.shape, lambda i: (0,) * a.ndim)
    return pl.pallas_call(
        _mlaprep_kernel,
        grid=(s // tm,),
        in_specs=[pl.BlockSpec((tm, 512), lambda i: (i, 8)),
                  pl.BlockSpec((tm, 256), lambda i: (i, 18)),
                  pl.BlockSpec((tm, LANES), lambda i: (i, 38)),
                  pl.BlockSpec((1, tm), lambda i: (0, i)),
                  full(freq), full(gql), full(gkvl), full(wuq_t), full(wk), full(wv_t), full(gq_col), full(gk)],
        out_specs=[pl.BlockSpec((hq, MLA_QK_PAD, tm), lambda i: (0, 0, i)),
                   pl.BlockSpec((hq, tm, MLA_QK_PAD), lambda i: (0, i, 0)),
                   pl.BlockSpec((hq, tm // ATTN_TK, MLA_V_AUG, ATTN_TK), lambda i: (0, i, 0, 0))],
        out_shape=[jax.ShapeDtypeStruct((hq, MLA_QK_PAD, s), BF16),
                   jax.ShapeDtypeStruct((hq, s, MLA_QK_PAD), BF16),
                   jax.ShapeDtypeStruct((hq, s // ATTN_TK, MLA_V_AUG, ATTN_TK), BF16)],
        compiler_params=_cparams(("arbitrary",)),
        name="mlaprep",
    )(proj, proj, proj, pos_row, freq, gql, gkvl, wuq_t, wk, wv_t, gq_col, gk)


def _attn_kernel(qt_ref, k_ref, vt_ref, *refs, n_side, shifted, tk, ahead):
    side_in, o_ref, side_out = refs[:n_side], refs[n_side], refs[n_side + 1:]
    for src, dst in zip(side_in, side_out):
        dst[...] = src[...].astype(BF16)

    qt = qt_ref[0]
    tq = qt.shape[1]
    n_chunks = k_ref.shape[1] // tk

    def logits_t(j):
        return jnp.dot(k_ref[0, j * tk:(j + 1) * tk, :], qt, preferred_element_type=F32)

    def values_t(j, p_t):
        return jnp.dot(vt_ref[0, j], p_t, preferred_element_type=F32)

    m = jnp.full((1, tq), -jnp.inf, F32)
    acc = None
    pending = [logits_t(j) for j in range(ahead)]
    for j in range(n_chunks):
        if j + ahead < n_chunks:
            pending.append(logits_t(j + ahead))
        s2 = pending.pop(0)
        if shifted:
            m_new = jnp.maximum(m, jnp.max(s2, axis=0, keepdims=True))
            pv = values_t(j, jnp.exp2(s2 - m_new).astype(BF16))
            acc = pv if acc is None else acc * jnp.exp2(m - m_new) + pv
            m = m_new
        else:
            pv = values_t(j, jnp.exp2(s2).astype(BF16))
            acc = pv if acc is None else acc + pv
    o_ref[...] = (acc[:MLA_V] / acc[MLA_V:MLA_V + 1]).T.astype(BF16)


def _attention(q_t, k, v_t, side_weights, *, shifted, tq=512, tk=ATTN_TK, ahead=2):
    hq, s, _ = k.shape
    n_q = s // tq
    n_steps = hq * n_q
    slab = lambda w: pl.BlockSpec((w.shape[0] // n_steps, w.shape[1]), lambda h, i: (h * n_q + i, 0))
    outs = pl.pallas_call(
        functools.partial(_attn_kernel, n_side=len(side_weights), shifted=shifted, tk=tk, ahead=ahead),
        grid=(hq, n_q),
        in_specs=[pl.BlockSpec((1, MLA_QK_PAD, tq), lambda h, i: (h, 0, i)),
                  pl.BlockSpec((1, s, MLA_QK_PAD), lambda h, i: (h, 0, 0)),
                  pl.BlockSpec((1, s // tk, MLA_V_AUG, tk), lambda h, i: (h, 0, 0, 0))]
                 + [slab(w) for w in side_weights],
        out_specs=[pl.BlockSpec((tq, MLA_V), lambda h, i: (i, h))] + [slab(w) for w in side_weights],
        out_shape=[jax.ShapeDtypeStruct((s, hq * MLA_V), BF16)]
                  + [jax.ShapeDtypeStruct(w.shape, BF16) for w in side_weights],
        compiler_params=_cparams(("arbitrary", "arbitrary")),
        name="attn_shifted" if shifted else "attn",
    )(q_t, k, v_t, *side_weights)
    return outs[0], tuple(outs[1:])


def _conv_kernel(x_ref, w_ref, b_ref, o_ref, pad_ref, *, rows, n_q_tiles):
    s = x_ref.shape[0]
    zeros = jnp.zeros((CONV_HALO, LANES), F32)
    pad_ref[0:CONV_HALO, :] = zeros
    pad_ref[CONV_HALO + s:CONV_HALO + s + CONV_HALO, :] = zeros
    pad_ref[CONV_HALO:CONV_HALO + s, :] = x_ref[...].astype(F32)
    out_scale = jnp.where(pl.program_id(0) >= n_q_tiles, MLSTM_DH ** -0.5, 1.0).astype(F32)
    w = w_ref[...]
    b = b_ref[...]

    def body(r, carry):
        base = pl.multiple_of(r * rows, rows)
        acc = b
        for j in range(CONV_WIDTH):
            off = CONV_HALO + j - CONV_WIDTH // 2
            acc = acc + w[j:j + 1, :] * pad_ref[pl.ds(base + off, rows), :]
        o_ref[pl.ds(base, rows), :] = (acc * jax.nn.sigmoid(acc) * out_scale).astype(BF16)
        return carry

    lax.fori_loop(0, s // rows, body, 0)


def _conv(proj, conv_w, conv_b, rows=256):
    s = proj.shape[0]
    n_ch = conv_w.shape[1]
    return pl.pallas_call(
        functools.partial(_conv_kernel, rows=rows, n_q_tiles=n_ch // 2 // LANES),
        grid=(n_ch // LANES,),
        in_specs=[pl.BlockSpec((s, LANES), lambda j: (0, j)),
                  pl.BlockSpec((CONV_WIDTH, LANES), lambda j: (0, j)),
                  pl.BlockSpec((1, LANES), lambda j: (0, j))],
        out_specs=pl.BlockSpec((s, LANES), lambda j: (0, j)),
        out_shape=jax.ShapeDtypeStruct((s, n_ch), BF16),
        scratch_shapes=[pltpu.VMEM((s + 2 * CONV_HALO, LANES), F32)],
        compiler_params=_cparams(("arbitrary",)),
        name="conv",
    )(proj, conv_w, conv_b)


def _mlstm_chain(q, k, v_aug, g, g_t, lf, b_c, b_ct, mask, ic, fc, c_ref, m_ref, chain):
    b_col, b_row = b_c[:, fc:fc + 1], b_ct[fc:fc + 1, :]
    i_col, i_row = g[:, ic:ic + 1], g_t[ic:ic + 1, :]
    b_tot = jnp.sum(lf[:, fc:fc + 1], axis=0, keepdims=True)
    m_prev = m_ref[chain:chain + 1, 0:1]

    log_d = jnp.where(mask, b_col - b_row + i_row, -jnp.inf)
    log_inter = b_col + m_prev
    m_t = jnp.maximum(log_inter, jnp.max(log_d, axis=-1, keepdims=True))
    d_m = jnp.exp(log_d - m_t)
    w_inter = jnp.exp(log_inter - m_t)

    c_aug = c_ref[chain]
    scores = lax.dot_general(q, k, (((1,), (1,)), ((), ())), preferred_element_type=F32) * d_m
    r = (jnp.dot(scores.astype(BF16), v_aug, preferred_element_type=F32)
         + w_inter * jnp.dot(q, c_aug.astype(BF16), preferred_element_type=F32))
    num, den = r[:, :MLSTM_DH], r[:, MLSTM_DH:MLSTM_DH + 1]
    h = num / jnp.maximum(jnp.abs(den), jnp.exp(-m_t))

    log_w = b_tot - b_col + i_col
    m_new = jnp.maximum(b_tot + m_prev, jnp.max(log_w, axis=0, keepdims=True))
    decay = jnp.exp(b_tot + m_prev - m_new)
    kw = (k.astype(F32) * jnp.exp(log_w - m_new)).astype(BF16)
    upd = lax.dot_general(kw, v_aug, (((0,), (0,)), ((), ())), preferred_element_type=F32)
    c_ref[chain] = decay * c_aug + upd
    m_ref[chain:chain + 1, :] = jnp.broadcast_to(m_new, (1, LANES))
    return h


def _mlstm_kernel(qf_ref, kf_ref, vf_ref, gf_ref, qb_ref, kb_ref, vb_ref, gb_ref, bg_ref,
                  hf_ref, hb_ref, c_ref, m_ref):
    @pl.when(pl.program_id(0) == 0)
    def _():
        c_ref[...] = jnp.zeros(c_ref.shape, F32)
        m_ref[...] = jnp.full(m_ref.shape, M_INIT, F32)

    L = MLSTM_CHUNK
    row = lax.broadcasted_iota(jnp.int32, (L, L), 0)
    col = lax.broadcasted_iota(jnp.int32, (L, L), 1)
    ones_col = (lax.broadcasted_iota(jnp.int32, (L, LANES), 1) == 0).astype(BF16)
    hm = N_MLSTM_HEADS
    for d, (q_ref, k_ref, v_ref, g_ref, o_ref) in enumerate(
            ((qf_ref, kf_ref, vf_ref, gf_ref, hf_ref), (qb_ref, kb_ref, vb_ref, gb_ref, hb_ref))):
        mask = (row >= col) if d == 0 else (row <= col)
        g = g_ref[...] + bg_ref[...]
        lf = jax.nn.log_sigmoid(g)
        b_c = jnp.dot(mask.astype(F32), lf, preferred_element_type=F32,
                      precision=lax.Precision.HIGHEST)
        g_t, b_ct = g.T, b_c.T
        for h in range(hm):
            sl = slice(h * MLSTM_DH, (h + 1) * MLSTM_DH)
            v_aug = jnp.concatenate([v_ref[:, sl], ones_col], axis=-1)
            o_ref[:, sl] = _mlstm_chain(q_ref[:, sl], k_ref[:, sl], v_aug, g, g_t, lf, b_c, b_ct, mask,
                                        2 * d * hm + h, (2 * d + 1) * hm + h, c_ref, m_ref,
                                        d * hm + h).astype(o_ref.dtype)


def _mlstm(qk, proj, gates, bg):
    s = qk.shape[0]
    L = MLSTM_CHUNK
    nc = s // L
    w = N_MLSTM_HEADS * MLSTM_DH
    fwd = lambda blk: (lambda i: (i, blk))
    bwd = lambda blk: (lambda i: (nc - 1 - i, blk))
    specs = lambda ix: [pl.BlockSpec((L, w), ix(0)), pl.BlockSpec((L, w), ix(1)),
                        pl.BlockSpec((L, w), ix(2)), pl.BlockSpec((L, LANES), ix(0))]
    return pl.pallas_call(
        _mlstm_kernel,
        grid=(nc,),
        in_specs=specs(fwd) + specs(bwd) + [pl.BlockSpec((1, LANES), lambda i: (0, 0))],
        out_specs=[pl.BlockSpec((L, w), fwd(0)), pl.BlockSpec((L, w), bwd(0))],
        out_shape=[jax.ShapeDtypeStruct((s, w), BF16), jax.ShapeDtypeStruct((s, w), BF16)],
        scratch_shapes=[pltpu.VMEM((2 * N_MLSTM_HEADS, MLSTM_DH, MLSTM_AUG), F32),
                        pltpu.VMEM((2 * N_MLSTM_HEADS, LANES), F32)],
        compiler_params=_cparams(("arbitrary",)),
        name="mlstm",
    )(qk, qk, proj, gates, qk, qk, proj, gates, bg)


def _outproj_kernel(attn_ref, hf_ref, hb_ref, om_ref, x_ref, gm_ref, w_ref, gate_ref, g2_ref, sc_ref, sh_ref,
                    x1_ref, h2_ref, *, n_sub):
    gm = gm_ref[...]
    n_attn = attn_ref.shape[1]
    sub = x_ref.shape[0] // n_sub
    for t in range(n_sub):
        rows = slice(t * sub, (t + 1) * sub)
        hsum = hf_ref[rows, :].astype(F32) + hb_ref[rows, :].astype(F32)
        parts = []
        for h in range(N_MLSTM_HEADS):
            sl = slice(h * MLSTM_DH, (h + 1) * MLSTM_DH)
            seg = hsum[:, sl]
            parts.append(seg * _rms_scale(seg, MLSTM_DH) * gm[:, sl])
        ml = (jax.nn.sigmoid(om_ref[rows, :].astype(F32)) * jnp.concatenate(parts, axis=-1)).astype(BF16)
        mixed = (jnp.dot(attn_ref[rows, :], w_ref[:n_attn, :], preferred_element_type=F32)
                 + jnp.dot(ml, w_ref[n_attn:, :], preferred_element_type=F32))
        x1 = x_ref[rows, :] + gate_ref[...] * mixed
        x1_ref[rows, :] = x1
        h2_ref[rows, :] = _modulated_norm(x1, g2_ref[...], sc_ref[...], sh_ref[...]).astype(BF16)


def _outproj(attn, hf, hb, proj, x, gm, w_out, mod, g2, tm=512, n_sub=1):
    s, d = x.shape
    wm = hf.shape[1]
    row = lambda blk: pl.BlockSpec((1, d), lambda i: (0, blk))
    return pl.pallas_call(
        functools.partial(_outproj_kernel, n_sub=n_sub),
        grid=(s // tm,),
        in_specs=[pl.BlockSpec((tm, attn.shape[1]), lambda i: (i, 0)),
                  pl.BlockSpec((tm, wm), lambda i: (i, 0)),
                  pl.BlockSpec((tm, wm), lambda i: (i, 0)),
                  pl.BlockSpec((tm, wm), lambda i: (i, 3)),
                  pl.BlockSpec((tm, d), lambda i: (i, 0)),
                  pl.BlockSpec((1, wm), lambda i: (0, 0)),
                  pl.BlockSpec(w_out.shape, lambda i: (0, 0), pipeline_mode=pl.Buffered(1)),
                  row(2), row(0), row(4), row(3)],
        out_specs=[pl.BlockSpec((tm, d), lambda i: (i, 0)), pl.BlockSpec((tm, d), lambda i: (i, 0))],
        out_shape=[jax.ShapeDtypeStruct((s, d), F32), jax.ShapeDtypeStruct((s, d), BF16)],
        compiler_params=_cparams(("arbitrary",)),
        name="outproj",
    )(attn, hf, hb, proj, x, gm, w_out, mod, g2, mod, mod)


def _ffn_kernel(h2_ref, w1_ref, w2_ref, x1_hbm, gate_ref, o_ref, x1_buf, x1_sem, *, f_last):
    m, f = pl.program_id(0), pl.program_id(1)
    tm = o_ref.shape[0]
    x1_copy = pltpu.make_async_copy(x1_hbm.at[pl.ds(pl.multiple_of(m * tm, tm), tm), :], x1_buf, x1_sem)

    @pl.when(f == 0)
    def _():
        x1_copy.start()
        o_ref[...] = jnp.zeros(o_ref.shape, F32)

    a = jnp.maximum(jnp.dot(h2_ref[...], w1_ref[...], preferred_element_type=F32), 0.0)
    o_ref[...] += jnp.dot((a * a).astype(BF16), w2_ref[...], preferred_element_type=F32)

    @pl.when(f == f_last)
    def _():
        x1_copy.wait()
        o_ref[...] = x1_buf[...] + gate_ref[...] * o_ref[...]


def _ffn(h2, w1, w2, x1, mod, tm=1024, tf=512):
    s, d = h2.shape
    dff = w1.shape[1]
    return pl.pallas_call(
        functools.partial(_ffn_kernel, f_last=dff // tf - 1),
        grid=(s // tm, dff // tf),
        in_specs=[pl.BlockSpec((tm, d), lambda m, f: (m, 0)),
                  pl.BlockSpec((d, tf), lambda m, f: (0, f)),
                  pl.BlockSpec((tf, d), lambda m, f: (f, 0)),
                  pl.BlockSpec(memory_space=pl.ANY),
                  pl.BlockSpec((1, d), lambda m, f: (0, 5))],
        out_specs=pl.BlockSpec((tm, d), lambda m, f: (m, 0)),
        out_shape=jax.ShapeDtypeStruct((s, d), F32),
        scratch_shapes=[pltpu.VMEM((tm, d), F32), pltpu.SemaphoreType.DMA(())],
        compiler_params=_cparams(("arbitrary", "arbitrary")),
        name="ffn",
    )(h2, w1, w2, x1, mod)


PACK_TILE = 512
PACK_SRC = 64
PACK_NSRC = PACK_TILE // PACK_SRC
W_IN_MIX_START = 832
W_IN_GATES_START = 4928
N_GATE_COLS = 16


def _pack_src_blocks():
    rows = []
    n_mix = 4 * N_MLSTM_HEADS * MLSTM_DH // PACK_TILE
    for n in range(n_mix):
        first = (W_IN_MIX_START + n * PACK_TILE) // PACK_SRC
        rows.append([first + i for i in range(PACK_NSRC)])
    rows.append(list(range(PACK_NSRC)))
    gates_blk = W_IN_GATES_START // PACK_SRC
    rows.append([8, 9, 10, 11, 12, 12, gates_blk, gates_blk])
    return jnp.asarray(rows, jnp.int32).reshape(-1)


def _pack_kernel(tbl_ref, *refs, n_mix):
    del tbl_ref
    srcs, o_ref = refs[:-1], refs[-1]
    n = pl.program_id(0)
    blk = lambda i: slice(i * PACK_SRC, (i + 1) * PACK_SRC)

    @pl.when(n <= n_mix)
    def _():
        for i, b in enumerate(srcs):
            o_ref[blk(i), :] = b[...].astype(BF16)

    @pl.when(n == n_mix + 1)
    def _():
        zeros = jnp.zeros(srcs[0].shape, BF16)
        for i in range(5):
            o_ref[blk(i), :] = srcs[i][...].astype(BF16)
        o_ref[blk(5), :] = zeros
        rows = lax.broadcasted_iota(jnp.int32, srcs[6].shape, 0)
        o_ref[blk(6), :] = jnp.where(rows < N_GATE_COLS, srcs[6][...], 0.0).astype(BF16)
        o_ref[blk(7), :] = zeros


def _pack_w_in(w_in_t):
    d = w_in_t.shape[1]
    tbl = _pack_src_blocks()
    n_tiles = tbl.shape[0] // PACK_NSRC
    src = lambda i: pl.BlockSpec((PACK_SRC, d), lambda n, t: (t[n * PACK_NSRC + i], 0))
    return pl.pallas_call(
        functools.partial(_pack_kernel, n_mix=n_tiles - 2),
        grid_spec=pltpu.PrefetchScalarGridSpec(
            num_scalar_prefetch=1, grid=(n_tiles,),
            in_specs=[src(i) for i in range(PACK_NSRC)],
            out_specs=pl.BlockSpec((PACK_TILE, d), lambda n, t: (n, 0))),
        out_shape=jax.ShapeDtypeStruct((n_tiles * PACK_TILE, d), BF16),
        compiler_params=_cparams(("arbitrary",)),
        name="packw",
    )(tbl, *([w_in_t] * PACK_NSRC))


def _pad_lanes(v, n):
    return jnp.pad(v, ((0, 0), (0, n - v.shape[1])))


def kernel(x, c, positions, w_ada, b_ada, norm_mix_g, w_in, b_gates, conv_w, conv_b, q_lora_g, w_uq,
           kv_lora_g, w_ukv, q_norm_g, k_norm_g, mlstm_norm_g, w_out, norm_mlp_g, w_ff1, w_ff2):
    bsz, s, d = x.shape
    assert bsz == 1, "kernels are written for a single sequence"
    xs = x[0]
    pos_row = positions.reshape(1, s)
    half = jnp.arange(ROPE_HALF, dtype=F32)
    freq = (ROPE_THETA ** (-half / ROPE_HALF)).reshape(ROPE_HALF, 1)
    row = lambda v: v.reshape(1, -1).astype(F32)

    for l in range(w_ada.shape[0]):
        mod = _mod(c.reshape(d, 1), w_ada[l], row(b_ada[l]))
        proj, gates = _inproj(xs, row(norm_mix_g[l]), mod, _pack_w_in(w_in[l].T))

        wuq_t = jnp.pad(w_uq[l].reshape(-1, N_MLA_HEADS, MLA_QK), ((0, 0), (0, 0), (0, MLA_QK_PAD - MLA_QK)))
        wuq_t = wuq_t.reshape(-1, N_MLA_HEADS * MLA_QK_PAD).T.astype(BF16)
        wkv = w_ukv[l].reshape(-1, N_MLA_HEADS, MLA_NOPE + MLA_V)
        wk = wkv[:, :, :MLA_NOPE].reshape(-1, N_MLA_HEADS * MLA_NOPE).astype(BF16)
        wv_t = wkv[:, :, MLA_NOPE:].reshape(-1, N_MLA_HEADS * MLA_V).T.astype(BF16)
        g_q, g_k = row(q_norm_g[l]), row(k_norm_g[l])
        q_t, k, v_t = _mlaprep(proj, pos_row, freq, row(q_lora_g[l]), row(kv_lora_g[l]), wuq_t, wk, wv_t,
                               g_q.reshape(-1, 1), _pad_lanes(g_k, MLA_QK_PAD))
        logit_bound = 1.02 * LOG2_E * MLA_QK ** 0.5 * jnp.max(jnp.abs(g_q)) * jnp.max(jnp.abs(g_k))
        attn, (w_out_b, w_ff1_b, w_ff2_b) = lax.cond(
            logit_bound <= SAFE_LOG2,
            functools.partial(_attention, shifted=False), functools.partial(_attention, shifted=True),
            q_t, k, v_t, (w_out[l], w_ff1[l], w_ff2[l]))

        qk = _conv(proj, conv_w[l], row(conv_b[l]))
        hf, hb = _mlstm(qk, proj, gates, _pad_lanes(row(b_gates[l]), LANES))

        x1, h2 = _outproj(attn, hf, hb, proj, xs, row(mlstm_norm_g[l]), w_out_b, mod, row(norm_mlp_g[l]))
        xs = _ffn(h2, w_ff1_b, w_ff2_b, x1, mod)
    return xs[None]
```

```python
import functools

import jax
import jax.numpy as jnp
from jax import lax
from jax.experimental import pallas as pl
from jax.experimental.pallas import tpu as pltpu

F32 = jnp.float32
BF16 = jnp.bfloat16

LANES = 128
N_MLA_HEADS = 8
MLA_NOPE = 128
ROPE_DIM = 64
ROPE_HALF = ROPE_DIM // 2
MLA_QK = MLA_NOPE + ROPE_DIM
MLA_QK_PAD = 256
MLA_V = 128
ATTN_TK = 256
NORM_ROWS = 128
ROPE_THETA = 10000.0
N_MLSTM_HEADS = 4
MLSTM_DH = 256
MLSTM_CHUNK = 256
MLSTM_AUG = MLSTM_DH + LANES
CONV_WIDTH = 5
CONV_HALO = 8
EPS = 1e-6
M_INIT = -1e30
LOG2_E = 1.4426950408889634
SAFE_LOG2 = 60.0
VMEM_LIMIT = 56 * 1024 * 1024


def _cparams(sem):
    return pltpu.CompilerParams(dimension_semantics=sem, vmem_limit_bytes=VMEM_LIMIT)


def _mod_kernel(c_ref, w_ref, b_ref, o_ref, sb_ref, *, tn):
    @pl.when(pl.program_id(0) == 0)
    def _():
        cc = c_ref[...]
        sb_ref[...] = jnp.broadcast_to(cc * jax.nn.sigmoid(cc), sb_ref.shape)

    sb = sb_ref[...]
    for j in range(tn // LANES):
        sl = slice(j * LANES, (j + 1) * LANES)
        o_ref[:, sl] = jnp.sum(w_ref[:, sl] * sb, axis=0, keepdims=True) + b_ref[:, sl]


def _mod(c_col, w_ada, b_ada, tn=1024):
    d, n = w_ada.shape
    return pl.pallas_call(
        functools.partial(_mod_kernel, tn=tn),
        grid=(n // tn,),
        in_specs=[pl.BlockSpec((d, 1), lambda j: (0, 0)),
                  pl.BlockSpec((d, tn), lambda j: (0, j)),
                  pl.BlockSpec((1, tn), lambda j: (0, j))],
        out_specs=pl.BlockSpec((1, tn), lambda j: (0, j)),
        out_shape=jax.ShapeDtypeStruct((1, n), F32),
        scratch_shapes=[pltpu.VMEM((d, LANES), F32)],
        compiler_params=_cparams(("arbitrary",)),
        name="mod",
    )(c_col, w_ada, b_ada)


def _modulated_norm(x, g, scale, shift):
    ms = jnp.mean(x * x, axis=-1, keepdims=True)
    return (x * lax.rsqrt(ms + EPS) * g) * (1.0 + scale) + shift


def _inproj_kernel(x_ref, g_ref, sc_ref, sh_ref, w_ref, o_ref, gate_ref, h_ref, *, n_last):
    n = pl.program_id(1)

    @pl.when(n == 0)
    def _():
        def body(r, carry):
            rows = pl.ds(pl.multiple_of(r * NORM_ROWS, NORM_ROWS), NORM_ROWS)
            h_ref[rows, :] = _modulated_norm(x_ref[rows, :], g_ref[...], sc_ref[...], sh_ref[...]).astype(BF16)
            return carry
        lax.fori_loop(0, x_ref.shape[0] // NORM_ROWS, body, 0)

    acc = lax.dot_general(h_ref[...], w_ref[...], (((1,), (1,)), ((), ())), preferred_element_type=F32)
    o_ref[...] = acc.astype(BF16)

    @pl.when(n == n_last)
    def _():
        gate_ref[...] = acc[:, -LANES:]


def _inproj(x, g, mod, w_t, tm=1024, tn=1024):
    s, d = x.shape
    n_tot = w_t.shape[0]
    return pl.pallas_call(
        functools.partial(_inproj_kernel, n_last=n_tot // tn - 1),
        grid=(s // tm, n_tot // tn),
        in_specs=[pl.BlockSpec((tm, d), lambda m, n: (m, 0)),
                  pl.BlockSpec((1, d), lambda m, n: (0, 0)),
                  pl.BlockSpec((1, d), lambda m, n: (0, 1)),
                  pl.BlockSpec((1, d), lambda m, n: (0, 0)),
                  pl.BlockSpec((tn, d), lambda m, n: (n, 0))],
        out_specs=[pl.BlockSpec((tm, tn), lambda m, n: (m, n)),
                   pl.BlockSpec((tm, LANES), lambda m, n: (m, 0))],
        out_shape=[jax.ShapeDtypeStruct((s, n_tot), BF16),
                   jax.ShapeDtypeStruct((s, LANES), F32)],
        scratch_shapes=[pltpu.VMEM((tm, d), BF16)],
        compiler_params=_cparams(("arbitrary", "arbitrary")),
        name="inproj",
    )(x, g, mod, mod, w_t)


def _rms_scale(x, n):
    return lax.rsqrt(jnp.sum(x * x, axis=-1, keepdims=True) * (1.0 / n) + EPS)


def _mlaprep_kernel(cq_ref, ckv_ref, kpe_ref, pos_ref, freq_ref, gql_ref, gkvl_ref, wuqt_ref, wk_ref, wvt_ref,
                    gqc_ref, gk_ref, q_ref, k_ref, v_ref):
    tm = cq_ref.shape[0]
    nt = (((1,), (1,)), ((), ()))
    cq = cq_ref[...].astype(F32)
    cqn = (cq * _rms_scale(cq, cq.shape[-1]) * gql_ref[...]).astype(BF16)
    ckv = ckv_ref[...].astype(F32)
    ckvn = (ckv * _rms_scale(ckv, ckv.shape[-1]) * gkvl_ref[...]).astype(BF16)
    qf_t = lax.dot_general(wuqt_ref[...], cqn, nt, preferred_element_type=F32)
    kf = jnp.dot(ckvn, wk_ref[...], preferred_element_type=F32)
    vf_t = lax.dot_general(wvt_ref[...], ckvn, nt, preferred_element_type=F32)

    ang_t = freq_ref[...] * pos_ref[...].astype(F32)
    cos_t, sin_t = jnp.cos(ang_t), jnp.sin(ang_t)

    gq_b = jnp.broadcast_to(gqc_ref[...] * (LOG2_E * MLA_QK ** -0.5), (MLA_QK, tm))
    q_pad = jnp.zeros((MLA_QK_PAD - MLA_QK, tm), F32)
    for h in range(N_MLA_HEADS):
        qh = qf_t[h * MLA_QK_PAD:h * MLA_QK_PAD + MLA_QK]
        r = lax.rsqrt(jnp.sum(qh * qh, axis=0, keepdims=True) * (1.0 / MLA_QK) + EPS)
        qn = qh * r * gq_b
        x1, x2 = qn[MLA_NOPE:MLA_NOPE + ROPE_HALF], qn[MLA_NOPE + ROPE_HALF:]
        q_ref[h] = jnp.concatenate([qn[:MLA_NOPE], x1 * cos_t - x2 * sin_t, x1 * sin_t + x2 * cos_t, q_pad],
                                   axis=0).astype(BF16)

    for h in range(N_MLA_HEADS):
        for c in range(tm // ATTN_TK):
            v_ref[h, c] = vf_t[h * MLA_V:(h + 1) * MLA_V, c * ATTN_TK:(c + 1) * ATTN_TK].astype(BF16)

    z_half, z_pad = jnp.zeros_like(cos_t), jnp.zeros((LANES - ROPE_DIM, tm), F32)
    c_tab = jnp.concatenate([cos_t, cos_t, z_pad], axis=0).T
    s_up = jnp.concatenate([z_half, sin_t, z_pad], axis=0).T
    s_dn = jnp.concatenate([-sin_t, z_half, z_pad], axis=0).T
    gk = gk_ref[...]
    kpe = kpe_ref[...].astype(F32)
    kpe_g = kpe * gk[:, MLA_NOPE:]
    kpe_rot = (kpe_g * c_tab + pltpu.roll(kpe_g, ROPE_HALF, 1) * s_up
               + pltpu.roll(kpe_g, LANES - ROPE_HALF, 1) * s_dn)
    kpe_ssq = jnp.sum(kpe * kpe, axis=-1, keepdims=True)
    for h in range(N_MLA_HEADS):
        kn = kf[:, h * MLA_NOPE:(h + 1) * MLA_NOPE]
        r = lax.rsqrt((jnp.sum(kn * kn, axis=-1, keepdims=True) + kpe_ssq) * (1.0 / MLA_QK) + EPS)
        k_ref[h] = jnp.concatenate([kn * r * gk[:, :MLA_NOPE], kpe_rot * r], axis=-1).astype(BF16)


def _mlaprep(proj, pos_row, freq, gql, gkvl, wuq_t, wk, wv_t, gq_col, gk, tm=512):
    s = proj.shape[0]
    hq = N_MLA_HEADS
    full = lambda a: pl.BlockSpec(a.shape, lambda i: (0,) * a.ndim)
    return pl.pallas_call(
        _mlaprep_kernel,
        grid=(s // tm,),
        in_specs=[pl.BlockSpec((tm, 512), lambda i: (i, 8)),
                  pl.BlockSpec((tm, 256), lambda i: (i, 18)),
                  pl.BlockSpec((tm, LANES), lambda i: (i, 38)),
                  pl.BlockSpec((1, tm), lambda i: (0, i)),
                  full(freq), full(gql), full(gkvl), full(wuq_t), full(wk), full(wv_t), full(gq_col), full(gk)],
        out_specs=[pl.BlockSpec((hq, MLA_QK_PAD, tm), lambda i: (0, 0, i)),
                   pl.BlockSpec((hq, tm, MLA_QK_PAD), lambda i: (0, i, 0)),
                   pl.BlockSpec((hq, tm // ATTN_TK, MLA_V, ATTN_TK), lambda i: (0, i, 0, 0))],
        out_shape=[jax.ShapeDtypeStruct((hq, MLA_QK_PAD, s), BF16),
                   jax.ShapeDtypeStruct((hq, s, MLA_QK_PAD), BF16),
                   jax.ShapeDtypeStruct((hq, s // ATTN_TK, MLA_V, ATTN_TK), BF16)],
        compiler_params=_cparams(("arbitrary",)),
        name="mlaprep",
    )(proj, proj, proj, pos_row, freq, gql, gkvl, wuq_t, wk, wv_t, gq_col, gk)


def _attn_kernel(qt_ref, k_ref, vt_ref, *refs, n_side, shifted, tk, ahead):
    side_in, o_ref, side_out = refs[:n_side], refs[n_side], refs[n_side + 1:]
    for src, dst in zip(side_in, side_out):
        dst[...] = src[...].astype(BF16)

    qt = qt_ref[0]
    tq = qt.shape[1]
    n_chunks = k_ref.shape[1] // tk

    def logits_t(j):
        return jnp.dot(k_ref[0, j * tk:(j + 1) * tk, :], qt, preferred_element_type=F32)

    def values_t(j, p_t):
        return jnp.dot(vt_ref[0, j], p_t.astype(BF16), preferred_element_type=F32)

    m = jnp.full((1, tq), -jnp.inf, F32)
    acc = l = None
    pending = [logits_t(j) for j in range(ahead)]
    for j in range(n_chunks):
        if j + ahead < n_chunks:
            pending.append(logits_t(j + ahead))
        s2 = pending.pop(0)
        if shifted:
            m_new = jnp.maximum(m, jnp.max(s2, axis=0, keepdims=True))
            p_t, alpha = jnp.exp2(s2 - m_new), jnp.exp2(m - m_new)
            pv, ps = values_t(j, p_t), jnp.sum(p_t, axis=0, keepdims=True)
            acc, l = (pv, ps) if acc is None else (acc * alpha + pv, l * alpha + ps)
            m = m_new
        else:
            p_t = jnp.exp2(s2)
            pv, ps = values_t(j, p_t), jnp.sum(p_t, axis=0, keepdims=True)
            acc, l = (pv, ps) if acc is None else (acc + pv, l + ps)
    o_ref[...] = (acc / l).T.astype(BF16)


def _attention(q_t, k, v_t, side_weights, *, shifted, tq=512, tk=ATTN_TK, ahead=2):
    hq, s, _ = k.shape
    n_q = s // tq
    n_steps = hq * n_q
    slab = lambda w: pl.BlockSpec((w.shape[0] // n_steps, w.shape[1]), lambda h, i: (h * n_q + i, 0))
    outs = pl.pallas_call(
        functools.partial(_attn_kernel, n_side=len(side_weights), shifted=shifted, tk=tk, ahead=ahead),
        grid=(hq, n_q),
        in_specs=[pl.BlockSpec((1, MLA_QK_PAD, tq), lambda h, i: (h, 0, i)),
                  pl.BlockSpec((1, s, MLA_QK_PAD), lambda h, i: (h, 0, 0)),
                  pl.BlockSpec((1, s // tk, MLA_V, tk), lambda h, i: (h, 0, 0, 0))]
                 + [slab(w) for w in side_weights],
        out_specs=[pl.BlockSpec((tq, MLA_V), lambda h, i: (i, h))] + [slab(w) for w in side_weights],
        out_shape=[jax.ShapeDtypeStruct((s, hq * MLA_V), BF16)]
                  + [jax.ShapeDtypeStruct(w.shape, BF16) for w in side_weights],
        compiler_params=_cparams(("arbitrary", "arbitrary")),
        name="attn_shifted" if shifted else "attn",
    )(q_t, k, v_t, *side_weights)
    return outs[0], tuple(outs[1:])


def _conv_kernel(x_ref, w_ref, b_ref, o_ref, pad_ref, *, rows, n_q_tiles):
    s = x_ref.shape[0]
    zeros = jnp.zeros((CONV_HALO, LANES), F32)
    pad_ref[0:CONV_HALO, :] = zeros
    pad_ref[CONV_HALO + s:CONV_HALO + s + CONV_HALO, :] = zeros
    pad_ref[CONV_HALO:CONV_HALO + s, :] = x_ref[...].astype(F32)
    out_scale = jnp.where(pl.program_id(0) >= n_q_tiles, MLSTM_DH ** -0.5, 1.0).astype(F32)
    w = w_ref[...]
    b = b_ref[...]

    def body(r, carry):
        base = pl.multiple_of(r * rows, rows)
        acc = b
        for j in range(CONV_WIDTH):
            off = CONV_HALO + j - CONV_WIDTH // 2
            acc = acc + w[j:j + 1, :] * pad_ref[pl.ds(base + off, rows), :]
        o_ref[pl.ds(base, rows), :] = (acc * jax.nn.sigmoid(acc) * out_scale).astype(BF16)
        return carry

    lax.fori_loop(0, s // rows, body, 0)


def _conv(proj, conv_w, conv_b, rows=256):
    s = proj.shape[0]
    n_ch = conv_w.shape[1]
    return pl.pallas_call(
        functools.partial(_conv_kernel, rows=rows, n_q_tiles=n_ch // 2 // LANES),
        grid=(n_ch // LANES,),
        in_specs=[pl.BlockSpec((s, LANES), lambda j: (0, j)),
                  pl.BlockSpec((CONV_WIDTH, LANES), lambda j: (0, j)),
                  pl.BlockSpec((1, LANES), lambda j: (0, j))],
        out_specs=pl.BlockSpec((s, LANES), lambda j: (0, j)),
        out_shape=jax.ShapeDtypeStruct((s, n_ch), BF16),
        scratch_shapes=[pltpu.VMEM((s + 2 * CONV_HALO, LANES), F32)],
        compiler_params=_cparams(("arbitrary",)),
        name="conv",
    )(proj, conv_w, conv_b)


def _mlstm_chain(q, k, v_aug, g, g_t, lf, b_c, b_ct, mask, ic, fc, c_ref, m_ref, chain):
    b_col, b_row = b_c[:, fc:fc + 1], b_ct[fc:fc + 1, :]
    i_col, i_row = g[:, ic:ic + 1], g_t[ic:ic + 1, :]
    b_tot = jnp.sum(lf[:, fc:fc + 1], axis=0, keepdims=True)
    m_prev = m_ref[chain:chain + 1, 0:1]

    log_d = jnp.where(mask, b_col - b_row + i_row, -jnp.inf)
    log_inter = b_col + m_prev
    m_t = jnp.maximum(log_inter, jnp.max(log_d, axis=-1, keepdims=True))
    d_m = jnp.exp(log_d - m_t)
    w_inter = jnp.exp(log_inter - m_t)

    c_aug = c_ref[chain]
    scores = lax.dot_general(q, k, (((1,), (1,)), ((), ())), preferred_element_type=F32) * d_m
    r = (jnp.dot(scores.astype(BF16), v_aug, preferred_element_type=F32)
         + w_inter * jnp.dot(q, c_aug.astype(BF16), preferred_element_type=F32))
    num, den = r[:, :MLSTM_DH], r[:, MLSTM_DH:MLSTM_DH + 1]
    h = num / jnp.maximum(jnp.abs(den), jnp.exp(-m_t))

    log_w = b_tot - b_col + i_col
    m_new = jnp.maximum(b_tot + m_prev, jnp.max(log_w, axis=0, keepdims=True))
    decay = jnp.exp(b_tot + m_prev - m_new)
    kw = (k.astype(F32) * jnp.exp(log_w - m_new)).astype(BF16)
    upd = lax.dot_general(kw, v_aug, (((0,), (0,)), ((), ())), preferred_element_type=F32)
    c_ref[chain] = decay * c_aug + upd
    m_ref[chain:chain + 1, :] = jnp.broadcast_to(m_new, (1, LANES))
    return h


def _mlstm_kernel(qf_ref, kf_ref, vf_ref, gf_ref, qb_ref, kb_ref, vb_ref, gb_ref, bg_ref,
                  hf_ref, hb_ref, c_ref, m_ref):
    @pl.when(pl.program_id(0) == 0)
    def _():
        c_ref[...] = jnp.zeros(c_ref.shape, F32)
        m_ref[...] = jnp.full(m_ref.shape, M_INIT, F32)

    L = MLSTM_CHUNK
    row = lax.broadcasted_iota(jnp.int32, (L, L), 0)
    col = lax.broadcasted_iota(jnp.int32, (L, L), 1)
    ones_col = (lax.broadcasted_iota(jnp.int32, (L, LANES), 1) == 0).astype(BF16)
    hm = N_MLSTM_HEADS
    for d, (q_ref, k_ref, v_ref, g_ref, o_ref) in enumerate(
            ((qf_ref, kf_ref, vf_ref, gf_ref, hf_ref), (qb_ref, kb_ref, vb_ref, gb_ref, hb_ref))):
        mask = (row >= col) if d == 0 else (row <= col)
        g = g_ref[...] + bg_ref[...]
        lf = jax.nn.log_sigmoid(g)
        b_c = jnp.dot(mask.astype(F32), lf, preferred_element_type=F32,
                      precision=lax.Precision.HIGHEST)
        g_t, b_ct = g.T, b_c.T
        for h in range(hm):
            sl = slice(h * MLSTM_DH, (h + 1) * MLSTM_DH)
            v_aug = jnp.concatenate([v_ref[:, sl], ones_col], axis=-1)
            o_ref[:, sl] = _mlstm_chain(q_ref[:, sl], k_ref[:, sl], v_aug, g, g_t, lf, b_c, b_ct, mask,
                                        2 * d * hm + h, (2 * d + 1) * hm + h, c_ref, m_ref,
                                        d * hm + h).astype(o_ref.dtype)


def _mlstm(qk, proj, gates, bg):
    s = qk.shape[0]
    L = MLSTM_CHUNK
    nc = s // L
    w = N_MLSTM_HEADS * MLSTM_DH
    fwd = lambda blk: (lambda i: (i, blk))
    bwd = lambda blk: (lambda i: (nc - 1 - i, blk))
    specs = lambda ix: [pl.BlockSpec((L, w), ix(0)), pl.BlockSpec((L, w), ix(1)),
                        pl.BlockSpec((L, w), ix(2)), pl.BlockSpec((L, LANES), ix(0))]
    return pl.pallas_call(
        _mlstm_kernel,
        grid=(nc,),
        in_specs=specs(fwd) + specs(bwd) + [pl.BlockSpec((1, LANES), lambda i: (0, 0))],
        out_specs=[pl.BlockSpec((L, w), fwd(0)), pl.BlockSpec((L, w), bwd(0))],
        out_shape=[jax.ShapeDtypeStruct((s, w), BF16), jax.ShapeDtypeStruct((s, w), BF16)],
        scratch_shapes=[pltpu.VMEM((2 * N_MLSTM_HEADS, MLSTM_DH, MLSTM_AUG), F32),
                        pltpu.VMEM((2 * N_MLSTM_HEADS, LANES), F32)],
        compiler_params=_cparams(("arbitrary",)),
        name="mlstm",
    )(qk, qk, proj, gates, qk, qk, proj, gates, bg)


def _outproj_kernel(attn_ref, hf_ref, hb_ref, om_ref, x_ref, gm_ref, w_ref, gate_ref, g2_ref, sc_ref, sh_ref,
                    x1_ref, h2_ref, *, n_sub):
    gm = gm_ref[...]
    n_attn = attn_ref.shape[1]
    sub = x_ref.shape[0] // n_sub
    for t in range(n_sub):
        rows = slice(t * sub, (t + 1) * sub)
        hsum = hf_ref[rows, :].astype(F32) + hb_ref[rows, :].astype(F32)
        parts = []
        for h in range(N_MLSTM_HEADS):
            sl = slice(h * MLSTM_DH, (h + 1) * MLSTM_DH)
            seg = hsum[:, sl]
            parts.append(seg * _rms_scale(seg, MLSTM_DH) * gm[:, sl])
        ml = (jax.nn.sigmoid(om_ref[rows, :].astype(F32)) * jnp.concatenate(parts, axis=-1)).astype(BF16)
        mixed = (jnp.dot(attn_ref[rows, :], w_ref[:n_attn, :], preferred_element_type=F32)
                 + jnp.dot(ml, w_ref[n_attn:, :], preferred_element_type=F32))
        x1 = x_ref[rows, :] + gate_ref[...] * mixed
        x1_ref[rows, :] = x1
        h2_ref[rows, :] = _modulated_norm(x1, g2_ref[...], sc_ref[...], sh_ref[...]).astype(BF16)


def _outproj(attn, hf, hb, proj, x, gm, w_out, mod, g2, tm=512, n_sub=1):
    s, d = x.shape
    wm = hf.shape[1]
    row = lambda blk: pl.BlockSpec((1, d), lambda i: (0, blk))
    return pl.pallas_call(
        functools.partial(_outproj_kernel, n_sub=n_sub),
        grid=(s // tm,),
        in_specs=[pl.BlockSpec((tm, attn.shape[1]), lambda i: (i, 0)),
                  pl.BlockSpec((tm, wm), lambda i: (i, 0)),
                  pl.BlockSpec((tm, wm), lambda i: (i, 0)),
                  pl.BlockSpec((tm, wm), lambda i: (i, 3)),
                  pl.BlockSpec((tm, d), lambda i: (i, 0)),
                  pl.BlockSpec((1, wm), lambda i: (0, 0)),
                  pl.BlockSpec(w_out.shape, lambda i: (0, 0), pipeline_mode=pl.Buffered(1)),
                  row(2), row(0), row(4), row(3)],
        out_specs=[pl.BlockSpec((tm, d), lambda i: (i, 0)), pl.BlockSpec((tm, d), lambda i: (i, 0))],
        out_shape=[jax.ShapeDtypeStruct((s, d), F32), jax.ShapeDtypeStruct((s, d), BF16)],
        compiler_params=_cparams(("arbitrary",)),
        name="outproj",
    )(attn, hf, hb, proj, x, gm, w_out, mod, g2, mod, mod)


def _ffn_kernel(h2_ref, w1_ref, w2_ref, x1_hbm, gate_ref, o_ref, x1_buf, x1_sem, *, f_last):
    m, f = pl.program_id(0), pl.program_id(1)
    tm = o_ref.shape[0]
    x1_copy = pltpu.make_async_copy(x1_hbm.at[pl.ds(pl.multiple_of(m * tm, tm), tm), :], x1_buf, x1_sem)

    @pl.when(f == 0)
    def _():
        x1_copy.start()
        o_ref[...] = jnp.zeros(o_ref.shape, F32)

    a = jnp.maximum(jnp.dot(h2_ref[...], w1_ref[...], preferred_element_type=F32), 0.0)
    o_ref[...] += jnp.dot((a * a).astype(BF16), w2_ref[...], preferred_element_type=F32)

    @pl.when(f == f_last)
    def _():
        x1_copy.wait()
        o_ref[...] = x1_buf[...] + gate_ref[...] * o_ref[...]


def _ffn(h2, w1, w2, x1, mod, tm=1024, tf=1024):
    s, d = h2.shape
    dff = w1.shape[1]
    return pl.pallas_call(
        functools.partial(_ffn_kernel, f_last=dff // tf - 1),
        grid=(s // tm, dff // tf),
        in_specs=[pl.BlockSpec((tm, d), lambda m, f: (m, 0)),
                  pl.BlockSpec((d, tf), lambda m, f: (0, f)),
                  pl.BlockSpec((tf, d), lambda m, f: (f, 0)),
                  pl.BlockSpec(memory_space=pl.ANY),
                  pl.BlockSpec((1, d), lambda m, f: (0, 5))],
        out_specs=pl.BlockSpec((tm, d), lambda m, f: (m, 0)),
        out_shape=jax.ShapeDtypeStruct((s, d), F32),
        scratch_shapes=[pltpu.VMEM((tm, d), F32), pltpu.SemaphoreType.DMA(())],
        compiler_params=_cparams(("arbitrary", "arbitrary")),
        name="ffn",
    )(h2, w1, w2, x1, mod)


PACK_TILE = 512
PACK_SRC = 64
PACK_NSRC = PACK_TILE // PACK_SRC
W_IN_MIX_START = 832
W_IN_GATES_START = 4928
N_GATE_COLS = 16


def _pack_src_blocks():
    rows = []
    n_mix = 4 * N_MLSTM_HEADS * MLSTM_DH // PACK_TILE
    for n in range(n_mix):
        first = (W_IN_MIX_START + n * PACK_TILE) // PACK_SRC
        rows.append([first + i for i in range(PACK_NSRC)])
    rows.append(list(range(PACK_NSRC)))
    gates_blk = W_IN_GATES_START // PACK_SRC
    rows.append([8, 9, 10, 11, 12, 12, gates_blk, gates_blk])
    return jnp.asarray(rows, jnp.int32).reshape(-1)


def _pack_kernel(tbl_ref, *refs, n_mix):
    del tbl_ref
    srcs, o_ref = refs[:-1], refs[-1]
    n = pl.program_id(0)
    blk = lambda i: slice(i * PACK_SRC, (i + 1) * PACK_SRC)

    @pl.when(n <= n_mix)
    def _():
        for i, b in enumerate(srcs):
            o_ref[blk(i), :] = b[...].astype(BF16)

    @pl.when(n == n_mix + 1)
    def _():
        zeros = jnp.zeros(srcs[0].shape, BF16)
        for i in range(5):
            o_ref[blk(i), :] = srcs[i][...].astype(BF16)
        o_ref[blk(5), :] = zeros
        rows = lax.broadcasted_iota(jnp.int32, srcs[6].shape, 0)
        o_ref[blk(6), :] = jnp.where(rows < N_GATE_COLS, srcs[6][...], 0.0).astype(BF16)
        o_ref[blk(7), :] = zeros


def _pack_w_in(w_in_t):
    d = w_in_t.shape[1]
    tbl = _pack_src_blocks()
    n_tiles = tbl.shape[0] // PACK_NSRC
    src = lambda i: pl.BlockSpec((PACK_SRC, d), lambda n, t: (t[n * PACK_NSRC + i], 0))
    return pl.pallas_call(
        functools.partial(_pack_kernel, n_mix=n_tiles - 2),
        grid_spec=pltpu.PrefetchScalarGridSpec(
            num_scalar_prefetch=1, grid=(n_tiles,),
            in_specs=[src(i) for i in range(PACK_NSRC)],
            out_specs=pl.BlockSpec((PACK_TILE, d), lambda n, t: (n, 0))),
        out_shape=jax.ShapeDtypeStruct((n_tiles * PACK_TILE, d), BF16),
        compiler_params=_cparams(("arbitrary",)),
        name="packw",
    )(tbl, *([w_in_t] * PACK_NSRC))


def _pad_lanes(v, n):
    return jnp.pad(v, ((0, 0), (0, n - v.shape[1])))


def kernel(x, c, positions, w_ada, b_ada, norm_mix_g, w_in, b_gates, conv_w, conv_b, q_lora_g, w_uq,
           kv_lora_g, w_ukv, q_norm_g, k_norm_g, mlstm_norm_g, w_out, norm_mlp_g, w_ff1, w_ff2):
    bsz, s, d = x.shape
    assert bsz == 1, "kernels are written for a single sequence"
    xs = x[0]
    pos_row = positions.reshape(1, s)
    half = jnp.arange(ROPE_HALF, dtype=F32)
    freq = (ROPE_THETA ** (-half / ROPE_HALF)).reshape(ROPE_HALF, 1)
    row = lambda v: v.reshape(1, -1).astype(F32)

    for l in range(w_ada.shape[0]):
        mod = _mod(c.reshape(d, 1), w_ada[l], row(b_ada[l]))
        proj, gates = _inproj(xs, row(norm_mix_g[l]), mod, _pack_w_in(w_in[l].T))

        wuq_t = jnp.pad(w_uq[l].reshape(-1, N_MLA_HEADS, MLA_QK), ((0, 0), (0, 0), (0, MLA_QK_PAD - MLA_QK)))
        wuq_t = wuq_t.reshape(-1, N_MLA_HEADS * MLA_QK_PAD).T.astype(BF16)
        wkv = w_ukv[l].reshape(-1, N_MLA_HEADS, MLA_NOPE + MLA_V)
        wk = wkv[:, :, :MLA_NOPE].reshape(-1, N_MLA_HEADS * MLA_NOPE).astype(BF16)
        wv_t = wkv[:, :, MLA_NOPE:].reshape(-1, N_MLA_HEADS * MLA_V).T.astype(BF16)
        g_q, g_k = row(q_norm_g[l]), row(k_norm_g[l])
        q_t, k, v_t = _mlaprep(proj, pos_row, freq, row(q_lora_g[l]), row(kv_lora_g[l]), wuq_t, wk, wv_t,
                               g_q.reshape(-1, 1), _pad_lanes(g_k, MLA_QK_PAD))
        logit_bound = 1.02 * LOG2_E * MLA_QK ** 0.5 * jnp.max(jnp.abs(g_q)) * jnp.max(jnp.abs(g_k))
        attn, (w_out_b, w_ff1_b, w_ff2_b) = lax.cond(
            logit_bound <= SAFE_LOG2,
            functools.partial(_attention, shifted=False), functools.partial(_attention, shifted=True),
            q_t, k, v_t, (w_out[l], w_ff1[l], w_ff2[l]))

        qk = _conv(proj, conv_w[l], row(conv_b[l]))
        hf, hb = _mlstm(qk, proj, gates, _pad_lanes(row(b_gates[l]), LANES))

        x1, h2 = _outproj(attn, hf, hb, proj, xs, row(mlstm_norm_g[l]), w_out_b, mod, row(norm_mlp_g[l]))
        xs = _ffn(h2, w_ff1_b, w_ff2_b, x1, mod)
    return xs[None]
```

```python
import functools

import jax
import jax.numpy as jnp
from jax import lax
from jax.experimental import pallas as pl
from jax.experimental.pallas import tpu as pltpu

F32 = jnp.float32
BF16 = jnp.bfloat16

LANES = 128
N_MLA_HEADS = 8
MLA_NOPE = 128
ROPE_DIM = 64
ROPE_HALF = ROPE_DIM // 2
MLA_QK = MLA_NOPE + ROPE_DIM
MLA_QK_PAD = 256
MLA_V = 128
ATTN_TK = 256
NORM_ROWS = 128
ROPE_THETA = 10000.0
N_MLSTM_HEADS = 4
MLSTM_DH = 256
MLSTM_CHUNK = 256
MLSTM_AUG = MLSTM_DH + LANES
CONV_WIDTH = 5
CONV_HALO = 8
EPS = 1e-6
M_INIT = -1e30
LOG2_E = 1.4426950408889634
SAFE_LOG2 = 60.0
VMEM_LIMIT = 56 * 1024 * 1024


def _cparams(sem):
    return pltpu.CompilerParams(dimension_semantics=sem, vmem_limit_bytes=VMEM_LIMIT)


def _mod_kernel(c_ref, w_ref, b_ref, o_ref, sb_ref, *, tn):
    @pl.when(pl.program_id(0) == 0)
    def _():
        cc = c_ref[...]
        sb_ref[...] = jnp.broadcast_to(cc * jax.nn.sigmoid(cc), sb_ref.shape)

    sb = sb_ref[...]
    for j in range(tn // LANES):
        sl = slice(j * LANES, (j + 1) * LANES)
        o_ref[:, sl] = jnp.sum(w_ref[:, sl] * sb, axis=0, keepdims=True) + b_ref[:, sl]


def _mod(c_col, w_ada, b_ada, tn=1024):
    d, n = w_ada.shape
    return pl.pallas_call(
        functools.partial(_mod_kernel, tn=tn),
        grid=(n // tn,),
        in_specs=[pl.BlockSpec((d, 1), lambda j: (0, 0)),
                  pl.BlockSpec((d, tn), lambda j: (0, j)),
                  pl.BlockSpec((1, tn), lambda j: (0, j))],
        out_specs=pl.BlockSpec((1, tn), lambda j: (0, j)),
        out_shape=jax.ShapeDtypeStruct((1, n), F32),
        scratch_shapes=[pltpu.VMEM((d, LANES), F32)],
        compiler_params=_cparams(("arbitrary",)),
        name="mod",
    )(c_col, w_ada, b_ada)


def _modulated_norm(x, g, scale, shift):
    ms = jnp.mean(x * x, axis=-1, keepdims=True)
    return (x * lax.rsqrt(ms + EPS) * g) * (1.0 + scale) + shift


def _inproj_kernel(x_ref, g_ref, sc_ref, sh_ref, w_ref, o_ref, gate_ref, h_ref, *, n_last):
    n = pl.program_id(1)

    @pl.when(n == 0)
    def _():
        def body(r, carry):
            rows = pl.ds(pl.multiple_of(r * NORM_ROWS, NORM_ROWS), NORM_ROWS)
            h_ref[rows, :] = _modulated_norm(x_ref[rows, :], g_ref[...], sc_ref[...], sh_ref[...]).astype(BF16)
            return carry
        lax.fori_loop(0, x_ref.shape[0] // NORM_ROWS, body, 0)

    acc = lax.dot_general(h_ref[...], w_ref[...], (((1,), (1,)), ((), ())), preferred_element_type=F32)
    o_ref[...] = acc.astype(BF16)

    @pl.when(n == n_last)
    def _():
        gate_ref[...] = acc[:, -LANES:]


def _inproj(x, g, mod, w_t, tm=1024, tn=1024):
    s, d = x.shape
    n_tot = w_t.shape[0]
    return pl.pallas_call(
        functools.partial(_inproj_kernel, n_last=n_tot // tn - 1),
        grid=(s // tm, n_tot // tn),
        in_specs=[pl.BlockSpec((tm, d), lambda m, n: (m, 0)),
                  pl.BlockSpec((1, d), lambda m, n: (0, 0)),
                  pl.BlockSpec((1, d), lambda m, n: (0, 1)),
                  pl.BlockSpec((1, d), lambda m, n: (0, 0)),
                  pl.BlockSpec((tn, d), lambda m, n: (n, 0))],
        out_specs=[pl.BlockSpec((tm, tn), lambda m, n: (m, n)),
                   pl.BlockSpec((tm, LANES), lambda m, n: (m, 0))],
        out_shape=[jax.ShapeDtypeStruct((s, n_tot), BF16),
                   jax.ShapeDtypeStruct((s, LANES), F32)],
        scratch_shapes=[pltpu.VMEM((tm, d), BF16)],
        compiler_params=_cparams(("arbitrary", "arbitrary")),
        name="inproj",
    )(x, g, mod, mod, w_t)


def _rms_scale(x, n):
    return lax.rsqrt(jnp.sum(x * x, axis=-1, keepdims=True) * (1.0 / n) + EPS)


def _mlaprep_kernel(cq_ref, ckv_ref, kpe_ref, pos_ref, freq_ref, gql_ref, gkvl_ref, wuqt_ref, wk_ref, wvt_ref,
                    gqc_ref, gk_ref, q_ref, k_ref, v_ref):
    tm = cq_ref.shape[0]
    nt = (((1,), (1,)), ((), ()))
    cq = cq_ref[...].astype(F32)
    cqn = (cq * _rms_scale(cq, cq.shape[-1]) * gql_ref[...]).astype(BF16)
    ckv = ckv_ref[...].astype(F32)
    ckvn = (ckv * _rms_scale(ckv, ckv.shape[-1]) * gkvl_ref[...]).astype(BF16)
    qf_t = lax.dot_general(wuqt_ref[...], cqn, nt, preferred_element_type=F32)
    kf = jnp.dot(ckvn, wk_ref[...], preferred_element_type=F32)
    vf_t = lax.dot_general(wvt_ref[...], ckvn, nt, preferred_element_type=F32)

    ang_t = freq_ref[...] * pos_ref[...].astype(F32)
    cos_t, sin_t = jnp.cos(ang_t), jnp.sin(ang_t)

    gq_b = jnp.broadcast_to(gqc_ref[...] * (LOG2_E * MLA_QK ** -0.5), (MLA_QK, tm))
    q_pad = jnp.zeros((MLA_QK_PAD - MLA_QK, tm), F32)
    for h in range(N_MLA_HEADS):
        qh = qf_t[h * MLA_QK_PAD:h * MLA_QK_PAD + MLA_QK]
        r = lax.rsqrt(jnp.sum(qh * qh, axis=0, keepdims=True) * (1.0 / MLA_QK) + EPS)
        qn = qh * r * gq_b
        x1, x2 = qn[MLA_NOPE:MLA_NOPE + ROPE_HALF], qn[MLA_NOPE + ROPE_HALF:]
        q_ref[h] = jnp.concatenate([qn[:MLA_NOPE], x1 * cos_t - x2 * sin_t, x1 * sin_t + x2 * cos_t, q_pad],
                                   axis=0).astype(BF16)

    for h in range(N_MLA_HEADS):
        for c in range(tm // ATTN_TK):
            v_ref[h, c] = vf_t[h * MLA_V:(h + 1) * MLA_V, c * ATTN_TK:(c + 1) * ATTN_TK].astype(BF16)

    z_half, z_pad = jnp.zeros_like(cos_t), jnp.zeros((LANES - ROPE_DIM, tm), F32)
    c_tab = jnp.concatenate([cos_t, cos_t, z_pad], axis=0).T
    s_up = jnp.concatenate([z_half, sin_t, z_pad], axis=0).T
    s_dn = jnp.concatenate([-sin_t, z_half, z_pad], axis=0).T
    gk = gk_ref[...]
    kpe = kpe_ref[...].astype(F32)
    kpe_g = kpe * gk[:, MLA_NOPE:]
    kpe_rot = (kpe_g * c_tab + pltpu.roll(kpe_g, ROPE_HALF, 1) * s_up
               + pltpu.roll(kpe_g, LANES - ROPE_HALF, 1) * s_dn)
    kpe_ssq = jnp.sum(kpe * kpe, axis=-1, keepdims=True)
    for h in range(N_MLA_HEADS):
        kn = kf[:, h * MLA_NOPE:(h + 1) * MLA_NOPE]
        r = lax.rsqrt((jnp.sum(kn * kn, axis=-1, keepdims=True) + kpe_ssq) * (1.0 / MLA_QK) + EPS)
        k_ref[h] = jnp.concatenate([kn * r * gk[:, :MLA_NOPE], kpe_rot * r], axis=-1).astype(BF16)


def _mlaprep(proj, pos_row, freq, gql, gkvl, wuq_t, wk, wv_t, gq_col, gk, tm=512):
    s = proj.shape[0]
    hq = N_MLA_HEADS
    full = lambda a: pl.BlockSpec(a.shape, lambda i: (0,) * a.ndim)
    return pl.pallas_call(
        _mlaprep_kernel,
        grid=(s // tm,),
        in_specs=[pl.BlockSpec((tm, 512), lambda i: (i, 8)),
                  pl.BlockSpec((tm, 256), lambda i: (i, 18)),
                  pl.BlockSpec((tm, LANES), lambda i: (i, 38)),
                  pl.BlockSpec((1, tm), lambda i: (0, i)),
                  full(freq), full(gql), full(gkvl), full(wuq_t), full(wk), full(wv_t), full(gq_col), full(gk)],
        out_specs=[pl.BlockSpec((hq, MLA_QK_PAD, tm), lambda i: (0, 0, i)),
                   pl.BlockSpec((hq, tm, MLA_QK_PAD), lambda i: (0, i, 0)),
                   pl.BlockSpec((hq, tm // ATTN_TK, MLA_V, ATTN_TK), lambda i: (0, i, 0, 0))],
        out_shape=[jax.ShapeDtypeStruct((hq, MLA_QK_PAD, s), BF16),
                   jax.ShapeDtypeStruct((hq, s, MLA_QK_PAD), BF16),
                   jax.ShapeDtypeStruct((hq, s // ATTN_TK, MLA_V, ATTN_TK), BF16)],
        compiler_params=_cparams(("arbitrary",)),
        name="mlaprep",
    )(proj, proj, proj, pos_row, freq, gql, gkvl, wuq_t, wk, wv_t, gq_col, gk)


def _attn_kernel(qt_ref, k_ref, vt_ref, *refs, n_side, shifted, tk, ahead):
    side_in, o_ref, side_out = refs[:n_side], refs[n_side], refs[n_side + 1:]
    for src, dst in zip(side_in, side_out):
        dst[...] = src[...].astype(BF16)

    qt = qt_ref[0]
    tq = qt.shape[1]
    n_chunks = k_ref.shape[1] // tk

    def logits_t(j):
        return jnp.dot(k_ref[0, j * tk:(j + 1) * tk, :], qt, preferred_element_type=F32)

    def values_t(j, p_t):
        return jnp.dot(vt_ref[0, j], p_t.astype(BF16), preferred_element_type=F32)

    m = jnp.full((1, tq), -jnp.inf, F32)
    acc = l = None
    pending = [logits_t(j) for j in range(ahead)]
    for j in range(n_chunks):
        if j + ahead < n_chunks:
            pending.append(logits_t(j + ahead))
        s2 = pending.pop(0)
        if shifted:
            m_new = jnp.maximum(m, jnp.max(s2, axis=0, keepdims=True))
            p_t, alpha = jnp.exp2(s2 - m_new), jnp.exp2(m - m_new)
            pv, ps = values_t(j, p_t), jnp.sum(p_t, axis=0, keepdims=True)
            acc, l = (pv, ps) if acc is None else (acc * alpha + pv, l * alpha + ps)
            m = m_new
        else:
            p_t = jnp.exp2(s2)
            pv, ps = values_t(j, p_t), jnp.sum(p_t, axis=0, keepdims=True)
            acc, l = (pv, ps) if acc is None else (acc + pv, l + ps)
    o_ref[...] = (acc / l).T.astype(BF16)


def _attention(q_t, k, v_t, side_weights, *, shifted, tq=512, tk=ATTN_TK, ahead=2):
    hq, s, _ = k.shape
    n_q = s // tq
    n_steps = hq * n_q
    slab = lambda w: pl.BlockSpec((w.shape[0] // n_steps, w.shape[1]), lambda h, i: (h * n_q + i, 0))
    outs = pl.pallas_call(
        functools.partial(_attn_kernel, n_side=len(side_weights), shifted=shifted, tk=tk, ahead=ahead),
        grid=(hq, n_q),
        in_specs=[pl.BlockSpec((1, MLA_QK_PAD, tq), lambda h, i: (h, 0, i)),
                  pl.BlockSpec((1, s, MLA_QK_PAD), lambda h, i: (h, 0, 0)),
                  pl.BlockSpec((1, s // tk, MLA_V, tk), lambda h, i: (h, 0, 0, 0))]
                 + [slab(w) for w in side_weights],
        out_specs=[pl.BlockSpec((tq, MLA_V), lambda h, i: (i, h))] + [slab(w) for w in side_weights],
        out_shape=[jax.ShapeDtypeStruct((s, hq * MLA_V), BF16)]
                  + [jax.ShapeDtypeStruct(w.shape, BF16) for w in side_weights],
        compiler_params=_cparams(("arbitrary", "arbitrary")),
        name="attn_shifted" if shifted else "attn",
    )(q_t, k, v_t, *side_weights)
    return outs[0], tuple(outs[1:])


def _conv_kernel(x_ref, w_ref, b_ref, o_ref, pad_ref, *, rows, out_scale, transpose_out):
    s = x_ref.shape[0]
    zeros = jnp.zeros((CONV_HALO, LANES), F32)
    pad_ref[0:CONV_HALO, :] = zeros
    pad_ref[CONV_HALO + s:CONV_HALO + s + CONV_HALO, :] = zeros
    pad_ref[CONV_HALO:CONV_HALO + s, :] = x_ref[...].astype(F32)
    w = w_ref[...]
    b = b_ref[...]
    for r in range(s // rows):
        base = r * rows
        acc = b
        for j in range(CONV_WIDTH):
            off = CONV_HALO + j - CONV_WIDTH // 2
            acc = acc + w[j:j + 1, :] * pad_ref[base + off:base + off + rows, :]
        y = acc * jax.nn.sigmoid(acc) * out_scale
        if transpose_out:
            o_ref[:, base:base + rows] = y.T.astype(BF16)
        else:
            o_ref[base:base + rows, :] = y.astype(BF16)


def _conv(proj, conv_w, conv_b, *, first_tile, n_tiles, out_scale, transpose_out, rows=256):
    s = proj.shape[0]
    n_ch = n_tiles * LANES
    return pl.pallas_call(
        functools.partial(_conv_kernel, rows=rows, out_scale=out_scale, transpose_out=transpose_out),
        grid=(n_tiles,),
        in_specs=[pl.BlockSpec((s, LANES), lambda j: (0, first_tile + j)),
                  pl.BlockSpec((CONV_WIDTH, LANES), lambda j: (0, first_tile + j)),
                  pl.BlockSpec((1, LANES), lambda j: (0, first_tile + j))],
        out_specs=pl.BlockSpec((LANES, s), lambda j: (j, 0)) if transpose_out
                  else pl.BlockSpec((s, LANES), lambda j: (0, j)),
        out_shape=jax.ShapeDtypeStruct((n_ch, s) if transpose_out else (s, n_ch), BF16),
        scratch_shapes=[pltpu.VMEM((s + 2 * CONV_HALO, LANES), F32)],
        compiler_params=_cparams(("arbitrary",)),
        name="conv_k" if transpose_out else "conv_q",
    )(proj, conv_w, conv_b)


def _dot_f32_by_mask(mask01, x):
    hi = x.astype(BF16)
    rest = x - hi.astype(F32)
    mid = rest.astype(BF16)
    lo = (rest - mid.astype(F32)).astype(BF16)
    return (jnp.dot(mask01, hi, preferred_element_type=F32) + jnp.dot(mask01, mid, preferred_element_type=F32)
            + jnp.dot(mask01, lo, preferred_element_type=F32))


def _mlstm_chain(q, k_t, v_aug, g, g_t, lf, b_c, b_ct, mask, ic, fc, c_ref, m_ref, chain):
    b_col, b_row = b_c[:, fc:fc + 1], b_ct[fc:fc + 1, :]
    i_row = g_t[ic:ic + 1, :]
    b_tot = jnp.sum(lf[:, fc:fc + 1], axis=0, keepdims=True)
    m_prev = m_ref[chain:chain + 1, 0:1]

    log_d = jnp.where(mask, b_col - b_row + i_row, -jnp.inf)
    log_inter = b_col + m_prev
    m_t = jnp.maximum(log_inter, jnp.max(log_d, axis=-1, keepdims=True))
    d_m = jnp.exp(log_d - m_t)
    w_inter = jnp.exp(log_inter - m_t)

    c_aug = c_ref[chain]
    scores = jnp.dot(q, k_t, preferred_element_type=F32) * d_m
    r = (jnp.dot(scores.astype(BF16), v_aug, preferred_element_type=F32)
         + w_inter * jnp.dot(q, c_aug.astype(BF16), preferred_element_type=F32))
    num, den = r[:, :MLSTM_DH], r[:, MLSTM_DH:MLSTM_DH + 1]
    h = num / jnp.maximum(jnp.abs(den), jnp.exp(-m_t))

    log_w = b_tot - b_row + i_row
    m_new = jnp.maximum(b_tot + m_prev, jnp.max(log_w, axis=1, keepdims=True))
    decay = jnp.exp(b_tot + m_prev - m_new)
    kw_t = (k_t.astype(F32) * jnp.exp(log_w - m_new)).astype(BF16)
    c_ref[chain] = decay * c_aug + jnp.dot(kw_t, v_aug, preferred_element_type=F32)
    m_ref[chain:chain + 1, :] = jnp.broadcast_to(m_new, (1, LANES))
    return h


def _mlstm_kernel(qf_ref, kf_ref, vf_ref, gf_ref, qb_ref, kb_ref, vb_ref, gb_ref, bg_ref,
                  hf_ref, hb_ref, c_ref, m_ref):
    @pl.when(pl.program_id(0) == 0)
    def _():
        c_ref[...] = jnp.zeros(c_ref.shape, F32)
        m_ref[...] = jnp.full(m_ref.shape, M_INIT, F32)

    L = MLSTM_CHUNK
    row = lax.broadcasted_iota(jnp.int32, (L, L), 0)
    col = lax.broadcasted_iota(jnp.int32, (L, L), 1)
    ones_col = (lax.broadcasted_iota(jnp.int32, (L, LANES), 1) == 0).astype(BF16)
    hm = N_MLSTM_HEADS
    for d, (q_ref, k_ref, v_ref, g_ref, o_ref) in enumerate(
            ((qf_ref, kf_ref, vf_ref, gf_ref, hf_ref), (qb_ref, kb_ref, vb_ref, gb_ref, hb_ref))):
        mask = (row >= col) if d == 0 else (row <= col)
        g = g_ref[...] + bg_ref[...]
        lf = jax.nn.log_sigmoid(g)
        b_c = _dot_f32_by_mask(mask.astype(BF16), lf)
        g_t, b_ct = g.T, b_c.T
        for h in range(hm):
            sl = slice(h * MLSTM_DH, (h + 1) * MLSTM_DH)
            v_aug = jnp.concatenate([v_ref[:, sl], ones_col], axis=-1)
            o_ref[:, sl] = _mlstm_chain(q_ref[:, sl], k_ref[sl, :], v_aug, g, g_t, lf, b_c, b_ct, mask,
                                        2 * d * hm + h, (2 * d + 1) * hm + h, c_ref, m_ref,
                                        d * hm + h).astype(o_ref.dtype)


def _mlstm(q, k_t, proj, gates, bg):
    s = q.shape[0]
    L = MLSTM_CHUNK
    nc = s // L
    w = N_MLSTM_HEADS * MLSTM_DH
    fwd = lambda blk: (lambda i: (i, blk))
    bwd = lambda blk: (lambda i: (nc - 1 - i, blk))
    specs = lambda ix: [pl.BlockSpec((L, w), ix(0)),
                        pl.BlockSpec((w, L), lambda i: ix(0)(i)[::-1]),
                        pl.BlockSpec((L, w), ix(2)),
                        pl.BlockSpec((L, LANES), ix(0))]
    return pl.pallas_call(
        _mlstm_kernel,
        grid=(nc,),
        in_specs=specs(fwd) + specs(bwd) + [pl.BlockSpec((1, LANES), lambda i: (0, 0))],
        out_specs=[pl.BlockSpec((L, w), fwd(0)), pl.BlockSpec((L, w), bwd(0))],
        out_shape=[jax.ShapeDtypeStruct((s, w), BF16), jax.ShapeDtypeStruct((s, w), BF16)],
        scratch_shapes=[pltpu.VMEM((2 * N_MLSTM_HEADS, MLSTM_DH, MLSTM_AUG), F32),
                        pltpu.VMEM((2 * N_MLSTM_HEADS, LANES), F32)],
        compiler_params=_cparams(("arbitrary",)),
        name="mlstm",
    )(q, k_t, proj, gates, q, k_t, proj, gates, bg)


def _outproj_kernel(attn_ref, hf_ref, hb_ref, om_ref, x_ref, gm_ref, w_ref, gate_ref, g2_ref, sc_ref, sh_ref,
                    x1_ref, h2_ref, *, n_sub):
    gm = gm_ref[...]
    n_attn = attn_ref.shape[1]
    sub = x_ref.shape[0] // n_sub
    for t in range(n_sub):
        rows = slice(t * sub, (t + 1) * sub)
        hsum = hf_ref[rows, :].astype(F32) + hb_ref[rows, :].astype(F32)
        parts = []
        for h in range(N_MLSTM_HEADS):
            sl = slice(h * MLSTM_DH, (h + 1) * MLSTM_DH)
            seg = hsum[:, sl]
            parts.append(seg * _rms_scale(seg, MLSTM_DH) * gm[:, sl])
        ml = (jax.nn.sigmoid(om_ref[rows, :].astype(F32)) * jnp.concatenate(parts, axis=-1)).astype(BF16)
        mixed = (jnp.dot(attn_ref[rows, :], w_ref[:n_attn, :], preferred_element_type=F32)
                 + jnp.dot(ml, w_ref[n_attn:, :], preferred_element_type=F32))
        x1 = x_ref[rows, :] + gate_ref[...] * mixed
        x1_ref[rows, :] = x1
        h2_ref[rows, :] = _modulated_norm(x1, g2_ref[...], sc_ref[...], sh_ref[...]).astype(BF16)


def _outproj(attn, hf, hb, proj, x, gm, w_out, mod, g2, tm=512, n_sub=1):
    s, d = x.shape
    wm = hf.shape[1]
    row = lambda blk: pl.BlockSpec((1, d), lambda i: (0, blk))
    return pl.pallas_call(
        functools.partial(_outproj_kernel, n_sub=n_sub),
        grid=(s // tm,),
        in_specs=[pl.BlockSpec((tm, attn.shape[1]), lambda i: (i, 0)),
                  pl.BlockSpec((tm, wm), lambda i: (i, 0)),
                  pl.BlockSpec((tm, wm), lambda i: (i, 0)),
                  pl.BlockSpec((tm, wm), lambda i: (i, 3)),
                  pl.BlockSpec((tm, d), lambda i: (i, 0)),
                  pl.BlockSpec((1, wm), lambda i: (0, 0)),
                  pl.BlockSpec(w_out.shape, lambda i: (0, 0), pipeline_mode=pl.Buffered(1)),
                  row(2), row(0), row(4), row(3)],
        out_specs=[pl.BlockSpec((tm, d), lambda i: (i, 0)), pl.BlockSpec((tm, d), lambda i: (i, 0))],
        out_shape=[jax.ShapeDtypeStruct((s, d), F32), jax.ShapeDtypeStruct((s, d), BF16)],
        compiler_params=_cparams(("arbitrary",)),
        name="outproj",
    )(attn, hf, hb, proj, x, gm, w_out, mod, g2, mod, mod)


def _ffn_kernel(h2_ref, w1_ref, w2_ref, x1_hbm, gate_ref, o_ref, x1_buf, x1_sem, *, f_last):
    m, f = pl.program_id(0), pl.program_id(1)
    tm = o_ref.shape[0]
    x1_copy = pltpu.make_async_copy(x1_hbm.at[pl.ds(pl.multiple_of(m * tm, tm), tm), :], x1_buf, x1_sem)

    @pl.when(f == 0)
    def _():
        x1_copy.start()
        o_ref[...] = jnp.zeros(o_ref.shape, F32)

    a = jnp.maximum(jnp.dot(h2_ref[...], w1_ref[...], preferred_element_type=F32), 0.0)
    o_ref[...] += jnp.dot((a * a).astype(BF16), w2_ref[...], preferred_element_type=F32)

    @pl.when(f == f_last)
    def _():
        x1_copy.wait()
        o_ref[...] = x1_buf[...] + gate_ref[...] * o_ref[...]


def _ffn(h2, w1, w2, x1, mod, tm=1024, tf=1024):
    s, d = h2.shape
    dff = w1.shape[1]
    return pl.pallas_call(
        functools.partial(_ffn_kernel, f_last=dff // tf - 1),
        grid=(s // tm, dff // tf),
        in_specs=[pl.BlockSpec((tm, d), lambda m, f: (m, 0)),
                  pl.BlockSpec((d, tf), lambda m, f: (0, f)),
                  pl.BlockSpec((tf, d), lambda m, f: (f, 0)),
                  pl.BlockSpec(memory_space=pl.ANY),
                  pl.BlockSpec((1, d), lambda m, f: (0, 5))],
        out_specs=pl.BlockSpec((tm, d), lambda m, f: (m, 0)),
        out_shape=jax.ShapeDtypeStruct((s, d), F32),
        scratch_shapes=[pltpu.VMEM((tm, d), F32), pltpu.SemaphoreType.DMA(())],
        compiler_params=_cparams(("arbitrary", "arbitrary")),
        name="ffn",
    )(h2, w1, w2, x1, mod)


PACK_TILE = 512
PACK_SRC = 64
PACK_NSRC = PACK_TILE // PACK_SRC
W_IN_MIX_START = 832
W_IN_GATES_START = 4928
N_GATE_COLS = 16


def _pack_src_blocks():
    rows = []
    n_mix = 4 * N_MLSTM_HEADS * MLSTM_DH // PACK_TILE
    for n in range(n_mix):
        first = (W_IN_MIX_START + n * PACK_TILE) // PACK_SRC
        rows.append([first + i for i in range(PACK_NSRC)])
    rows.append(list(range(PACK_NSRC)))
    gates_blk = W_IN_GATES_START // PACK_SRC
    rows.append([8, 9, 10, 11, 12, 12, gates_blk, gates_blk])
    return jnp.asarray(rows, jnp.int32).reshape(-1)


def _pack_kernel(tbl_ref, *refs, n_mix):
    del tbl_ref
    srcs, o_ref = refs[:-1], refs[-1]
    n = pl.program_id(0)
    blk = lambda i: slice(i * PACK_SRC, (i + 1) * PACK_SRC)

    @pl.when(n <= n_mix)
    def _():
        for i, b in enumerate(srcs):
            o_ref[blk(i), :] = b[...].astype(BF16)

    @pl.when(n == n_mix + 1)
    def _():
        zeros = jnp.zeros(srcs[0].shape, BF16)
        for i in range(5):
            o_ref[blk(i), :] = srcs[i][...].astype(BF16)
        o_ref[blk(5), :] = zeros
        rows = lax.broadcasted_iota(jnp.int32, srcs[6].shape, 0)
        o_ref[blk(6), :] = jnp.where(rows < N_GATE_COLS, srcs[6][...], 0.0).astype(BF16)
        o_ref[blk(7), :] = zeros


def _pack_w_in(w_in_t):
    d = w_in_t.shape[1]
    tbl = _pack_src_blocks()
    n_tiles = tbl.shape[0] // PACK_NSRC
    src = lambda i: pl.BlockSpec((PACK_SRC, d), lambda n, t: (t[n * PACK_NSRC + i], 0))
    return pl.pallas_call(
        functools.partial(_pack_kernel, n_mix=n_tiles - 2),
        grid_spec=pltpu.PrefetchScalarGridSpec(
            num_scalar_prefetch=1, grid=(n_tiles,),
            in_specs=[src(i) for i in range(PACK_NSRC)],
            out_specs=pl.BlockSpec((PACK_TILE, d), lambda n, t: (n, 0))),
        out_shape=jax.ShapeDtypeStruct((n_tiles * PACK_TILE, d), BF16),
        compiler_params=_cparams(("arbitrary",)),
        name="packw",
    )(tbl, *([w_in_t] * PACK_NSRC))


def _pad_lanes(v, n):
    return jnp.pad(v, ((0, 0), (0, n - v.shape[1])))


def kernel(x, c, positions, w_ada, b_ada, norm_mix_g, w_in, b_gates, conv_w, conv_b, q_lora_g, w_uq,
           kv_lora_g, w_ukv, q_norm_g, k_norm_g, mlstm_norm_g, w_out, norm_mlp_g, w_ff1, w_ff2):
    bsz, s, d = x.shape
    assert bsz == 1, "kernels are written for a single sequence"
    xs = x[0]
    pos_row = positions.reshape(1, s)
    half = jnp.arange(ROPE_HALF, dtype=F32)
    freq = (ROPE_THETA ** (-half / ROPE_HALF)).reshape(ROPE_HALF, 1)
    row = lambda v: v.reshape(1, -1).astype(F32)

    for l in range(w_ada.shape[0]):
        mod = _mod(c.reshape(d, 1), w_ada[l], row(b_ada[l]))
        proj, gates = _inproj(xs, row(norm_mix_g[l]), mod, _pack_w_in(w_in[l].T))

        wuq_t = jnp.pad(w_uq[l].reshape(-1, N_MLA_HEADS, MLA_QK), ((0, 0), (0, 0), (0, MLA_QK_PAD - MLA_QK)))
        wuq_t = wuq_t.reshape(-1, N_MLA_HEADS * MLA_QK_PAD).T.astype(BF16)
        wkv = w_ukv[l].reshape(-1, N_MLA_HEADS, MLA_NOPE + MLA_V)
        wk = wkv[:, :, :MLA_NOPE].reshape(-1, N_MLA_HEADS * MLA_NOPE).astype(BF16)
        wv_t = wkv[:, :, MLA_NOPE:].reshape(-1, N_MLA_HEADS * MLA_V).T.astype(BF16)
        g_q, g_k = row(q_norm_g[l]), row(k_norm_g[l])
        q_t, k, v_t = _mlaprep(proj, pos_row, freq, row(q_lora_g[l]), row(kv_lora_g[l]), wuq_t, wk, wv_t,
                               g_q.reshape(-1, 1), _pad_lanes(g_k, MLA_QK_PAD))
        logit_bound = 1.02 * LOG2_E * MLA_QK ** 0.5 * jnp.max(jnp.abs(g_q)) * jnp.max(jnp.abs(g_k))
        attn, (w_out_b, w_ff1_b, w_ff2_b) = lax.cond(
            logit_bound <= SAFE_LOG2,
            functools.partial(_attention, shifted=False), functools.partial(_attention, shifted=True),
            q_t, k, v_t, (w_out[l], w_ff1[l], w_ff2[l]))

        n_qk_tiles = N_MLSTM_HEADS * MLSTM_DH // LANES
        q_m = _conv(proj, conv_w[l], row(conv_b[l]), first_tile=0, n_tiles=n_qk_tiles, out_scale=1.0,
                    transpose_out=False)
        k_m_t = _conv(proj, conv_w[l], row(conv_b[l]), first_tile=n_qk_tiles, n_tiles=n_qk_tiles,
                      out_scale=MLSTM_DH ** -0.5, transpose_out=True)
        hf, hb = _mlstm(q_m, k_m_t, proj, gates, _pad_lanes(row(b_gates[l]), LANES))

        x1, h2 = _outproj(attn, hf, hb, proj, xs, row(mlstm_norm_g[l]), w_out_b, mod, row(norm_mlp_g[l]))
        xs = _ffn(h2, w_ff1_b, w_ff2_b, x1, mod)
    return xs[None]
```

```python
import functools

import jax
import jax.numpy as jnp
from jax import lax
from jax.experimental import pallas as pl
from jax.experimental.pallas import tpu as pltpu

F32 = jnp.float32
BF16 = jnp.bfloat16

LANES = 128
N_MLA_HEADS = 8
MLA_NOPE = 128
ROPE_DIM = 64
ROPE_HALF = ROPE_DIM // 2
MLA_QK = MLA_NOPE + ROPE_DIM
MLA_QK_PAD = 256
MLA_V = 128
ATTN_TK = 256
NORM_ROWS = 128
ROPE_THETA = 10000.0
N_MLSTM_HEADS = 4
MLSTM_DH = 256
MLSTM_CHUNK = 256
MLSTM_AUG = MLSTM_DH + LANES
CONV_WIDTH = 5
CONV_HALO = 8
EPS = 1e-6
M_INIT = -1e30
LOG2_E = 1.4426950408889634
SAFE_LOG2 = 60.0
VMEM_LIMIT = 56 * 1024 * 1024


def _cparams(sem):
    return pltpu.CompilerParams(dimension_semantics=sem, vmem_limit_bytes=VMEM_LIMIT)


def _mod_kernel(c_ref, w_ref, b_ref, o_ref, sb_ref, *, tn):
    @pl.when(pl.program_id(0) == 0)
    def _():
        cc = c_ref[...]
        sb_ref[...] = jnp.broadcast_to(cc * jax.nn.sigmoid(cc), sb_ref.shape)

    sb = sb_ref[...]
    for j in range(tn // LANES):
        sl = slice(j * LANES, (j + 1) * LANES)
        o_ref[:, sl] = jnp.sum(w_ref[:, sl] * sb, axis=0, keepdims=True) + b_ref[:, sl]


def _mod(c_col, w_ada, b_ada, tn=1024):
    d, n = w_ada.shape
    return pl.pallas_call(
        functools.partial(_mod_kernel, tn=tn),
        grid=(n // tn,),
        in_specs=[pl.BlockSpec((d, 1), lambda j: (0, 0)),
                  pl.BlockSpec((d, tn), lambda j: (0, j)),
                  pl.BlockSpec((1, tn), lambda j: (0, j))],
        out_specs=pl.BlockSpec((1, tn), lambda j: (0, j)),
        out_shape=jax.ShapeDtypeStruct((1, n), F32),
        scratch_shapes=[pltpu.VMEM((d, LANES), F32)],
        compiler_params=_cparams(("arbitrary",)),
        name="mod",
    )(c_col, w_ada, b_ada)


def _modulated_norm(x, g, scale, shift):
    ms = jnp.mean(x * x, axis=-1, keepdims=True)
    return (x * lax.rsqrt(ms + EPS) * g) * (1.0 + scale) + shift


def _inproj_kernel(x_ref, g_ref, sc_ref, sh_ref, w_ref, o_ref, gate_ref, h_ref, *, n_last):
    n = pl.program_id(1)

    @pl.when(n == 0)
    def _():
        def body(r, carry):
            rows = pl.ds(pl.multiple_of(r * NORM_ROWS, NORM_ROWS), NORM_ROWS)
            h_ref[rows, :] = _modulated_norm(x_ref[rows, :], g_ref[...], sc_ref[...], sh_ref[...]).astype(BF16)
            return carry
        lax.fori_loop(0, x_ref.shape[0] // NORM_ROWS, body, 0)

    acc = lax.dot_general(h_ref[...], w_ref[...], (((1,), (1,)), ((), ())), preferred_element_type=F32)
    o_ref[...] = acc.astype(BF16)

    @pl.when(n == n_last)
    def _():
        gate_ref[...] = acc[:, -LANES:]


def _inproj(x, g, mod, w_t, tm=1024, tn=1024):
    s, d = x.shape
    n_tot = w_t.shape[0]
    return pl.pallas_call(
        functools.partial(_inproj_kernel, n_last=n_tot // tn - 1),
        grid=(s // tm, n_tot // tn),
        in_specs=[pl.BlockSpec((tm, d), lambda m, n: (m, 0)),
                  pl.BlockSpec((1, d), lambda m, n: (0, 0)),
                  pl.BlockSpec((1, d), lambda m, n: (0, 1)),
                  pl.BlockSpec((1, d), lambda m, n: (0, 0)),
                  pl.BlockSpec((tn, d), lambda m, n: (n, 0))],
        out_specs=[pl.BlockSpec((tm, tn), lambda m, n: (m, n)),
                   pl.BlockSpec((tm, LANES), lambda m, n: (m, 0))],
        out_shape=[jax.ShapeDtypeStruct((s, n_tot), BF16),
                   jax.ShapeDtypeStruct((s, LANES), F32)],
        scratch_shapes=[pltpu.VMEM((tm, d), BF16)],
        compiler_params=_cparams(("arbitrary", "arbitrary")),
        name="inproj",
    )(x, g, mod, mod, w_t)


def _rms_scale(x, n):
    return lax.rsqrt(jnp.sum(x * x, axis=-1, keepdims=True) * (1.0 / n) + EPS)


def _mlaprep_kernel(cq_ref, ckv_ref, kpe_ref, pos_ref, freq_ref, gql_ref, gkvl_ref, wuqt_ref, wk_ref, wvt_ref,
                    gqc_ref, gk_ref, q_ref, k_ref, v_ref):
    tm = cq_ref.shape[0]
    nt = (((1,), (1,)), ((), ()))
    cq = cq_ref[...].astype(F32)
    cqn = (cq * _rms_scale(cq, cq.shape[-1]) * gql_ref[...]).astype(BF16)
    ckv = ckv_ref[...].astype(F32)
    ckvn = (ckv * _rms_scale(ckv, ckv.shape[-1]) * gkvl_ref[...]).astype(BF16)
    qf_t = lax.dot_general(wuqt_ref[...], cqn, nt, preferred_element_type=F32)
    kf = jnp.dot(ckvn, wk_ref[...], preferred_element_type=F32)
    vf_t = lax.dot_general(wvt_ref[...], ckvn, nt, preferred_element_type=F32)

    ang_t = freq_ref[...] * pos_ref[...].astype(F32)
    cos_t, sin_t = jnp.cos(ang_t), jnp.sin(ang_t)

    gq_b = jnp.broadcast_to(gqc_ref[...] * (LOG2_E * MLA_QK ** -0.5), (MLA_QK, tm))
    q_pad = jnp.zeros((MLA_QK_PAD - MLA_QK, tm), F32)
    for h in range(N_MLA_HEADS):
        qh = qf_t[h * MLA_QK_PAD:h * MLA_QK_PAD + MLA_QK]
        r = lax.rsqrt(jnp.sum(qh * qh, axis=0, keepdims=True) * (1.0 / MLA_QK) + EPS)
        qn = qh * r * gq_b
        x1, x2 = qn[MLA_NOPE:MLA_NOPE + ROPE_HALF], qn[MLA_NOPE + ROPE_HALF:]
        q_ref[h] = jnp.concatenate([qn[:MLA_NOPE], x1 * cos_t - x2 * sin_t, x1 * sin_t + x2 * cos_t, q_pad],
                                   axis=0).astype(BF16)

    for h in range(N_MLA_HEADS):
        for c in range(tm // ATTN_TK):
            v_ref[h, c] = vf_t[h * MLA_V:(h + 1) * MLA_V, c * ATTN_TK:(c + 1) * ATTN_TK].astype(BF16)

    z_half, z_pad = jnp.zeros_like(cos_t), jnp.zeros((LANES - ROPE_DIM, tm), F32)
    c_tab = jnp.concatenate([cos_t, cos_t, z_pad], axis=0).T
    s_up = jnp.concatenate([z_half, sin_t, z_pad], axis=0).T
    s_dn = jnp.concatenate([-sin_t, z_half, z_pad], axis=0).T
    gk = gk_ref[...]
    kpe = kpe_ref[...].astype(F32)
    kpe_g = kpe * gk[:, MLA_NOPE:]
    kpe_rot = (kpe_g * c_tab + pltpu.roll(kpe_g, ROPE_HALF, 1) * s_up
               + pltpu.roll(kpe_g, LANES - ROPE_HALF, 1) * s_dn)
    kpe_ssq = jnp.sum(kpe * kpe, axis=-1, keepdims=True)
    for h in range(N_MLA_HEADS):
        kn = kf[:, h * MLA_NOPE:(h + 1) * MLA_NOPE]
        r = lax.rsqrt((jnp.sum(kn * kn, axis=-1, keepdims=True) + kpe_ssq) * (1.0 / MLA_QK) + EPS)
        k_ref[h] = jnp.concatenate([kn * r * gk[:, :MLA_NOPE], kpe_rot * r], axis=-1).astype(BF16)


def _mlaprep(proj, pos_row, freq, gql, gkvl, wuq_t, wk, wv_t, gq_col, gk, tm=512):
    s = proj.shape[0]
    hq = N_MLA_HEADS
    full = lambda a: pl.BlockSpec(a.shape, lambda i: (0,) * a.ndim)
    return pl.pallas_call(
        _mlaprep_kernel,
        grid=(s // tm,),
        in_specs=[pl.BlockSpec((tm, 512), lambda i: (i, 8)),
                  pl.BlockSpec((tm, 256), lambda i: (i, 18)),
                  pl.BlockSpec((tm, LANES), lambda i: (i, 38)),
                  pl.BlockSpec((1, tm), lambda i: (0, i)),
                  full(freq), full(gql), full(gkvl), full(wuq_t), full(wk), full(wv_t), full(gq_col), full(gk)],
        out_specs=[pl.BlockSpec((hq, MLA_QK_PAD, tm), lambda i: (0, 0, i)),
                   pl.BlockSpec((hq, tm, MLA_QK_PAD), lambda i: (0, i, 0)),
                   pl.BlockSpec((hq, tm // ATTN_TK, MLA_V, ATTN_TK), lambda i: (0, i, 0, 0))],
        out_shape=[jax.ShapeDtypeStruct((hq, MLA_QK_PAD, s), BF16),
                   jax.ShapeDtypeStruct((hq, s, MLA_QK_PAD), BF16),
                   jax.ShapeDtypeStruct((hq, s // ATTN_TK, MLA_V, ATTN_TK), BF16)],
        compiler_params=_cparams(("arbitrary",)),
        name="mlaprep",
    )(proj, proj, proj, pos_row, freq, gql, gkvl, wuq_t, wk, wv_t, gq_col, gk)


def _attn_kernel(qt_ref, k_ref, vt_ref, *refs, n_side, shifted, tk, ahead):
    side_in, o_ref, side_out = refs[:n_side], refs[n_side], refs[n_side + 1:]
    for src, dst in zip(side_in, side_out):
        dst[...] = src[...].astype(BF16)

    qt = qt_ref[0]
    tq = qt.shape[1]
    n_chunks = k_ref.shape[1] // tk

    def logits_t(j):
        return jnp.dot(k_ref[0, j * tk:(j + 1) * tk, :], qt, preferred_element_type=F32)

    def values_t(j, p_t):
        return jnp.dot(vt_ref[0, j], p_t.astype(BF16), preferred_element_type=F32)

    m = jnp.full((1, tq), -jnp.inf, F32)
    acc = l = None
    pending = [logits_t(j) for j in range(ahead)]
    for j in range(n_chunks):
        if j + ahead < n_chunks:
            pending.append(logits_t(j + ahead))
        s2 = pending.pop(0)
        if shifted:
            m_new = jnp.maximum(m, jnp.max(s2, axis=0, keepdims=True))
            p_t, alpha = jnp.exp2(s2 - m_new), jnp.exp2(m - m_new)
            pv, ps = values_t(j, p_t), jnp.sum(p_t, axis=0, keepdims=True)
            acc, l = (pv, ps) if acc is None else (acc * alpha + pv, l * alpha + ps)
            m = m_new
        else:
            p_t = jnp.exp2(s2)
            pv, ps = values_t(j, p_t), jnp.sum(p_t, axis=0, keepdims=True)
            acc, l = (pv, ps) if acc is None else (acc + pv, l + ps)
    o_ref[...] = (acc / l).T.astype(BF16)


def _attention(q_t, k, v_t, side_weights, *, shifted, tq=512, tk=ATTN_TK, ahead=2):
    hq, s, _ = k.shape
    n_q = s // tq
    n_steps = hq * n_q
    slab = lambda w: pl.BlockSpec((w.shape[0] // n_steps, w.shape[1]), lambda h, i: (h * n_q + i, 0))
    outs = pl.pallas_call(
        functools.partial(_attn_kernel, n_side=len(side_weights), shifted=shifted, tk=tk, ahead=ahead),
        grid=(hq, n_q),
        in_specs=[pl.BlockSpec((1, MLA_QK_PAD, tq), lambda h, i: (h, 0, i)),
                  pl.BlockSpec((1, s, MLA_QK_PAD), lambda h, i: (h, 0, 0)),
                  pl.BlockSpec((1, s // tk, MLA_V, tk), lambda h, i: (h, 0, 0, 0))]
                 + [slab(w) for w in side_weights],
        out_specs=[pl.BlockSpec((tq, MLA_V), lambda h, i: (i, h))] + [slab(w) for w in side_weights],
        out_shape=[jax.ShapeDtypeStruct((s, hq * MLA_V), BF16)]
                  + [jax.ShapeDtypeStruct(w.shape, BF16) for w in side_weights],
        compiler_params=_cparams(("arbitrary", "arbitrary")),
        name="attn_shifted" if shifted else "attn",
    )(q_t, k, v_t, *side_weights)
    return outs[0], tuple(outs[1:])


def _conv_kernel(x_ref, w_ref, b_ref, o_ref, pad_ref, *, rows, out_scale, transpose_out):
    s = x_ref.shape[0]
    zeros = jnp.zeros((CONV_HALO, LANES), F32)
    pad_ref[0:CONV_HALO, :] = zeros
    pad_ref[CONV_HALO + s:CONV_HALO + s + CONV_HALO, :] = zeros
    pad_ref[CONV_HALO:CONV_HALO + s, :] = x_ref[...].astype(F32)
    w = w_ref[...]
    b = b_ref[...]
    for r in range(s // rows):
        base = r * rows
        acc = b
        for j in range(CONV_WIDTH):
            off = CONV_HALO + j - CONV_WIDTH // 2
            acc = acc + w[j:j + 1, :] * pad_ref[base + off:base + off + rows, :]
        y = acc * jax.nn.sigmoid(acc) * out_scale
        if transpose_out:
            o_ref[:, base:base + rows] = y.T.astype(BF16)
        else:
            o_ref[base:base + rows, :] = y.astype(BF16)


def _conv(proj, conv_w, conv_b, *, first_tile, n_tiles, out_scale, transpose_out, rows=256):
    s = proj.shape[0]
    n_ch = n_tiles * LANES
    return pl.pallas_call(
        functools.partial(_conv_kernel, rows=rows, out_scale=out_scale, transpose_out=transpose_out),
        grid=(n_tiles,),
        in_specs=[pl.BlockSpec((s, LANES), lambda j: (0, first_tile + j)),
                  pl.BlockSpec((CONV_WIDTH, LANES), lambda j: (0, first_tile + j)),
                  pl.BlockSpec((1, LANES), lambda j: (0, first_tile + j))],
        out_specs=pl.BlockSpec((LANES, s), lambda j: (j, 0)) if transpose_out
                  else pl.BlockSpec((s, LANES), lambda j: (0, j)),
        out_shape=jax.ShapeDtypeStruct((n_ch, s) if transpose_out else (s, n_ch), BF16),
        scratch_shapes=[pltpu.VMEM((s + 2 * CONV_HALO, LANES), F32)],
        compiler_params=_cparams(("arbitrary",)),
        name="conv_k" if transpose_out else "conv_q",
    )(proj, conv_w, conv_b)


def _dot_f32_by_mask(mask01, x):
    hi = x.astype(BF16)
    rest = x - hi.astype(F32)
    mid = rest.astype(BF16)
    lo = (rest - mid.astype(F32)).astype(BF16)
    return (jnp.dot(mask01, hi, preferred_element_type=F32) + jnp.dot(mask01, mid, preferred_element_type=F32)
            + jnp.dot(mask01, lo, preferred_element_type=F32))


def _scan_max_rows(x, row_id, reverse):
    n, s = x.shape[0], 1
    while s < n:
        shifted = pltpu.roll(x, n - s if reverse else s, 0)
        ok = (row_id < n - s) if reverse else (row_id >= s)
        x = jnp.maximum(x, jnp.where(ok, shifted, -jnp.inf))
        s *= 2
    return x


def _mlstm_kernel(qf_ref, kf_ref, vf_ref, gf_ref, qb_ref, kb_ref, vb_ref, gb_ref, bg_ref,
                  hf_ref, hb_ref, c_ref, m_ref):
    @pl.when(pl.program_id(0) == 0)
    def _():
        c_ref[...] = jnp.zeros(c_ref.shape, F32)
        m_ref[...] = jnp.full(m_ref.shape, M_INIT, F32)

    L = MLSTM_CHUNK
    row = lax.broadcasted_iota(jnp.int32, (L, L), 0)
    col = lax.broadcasted_iota(jnp.int32, (L, L), 1)
    row_id = lax.broadcasted_iota(jnp.int32, (L, LANES), 0)
    ones_col = (lax.broadcasted_iota(jnp.int32, (L, LANES), 1) == 0).astype(BF16)
    hm = N_MLSTM_HEADS
    for d, (q_ref, k_ref, v_ref, g_ref, o_ref) in enumerate(
            ((qf_ref, kf_ref, vf_ref, gf_ref, hf_ref), (qb_ref, kb_ref, vb_ref, gb_ref, hb_ref))):
        backward = d == 1
        mask = (row <= col) if backward else (row >= col)
        g = g_ref[...] + bg_ref[...]
        lf = jax.nn.log_sigmoid(g)
        b_c = _dot_f32_by_mask(mask.astype(BF16), lf)
        to_lane0 = lambda t, first: t if first == 0 else pltpu.roll(t, LANES - first, 1)
        log_i = to_lane0(g, 2 * d * hm)
        b = to_lane0(b_c, (2 * d + 1) * hm)
        b_tot = jnp.sum(to_lane0(lf, (2 * d + 1) * hm), axis=0, keepdims=True)
        a = log_i - b
        m_prev = m_ref[d:d + 1, :]
        mx = jnp.maximum(m_prev, _scan_max_rows(a, row_id, backward))
        u2 = mx * (-LOG2_E)
        w_inter = jnp.exp(m_prev - mx)
        floor = jnp.exp(-(b + mx))
        mn = jnp.maximum(m_prev, jnp.max(a, axis=0, keepdims=True))
        decay = jnp.exp(m_prev - mn)
        m_ref[d:d + 1, :] = b_tot + mn
        a_t = a.T
        for h in range(hm):
            sl = slice(h * MLSTM_DH, (h + 1) * MLSTM_DH)
            chain = d * hm + h
            q, k_t = q_ref[:, sl], k_ref[sl, :]
            v_aug = jnp.concatenate([v_ref[:, sl], ones_col], axis=-1)
            a_row = a_t[h:h + 1, :]
            d_m = jnp.where(mask, jnp.exp2(u2[:, h:h + 1] + a_row * LOG2_E), 0.0)
            c_aug = c_ref[chain]
            scores = jnp.dot(q, k_t, preferred_element_type=F32) * d_m
            r = (jnp.dot(scores.astype(BF16), v_aug, preferred_element_type=F32)
                 + w_inter[:, h:h + 1] * jnp.dot(q, c_aug.astype(BF16), preferred_element_type=F32))
            num, den = r[:, :MLSTM_DH], r[:, MLSTM_DH:MLSTM_DH + 1]
            o_ref[:, sl] = (num / jnp.maximum(jnp.abs(den), floor[:, h:h + 1])).astype(o_ref.dtype)

            kw_t = (k_t.astype(F32) * jnp.exp(a_row - mn[:, h:h + 1])).astype(BF16)
            c_ref[chain] = decay[:, h:h + 1] * c_aug + jnp.dot(kw_t, v_aug, preferred_element_type=F32)


def _mlstm(q, k_t, proj, gates, bg):
    s = q.shape[0]
    L = MLSTM_CHUNK
    nc = s // L
    w = N_MLSTM_HEADS * MLSTM_DH
    fwd = lambda blk: (lambda i: (i, blk))
    bwd = lambda blk: (lambda i: (nc - 1 - i, blk))
    specs = lambda ix: [pl.BlockSpec((L, w), ix(0)),
                        pl.BlockSpec((w, L), lambda i: ix(0)(i)[::-1]),
                        pl.BlockSpec((L, w), ix(2)),
                        pl.BlockSpec((L, LANES), ix(0))]
    return pl.pallas_call(
        _mlstm_kernel,
        grid=(nc,),
        in_specs=specs(fwd) + specs(bwd) + [pl.BlockSpec((1, LANES), lambda i: (0, 0))],
        out_specs=[pl.BlockSpec((L, w), fwd(0)), pl.BlockSpec((L, w), bwd(0))],
        out_shape=[jax.ShapeDtypeStruct((s, w), BF16), jax.ShapeDtypeStruct((s, w), BF16)],
        scratch_shapes=[pltpu.VMEM((2 * N_MLSTM_HEADS, MLSTM_DH, MLSTM_AUG), F32),
                        pltpu.VMEM((2 * N_MLSTM_HEADS, LANES), F32)],
        compiler_params=_cparams(("arbitrary",)),
        name="mlstm",
    )(q, k_t, proj, gates, q, k_t, proj, gates, bg)


def _outproj_kernel(attn_ref, hf_ref, hb_ref, om_ref, x_ref, gm_ref, w_ref, gate_ref, g2_ref, sc_ref, sh_ref,
                    x1_ref, h2_ref, *, n_sub):
    gm = gm_ref[...]
    n_attn = attn_ref.shape[1]
    sub = x_ref.shape[0] // n_sub
    for t in range(n_sub):
        rows = slice(t * sub, (t + 1) * sub)
        hsum = hf_ref[rows, :].astype(F32) + hb_ref[rows, :].astype(F32)
        parts = []
        for h in range(N_MLSTM_HEADS):
            sl = slice(h * MLSTM_DH, (h + 1) * MLSTM_DH)
            seg = hsum[:, sl]
            parts.append(seg * _rms_scale(seg, MLSTM_DH) * gm[:, sl])
        ml = (jax.nn.sigmoid(om_ref[rows, :].astype(F32)) * jnp.concatenate(parts, axis=-1)).astype(BF16)
        mixed = (jnp.dot(attn_ref[rows, :], w_ref[:n_attn, :], preferred_element_type=F32)
                 + jnp.dot(ml, w_ref[n_attn:, :], preferred_element_type=F32))
        x1 = x_ref[rows, :] + gate_ref[...] * mixed
        x1_ref[rows, :] = x1
        h2_ref[rows, :] = _modulated_norm(x1, g2_ref[...], sc_ref[...], sh_ref[...]).astype(BF16)


def _outproj(attn, hf, hb, proj, x, gm, w_out, mod, g2, tm=512, n_sub=1):
    s, d = x.shape
    wm = hf.shape[1]
    row = lambda blk: pl.BlockSpec((1, d), lambda i: (0, blk))
    return pl.pallas_call(
        functools.partial(_outproj_kernel, n_sub=n_sub),
        grid=(s // tm,),
        in_specs=[pl.BlockSpec((tm, attn.shape[1]), lambda i: (i, 0)),
                  pl.BlockSpec((tm, wm), lambda i: (i, 0)),
                  pl.BlockSpec((tm, wm), lambda i: (i, 0)),
                  pl.BlockSpec((tm, wm), lambda i: (i, 3)),
                  pl.BlockSpec((tm, d), lambda i: (i, 0)),
                  pl.BlockSpec((1, wm), lambda i: (0, 0)),
                  pl.BlockSpec(w_out.shape, lambda i: (0, 0), pipeline_mode=pl.Buffered(1)),
                  row(2), row(0), row(4), row(3)],
        out_specs=[pl.BlockSpec((tm, d), lambda i: (i, 0)), pl.BlockSpec((tm, d), lambda i: (i, 0))],
        out_shape=[jax.ShapeDtypeStruct((s, d), F32), jax.ShapeDtypeStruct((s, d), BF16)],
        compiler_params=_cparams(("arbitrary",)),
        name="outproj",
    )(attn, hf, hb, proj, x, gm, w_out, mod, g2, mod, mod)


def _ffn_kernel(h2_ref, w1_ref, w2_ref, x1_hbm, gate_ref, o_ref, x1_buf, x1_sem, *, f_last):
    m, f = pl.program_id(0), pl.program_id(1)
    tm = o_ref.shape[0]
    x1_copy = pltpu.make_async_copy(x1_hbm.at[pl.ds(pl.multiple_of(m * tm, tm), tm), :], x1_buf, x1_sem)

    @pl.when(f == 0)
    def _():
        x1_copy.start()
        o_ref[...] = jnp.zeros(o_ref.shape, F32)

    a = jnp.maximum(jnp.dot(h2_ref[...], w1_ref[...], preferred_element_type=F32), 0.0)
    o_ref[...] += jnp.dot((a * a).astype(BF16), w2_ref[...], preferred_element_type=F32)

    @pl.when(f == f_last)
    def _():
        x1_copy.wait()
        o_ref[...] = x1_buf[...] + gate_ref[...] * o_ref[...]


def _ffn(h2, w1, w2, x1, mod, tm=1024, tf=1024):
    s, d = h2.shape
    dff = w1.shape[1]
    return pl.pallas_call(
        functools.partial(_ffn_kernel, f_last=dff // tf - 1),
        grid=(s // tm, dff // tf),
        in_specs=[pl.BlockSpec((tm, d), lambda m, f: (m, 0)),
                  pl.BlockSpec((d, tf), lambda m, f: (0, f)),
                  pl.BlockSpec((tf, d), lambda m, f: (f, 0)),
                  pl.BlockSpec(memory_space=pl.ANY),
                  pl.BlockSpec((1, d), lambda m, f: (0, 5))],
        out_specs=pl.BlockSpec((tm, d), lambda m, f: (m, 0)),
        out_shape=jax.ShapeDtypeStruct((s, d), F32),
        scratch_shapes=[pltpu.VMEM((tm, d), F32), pltpu.SemaphoreType.DMA(())],
        compiler_params=_cparams(("arbitrary", "arbitrary")),
        name="ffn",
    )(h2, w1, w2, x1, mod)


PACK_TILE = 512
PACK_SRC = 64
PACK_NSRC = PACK_TILE // PACK_SRC
W_IN_MIX_START = 832
W_IN_GATES_START = 4928
N_GATE_COLS = 16


def _pack_src_blocks():
    rows = []
    n_mix = 4 * N_MLSTM_HEADS * MLSTM_DH // PACK_TILE
    for n in range(n_mix):
        first = (W_IN_MIX_START + n * PACK_TILE) // PACK_SRC
        rows.append([first + i for i in range(PACK_NSRC)])
    rows.append(list(range(PACK_NSRC)))
    gates_blk = W_IN_GATES_START // PACK_SRC
    rows.append([8, 9, 10, 11, 12, 12, gates_blk, gates_blk])
    return jnp.asarray(rows, jnp.int32).reshape(-1)


def _pack_kernel(tbl_ref, *refs, n_mix):
    del tbl_ref
    srcs, o_ref = refs[:-1], refs[-1]
    n = pl.program_id(0)
    blk = lambda i: slice(i * PACK_SRC, (i + 1) * PACK_SRC)

    @pl.when(n <= n_mix)
    def _():
        for i, b in enumerate(srcs):
            o_ref[blk(i), :] = b[...].astype(BF16)

    @pl.when(n == n_mix + 1)
    def _():
        zeros = jnp.zeros(srcs[0].shape, BF16)
        for i in range(5):
            o_ref[blk(i), :] = srcs[i][...].astype(BF16)
        o_ref[blk(5), :] = zeros
        rows = lax.broadcasted_iota(jnp.int32, srcs[6].shape, 0)
        o_ref[blk(6), :] = jnp.where(rows < N_GATE_COLS, srcs[6][...], 0.0).astype(BF16)
        o_ref[blk(7), :] = zeros


def _pack_w_in(w_in_t):
    d = w_in_t.shape[1]
    tbl = _pack_src_blocks()
    n_tiles = tbl.shape[0] // PACK_NSRC
    src = lambda i: pl.BlockSpec((PACK_SRC, d), lambda n, t: (t[n * PACK_NSRC + i], 0))
    return pl.pallas_call(
        functools.partial(_pack_kernel, n_mix=n_tiles - 2),
        grid_spec=pltpu.PrefetchScalarGridSpec(
            num_scalar_prefetch=1, grid=(n_tiles,),
            in_specs=[src(i) for i in range(PACK_NSRC)],
            out_specs=pl.BlockSpec((PACK_TILE, d), lambda n, t: (n, 0))),
        out_shape=jax.ShapeDtypeStruct((n_tiles * PACK_TILE, d), BF16),
        compiler_params=_cparams(("arbitrary",)),
        name="packw",
    )(tbl, *([w_in_t] * PACK_NSRC))


def _pad_lanes(v, n):
    return jnp.pad(v, ((0, 0), (0, n - v.shape[1])))


def kernel(x, c, positions, w_ada, b_ada, norm_mix_g, w_in, b_gates, conv_w, conv_b, q_lora_g, w_uq,
           kv_lora_g, w_ukv, q_norm_g, k_norm_g, mlstm_norm_g, w_out, norm_mlp_g, w_ff1, w_ff2):
    bsz, s, d = x.shape
    assert bsz == 1, "kernels are written for a single sequence"
    xs = x[0]
    pos_row = positions.reshape(1, s)
    half = jnp.arange(ROPE_HALF, dtype=F32)
    freq = (ROPE_THETA ** (-half / ROPE_HALF)).reshape(ROPE_HALF, 1)
    row = lambda v: v.reshape(1, -1).astype(F32)

    for l in range(w_ada.shape[0]):
        mod = _mod(c.reshape(d, 1), w_ada[l], row(b_ada[l]))
        proj, gates = _inproj(xs, row(norm_mix_g[l]), mod, _pack_w_in(w_in[l].T))

        wuq_t = jnp.pad(w_uq[l].reshape(-1, N_MLA_HEADS, MLA_QK), ((0, 0), (0, 0), (0, MLA_QK_PAD - MLA_QK)))
        wuq_t = wuq_t.reshape(-1, N_MLA_HEADS * MLA_QK_PAD).T.astype(BF16)
        wkv = w_ukv[l].reshape(-1, N_MLA_HEADS, MLA_NOPE + MLA_V)
        wk = wkv[:, :, :MLA_NOPE].reshape(-1, N_MLA_HEADS * MLA_NOPE).astype(BF16)
        wv_t = wkv[:, :, MLA_NOPE:].reshape(-1, N_MLA_HEADS * MLA_V).T.astype(BF16)
        g_q, g_k = row(q_norm_g[l]), row(k_norm_g[l])
        q_t, k, v_t = _mlaprep(proj, pos_row, freq, row(q_lora_g[l]), row(kv_lora_g[l]), wuq_t, wk, wv_t,
                               g_q.reshape(-1, 1), _pad_lanes(g_k, MLA_QK_PAD))
        logit_bound = 1.02 * LOG2_E * MLA_QK ** 0.5 * jnp.max(jnp.abs(g_q)) * jnp.max(jnp.abs(g_k))
        attn, (w_out_b, w_ff1_b, w_ff2_b) = lax.cond(
            logit_bound <= SAFE_LOG2,
            functools.partial(_attention, shifted=False), functools.partial(_attention, shifted=True),
            q_t, k, v_t, (w_out[l], w_ff1[l], w_ff2[l]))

        n_qk_tiles = N_MLSTM_HEADS * MLSTM_DH // LANES
        q_m = _conv(proj, conv_w[l], row(conv_b[l]), first_tile=0, n_tiles=n_qk_tiles, out_scale=1.0,
                    transpose_out=False)
        k_m_t = _conv(proj, conv_w[l], row(conv_b[l]), first_tile=n_qk_tiles, n_tiles=n_qk_tiles,
                      out_scale=MLSTM_DH ** -0.5, transpose_out=True)
        hf, hb = _mlstm(q_m, k_m_t, proj, gates, _pad_lanes(row(b_gates[l]), LANES))

        x1, h2 = _outproj(attn, hf, hb, proj, xs, row(mlstm_norm_g[l]), w_out_b, mod, row(norm_mlp_g[l]))
        xs = _ffn(h2, w_ff1_b, w_ff2_b, x1, mod)
    return xs[None]
```

```python
import functools

import jax
import jax.numpy as jnp
from jax import lax
from jax.experimental import pallas as pl
from jax.experimental.pallas import tpu as pltpu

F32 = jnp.float32
BF16 = jnp.bfloat16

LANES = 128
N_MLA_HEADS = 8
MLA_NOPE = 128
ROPE_DIM = 64
ROPE_HALF = ROPE_DIM // 2
MLA_QK = MLA_NOPE + ROPE_DIM
MLA_QK_PAD = 256
MLA_V = 128
ATTN_TK = 256
NORM_ROWS = 128
ROPE_THETA = 10000.0
N_MLSTM_HEADS = 4
MLSTM_DH = 256
MLSTM_CHUNK = 256
MLSTM_AUG = MLSTM_DH + LANES
CONV_WIDTH = 5
CONV_HALO = 8
EPS = 1e-6
M_INIT = -1e30
LOG2_E = 1.4426950408889634
SAFE_LOG2 = 60.0
VMEM_LIMIT = 56 * 1024 * 1024


def _cparams(sem):
    return pltpu.CompilerParams(dimension_semantics=sem, vmem_limit_bytes=VMEM_LIMIT)


def _mod_kernel(c_ref, w_ref, b_ref, o_ref, sb_ref, *, tn):
    @pl.when(pl.program_id(0) == 0)
    def _():
        cc = c_ref[...]
        sb_ref[...] = jnp.broadcast_to(cc * jax.nn.sigmoid(cc), sb_ref.shape)

    sb = sb_ref[...]
    for j in range(tn // LANES):
        sl = slice(j * LANES, (j + 1) * LANES)
        o_ref[:, sl] = jnp.sum(w_ref[:, sl] * sb, axis=0, keepdims=True) + b_ref[:, sl]


def _mod(c_col, w_ada, b_ada, tn=1024):
    d, n = w_ada.shape
    return pl.pallas_call(
        functools.partial(_mod_kernel, tn=tn),
        grid=(n // tn,),
        in_specs=[pl.BlockSpec((d, 1), lambda j: (0, 0)),
                  pl.BlockSpec((d, tn), lambda j: (0, j)),
                  pl.BlockSpec((1, tn), lambda j: (0, j))],
        out_specs=pl.BlockSpec((1, tn), lambda j: (0, j)),
        out_shape=jax.ShapeDtypeStruct((1, n), F32),
        scratch_shapes=[pltpu.VMEM((d, LANES), F32)],
        compiler_params=_cparams(("arbitrary",)),
        name="mod",
    )(c_col, w_ada, b_ada)


def _modulated_norm(x, g, scale, shift):
    ms = jnp.mean(x * x, axis=-1, keepdims=True)
    return (x * lax.rsqrt(ms + EPS) * g) * (1.0 + scale) + shift


def _inproj_kernel(x_ref, g_ref, sc_ref, sh_ref, w_ref, o_ref, gate_ref, h_ref, *, n_last):
    n = pl.program_id(1)
    project = lambda h: lax.dot_general(h, w_ref[...], (((1,), (1,)), ((), ())), preferred_element_type=F32)

    @pl.when(n == 0)
    def _():
        h = _modulated_norm(x_ref[...], g_ref[...], sc_ref[...], sh_ref[...]).astype(BF16)
        h_ref[...] = h
        o_ref[...] = project(h).astype(BF16)

    @pl.when(n > 0)
    def _():
        acc = project(h_ref[...])
        o_ref[...] = acc.astype(BF16)

        @pl.when(n == n_last)
        def _():
            gate_ref[...] = acc[:, -LANES:]


def _inproj(x, g, mod, w_t, tm=1024, tn=1024):
    s, d = x.shape
    n_tot = w_t.shape[0]
    return pl.pallas_call(
        functools.partial(_inproj_kernel, n_last=n_tot // tn - 1),
        grid=(s // tm, n_tot // tn),
        in_specs=[pl.BlockSpec((tm, d), lambda m, n: (m, 0)),
                  pl.BlockSpec((1, d), lambda m, n: (0, 0)),
                  pl.BlockSpec((1, d), lambda m, n: (0, 1)),
                  pl.BlockSpec((1, d), lambda m, n: (0, 0)),
                  pl.BlockSpec((tn, d), lambda m, n: (n, 0))],
        out_specs=[pl.BlockSpec((tm, tn), lambda m, n: (m, n)),
                   pl.BlockSpec((tm, LANES), lambda m, n: (m, 0))],
        out_shape=[jax.ShapeDtypeStruct((s, n_tot), BF16),
                   jax.ShapeDtypeStruct((s, LANES), F32)],
        scratch_shapes=[pltpu.VMEM((tm, d), BF16)],
        compiler_params=_cparams(("arbitrary", "arbitrary")),
        name="inproj",
    )(x, g, mod, mod, w_t)


def _rms_scale(x, n):
    return lax.rsqrt(jnp.sum(x * x, axis=-1, keepdims=True) * (1.0 / n) + EPS)


def _mlaprep_kernel(cq_ref, ckv_ref, kpe_ref, pos_ref, freq_ref, gql_ref, gkvl_ref, wuqt_ref, wk_ref, wvt_ref,
                    gqc_ref, gk_ref, q_ref, k_ref, v_ref):
    tm = cq_ref.shape[0]
    nt = (((1,), (1,)), ((), ()))
    cq = cq_ref[...].astype(F32)
    cqn = (cq * _rms_scale(cq, cq.shape[-1]) * gql_ref[...]).astype(BF16)
    ckv = ckv_ref[...].astype(F32)
    ckvn = (ckv * _rms_scale(ckv, ckv.shape[-1]) * gkvl_ref[...]).astype(BF16)
    qf_t = lax.dot_general(wuqt_ref[...], cqn, nt, preferred_element_type=F32)
    kf = jnp.dot(ckvn, wk_ref[...], preferred_element_type=F32)
    vf_t = lax.dot_general(wvt_ref[...], ckvn, nt, preferred_element_type=F32)

    ang_t = freq_ref[...] * pos_ref[...].astype(F32)
    cos_t, sin_t = jnp.cos(ang_t), jnp.sin(ang_t)

    gq_b = jnp.broadcast_to(gqc_ref[...] * (LOG2_E * MLA_QK ** -0.5), (MLA_QK, tm))
    q_pad = jnp.zeros((MLA_QK_PAD - MLA_QK, tm), F32)
    for h in range(N_MLA_HEADS):
        qh = qf_t[h * MLA_QK_PAD:h * MLA_QK_PAD + MLA_QK]
        r = lax.rsqrt(jnp.sum(qh * qh, axis=0, keepdims=True) * (1.0 / MLA_QK) + EPS)
        qn = qh * r * gq_b
        x1, x2 = qn[MLA_NOPE:MLA_NOPE + ROPE_HALF], qn[MLA_NOPE + ROPE_HALF:]
        q_ref[h] = jnp.concatenate([qn[:MLA_NOPE], x1 * cos_t - x2 * sin_t, x1 * sin_t + x2 * cos_t, q_pad],
                                   axis=0).astype(BF16)

    for h in range(N_MLA_HEADS):
        for c in range(tm // ATTN_TK):
            v_ref[h, c] = vf_t[h * MLA_V:(h + 1) * MLA_V, c * ATTN_TK:(c + 1) * ATTN_TK].astype(BF16)

    z_half, z_pad = jnp.zeros_like(cos_t), jnp.zeros((LANES - ROPE_DIM, tm), F32)
    c_tab = jnp.concatenate([cos_t, cos_t, z_pad], axis=0).T
    s_up = jnp.concatenate([z_half, sin_t, z_pad], axis=0).T
    s_dn = jnp.concatenate([-sin_t, z_half, z_pad], axis=0).T
    gk = gk_ref[...]
    kpe = kpe_ref[...].astype(F32)
    kpe_g = kpe * gk[:, MLA_NOPE:]
    kpe_rot = (kpe_g * c_tab + pltpu.roll(kpe_g, ROPE_HALF, 1) * s_up
               + pltpu.roll(kpe_g, LANES - ROPE_HALF, 1) * s_dn)
    kpe_ssq = jnp.sum(kpe * kpe, axis=-1, keepdims=True)
    for h in range(N_MLA_HEADS):
        kn = kf[:, h * MLA_NOPE:(h + 1) * MLA_NOPE]
        r = lax.rsqrt((jnp.sum(kn * kn, axis=-1, keepdims=True) + kpe_ssq) * (1.0 / MLA_QK) + EPS)
        k_ref[h] = jnp.concatenate([kn * r * gk[:, :MLA_NOPE], kpe_rot * r], axis=-1).astype(BF16)


def _mlaprep(proj, pos_row, freq, gql, gkvl, wuq_t, wk, wv_t, gq_col, gk, tm=512):
    s = proj.shape[0]
    hq = N_MLA_HEADS
    full = lambda a: pl.BlockSpec(a.shape, lambda i: (0,) * a.ndim)
    return pl.pallas_call(
        _mlaprep_kernel,
        grid=(s // tm,),
        in_specs=[pl.BlockSpec((tm, 512), lambda i: (i, 8)),
                  pl.BlockSpec((tm, 256), lambda i: (i, 18)),
                  pl.BlockSpec((tm, LANES), lambda i: (i, 38)),
                  pl.BlockSpec((1, tm), lambda i: (0, i)),
                  full(freq), full(gql), full(gkvl), full(wuq_t), full(wk), full(wv_t), full(gq_col), full(gk)],
        out_specs=[pl.BlockSpec((hq, MLA_QK_PAD, tm), lambda i: (0, 0, i)),
                   pl.BlockSpec((hq, tm, MLA_QK_PAD), lambda i: (0, i, 0)),
                   pl.BlockSpec((hq, tm // ATTN_TK, MLA_V, ATTN_TK), lambda i: (0, i, 0, 0))],
        out_shape=[jax.ShapeDtypeStruct((hq, MLA_QK_PAD, s), BF16),
                   jax.ShapeDtypeStruct((hq, s, MLA_QK_PAD), BF16),
                   jax.ShapeDtypeStruct((hq, s // ATTN_TK, MLA_V, ATTN_TK), BF16)],
        compiler_params=_cparams(("arbitrary",)),
        name="mlaprep",
    )(proj, proj, proj, pos_row, freq, gql, gkvl, wuq_t, wk, wv_t, gq_col, gk)


def _attn_kernel(qt_ref, k_ref, vt_ref, *refs, n_side, shifted, tk, ahead):
    side_in, o_ref, side_out = refs[:n_side], refs[n_side], refs[n_side + 1:]
    for src, dst in zip(side_in, side_out):
        dst[...] = src[...].astype(BF16)

    qt = qt_ref[0]
    tq = qt.shape[1]
    n_chunks = k_ref.shape[1] // tk

    def logits_t(j):
        return jnp.dot(k_ref[0, j * tk:(j + 1) * tk, :], qt, preferred_element_type=F32)

    def values_t(j, p_t):
        return jnp.dot(vt_ref[0, j], p_t.astype(BF16), preferred_element_type=F32)

    m = jnp.full((1, tq), -jnp.inf, F32)
    acc = l = None
    pending = [logits_t(j) for j in range(ahead)]
    for j in range(n_chunks):
        if j + ahead < n_chunks:
            pending.append(logits_t(j + ahead))
        s2 = pending.pop(0)
        if shifted:
            m_new = jnp.maximum(m, jnp.max(s2, axis=0, keepdims=True))
            p_t, alpha = jnp.exp2(s2 - m_new), jnp.exp2(m - m_new)
            pv, ps = values_t(j, p_t), jnp.sum(p_t, axis=0, keepdims=True)
            acc, l = (pv, ps) if acc is None else (acc * alpha + pv, l * alpha + ps)
            m = m_new
        else:
            p_t = jnp.exp2(s2)
            pv, ps = values_t(j, p_t), jnp.sum(p_t, axis=0, keepdims=True)
            acc, l = (pv, ps) if acc is None else (acc + pv, l + ps)
    o_ref[...] = (acc / l).T.astype(BF16)


def _attention(q_t, k, v_t, side_weights, *, shifted, tq=512, tk=ATTN_TK, ahead=2):
    hq, s, _ = k.shape
    n_q = s // tq
    n_steps = hq * n_q
    slab = lambda w: pl.BlockSpec((w.shape[0] // n_steps, w.shape[1]), lambda h, i: (h * n_q + i, 0))
    outs = pl.pallas_call(
        functools.partial(_attn_kernel, n_side=len(side_weights), shifted=shifted, tk=tk, ahead=ahead),
        grid=(hq, n_q),
        in_specs=[pl.BlockSpec((1, MLA_QK_PAD, tq), lambda h, i: (h, 0, i)),
                  pl.BlockSpec((1, s, MLA_QK_PAD), lambda h, i: (h, 0, 0)),
                  pl.BlockSpec((1, s // tk, MLA_V, tk), lambda h, i: (h, 0, 0, 0))]
                 + [slab(w) for w in side_weights],
        out_specs=[pl.BlockSpec((tq, MLA_V), lambda h, i: (i, h))] + [slab(w) for w in side_weights],
        out_shape=[jax.ShapeDtypeStruct((s, hq * MLA_V), BF16)]
                  + [jax.ShapeDtypeStruct(w.shape, BF16) for w in side_weights],
        compiler_params=_cparams(("arbitrary", "arbitrary")),
        name="attn_shifted" if shifted else "attn",
    )(q_t, k, v_t, *side_weights)
    return outs[0], tuple(outs[1:])


def _conv_kernel(x_ref, w_ref, b_ref, o_ref, pad_ref, *, rows, out_scale, transpose_out):
    s = x_ref.shape[0]
    zeros = jnp.zeros((CONV_HALO, LANES), F32)
    pad_ref[0:CONV_HALO, :] = zeros
    pad_ref[CONV_HALO + s:CONV_HALO + s + CONV_HALO, :] = zeros
    pad_ref[CONV_HALO:CONV_HALO + s, :] = x_ref[...].astype(F32)
    w = w_ref[...]
    b = b_ref[...]
    for r in range(s // rows):
        base = r * rows
        acc = b
        for j in range(CONV_WIDTH):
            off = CONV_HALO + j - CONV_WIDTH // 2
            acc = acc + w[j:j + 1, :] * pad_ref[base + off:base + off + rows, :]
        y = acc * jax.nn.sigmoid(acc) * out_scale
        if transpose_out:
            o_ref[:, base:base + rows] = y.T.astype(BF16)
        else:
            o_ref[base:base + rows, :] = y.astype(BF16)


def _conv(proj, conv_w, conv_b, *, first_tile, n_tiles, out_scale, transpose_out, rows=256):
    s = proj.shape[0]
    n_ch = n_tiles * LANES
    return pl.pallas_call(
        functools.partial(_conv_kernel, rows=rows, out_scale=out_scale, transpose_out=transpose_out),
        grid=(n_tiles,),
        in_specs=[pl.BlockSpec((s, LANES), lambda j: (0, first_tile + j)),
                  pl.BlockSpec((CONV_WIDTH, LANES), lambda j: (0, first_tile + j)),
                  pl.BlockSpec((1, LANES), lambda j: (0, first_tile + j))],
        out_specs=pl.BlockSpec((LANES, s), lambda j: (j, 0)) if transpose_out
                  else pl.BlockSpec((s, LANES), lambda j: (0, j)),
        out_shape=jax.ShapeDtypeStruct((n_ch, s) if transpose_out else (s, n_ch), BF16),
        scratch_shapes=[pltpu.VMEM((s + 2 * CONV_HALO, LANES), F32)],
        compiler_params=_cparams(("arbitrary",)),
        name="conv_k" if transpose_out else "conv_q",
    )(proj, conv_w, conv_b)


def _dot_f32_by_mask(mask01, x):
    hi = x.astype(BF16)
    rest = x - hi.astype(F32)
    mid = rest.astype(BF16)
    lo = (rest - mid.astype(F32)).astype(BF16)
    return (jnp.dot(mask01, hi, preferred_element_type=F32) + jnp.dot(mask01, mid, preferred_element_type=F32)
            + jnp.dot(mask01, lo, preferred_element_type=F32))


def _scan_max_rows(x, row_id, reverse):
    n, s = x.shape[0], 1
    while s < n:
        shifted = pltpu.roll(x, n - s if reverse else s, 0)
        ok = (row_id < n - s) if reverse else (row_id >= s)
        x = jnp.maximum(x, jnp.where(ok, shifted, -jnp.inf))
        s *= 2
    return x


def _mlstm_kernel(qf_ref, kf_ref, vf_ref, gf_ref, qb_ref, kb_ref, vb_ref, gb_ref, bg_ref,
                  hf_ref, hb_ref, c_ref, m_ref):
    @pl.when(pl.program_id(0) == 0)
    def _():
        c_ref[...] = jnp.zeros(c_ref.shape, F32)
        m_ref[...] = jnp.full(m_ref.shape, M_INIT, F32)

    L = MLSTM_CHUNK
    row = lax.broadcasted_iota(jnp.int32, (L, L), 0)
    col = lax.broadcasted_iota(jnp.int32, (L, L), 1)
    row_id = lax.broadcasted_iota(jnp.int32, (L, LANES), 0)
    ones_col = (lax.broadcasted_iota(jnp.int32, (L, LANES), 1) == 0).astype(BF16)
    hm = N_MLSTM_HEADS
    for d, (q_ref, k_ref, v_ref, g_ref, o_ref) in enumerate(
            ((qf_ref, kf_ref, vf_ref, gf_ref, hf_ref), (qb_ref, kb_ref, vb_ref, gb_ref, hb_ref))):
        backward = d == 1
        mask = (row <= col) if backward else (row >= col)
        g = g_ref[...] + bg_ref[...]
        lf = jax.nn.log_sigmoid(g)
        b_c = _dot_f32_by_mask(mask.astype(BF16), lf)
        to_lane0 = lambda t, first: t if first == 0 else pltpu.roll(t, LANES - first, 1)
        log_i = to_lane0(g, 2 * d * hm)
        b = to_lane0(b_c, (2 * d + 1) * hm)
        b_tot = jnp.sum(to_lane0(lf, (2 * d + 1) * hm), axis=0, keepdims=True)
        a = log_i - b
        m_prev = m_ref[d:d + 1, :]
        mx = jnp.maximum(m_prev, _scan_max_rows(a, row_id, backward))
        u2 = mx * (-LOG2_E)
        w_inter = jnp.exp(m_prev - mx)
        floor = jnp.exp(-(b + mx))
        mn = jnp.maximum(m_prev, jnp.max(a, axis=0, keepdims=True))
        decay = jnp.exp(m_prev - mn)
        m_ref[d:d + 1, :] = b_tot + mn
        a_t = a.T
        for h in range(hm):
            sl = slice(h * MLSTM_DH, (h + 1) * MLSTM_DH)
            chain = d * hm + h
            q, k_t = q_ref[:, sl], k_ref[sl, :]
            v_aug = jnp.concatenate([v_ref[:, sl], ones_col], axis=-1)
            a_row = a_t[h:h + 1, :]
            d_m = jnp.where(mask, jnp.exp2(u2[:, h:h + 1] + a_row * LOG2_E), 0.0)
            c_aug = c_ref[chain]
            scores = jnp.dot(q, k_t, preferred_element_type=F32) * d_m
            r = (jnp.dot(scores.astype(BF16), v_aug, preferred_element_type=F32)
                 + w_inter[:, h:h + 1] * jnp.dot(q, c_aug.astype(BF16), preferred_element_type=F32))
            num, den = r[:, :MLSTM_DH], r[:, MLSTM_DH:MLSTM_DH + 1]
            o_ref[:, sl] = (num / jnp.maximum(jnp.abs(den), floor[:, h:h + 1])).astype(o_ref.dtype)

            kw_t = (k_t.astype(F32) * jnp.exp(a_row - mn[:, h:h + 1])).astype(BF16)
            c_ref[chain] = decay[:, h:h + 1] * c_aug + jnp.dot(kw_t, v_aug, preferred_element_type=F32)


def _mlstm(q, k_t, proj, gates, bg):
    s = q.shape[0]
    L = MLSTM_CHUNK
    nc = s // L
    w = N_MLSTM_HEADS * MLSTM_DH
    fwd = lambda blk: (lambda i: (i, blk))
    bwd = lambda blk: (lambda i: (nc - 1 - i, blk))
    specs = lambda ix: [pl.BlockSpec((L, w), ix(0)),
                        pl.BlockSpec((w, L), lambda i: ix(0)(i)[::-1]),
                        pl.BlockSpec((L, w), ix(2)),
                        pl.BlockSpec((L, LANES), ix(0))]
    return pl.pallas_call(
        _mlstm_kernel,
        grid=(nc,),
        in_specs=specs(fwd) + specs(bwd) + [pl.BlockSpec((1, LANES), lambda i: (0, 0))],
        out_specs=[pl.BlockSpec((L, w), fwd(0)), pl.BlockSpec((L, w), bwd(0))],
        out_shape=[jax.ShapeDtypeStruct((s, w), BF16), jax.ShapeDtypeStruct((s, w), BF16)],
        scratch_shapes=[pltpu.VMEM((2 * N_MLSTM_HEADS, MLSTM_DH, MLSTM_AUG), F32),
                        pltpu.VMEM((2 * N_MLSTM_HEADS, LANES), F32)],
        compiler_params=_cparams(("arbitrary",)),
        name="mlstm",
    )(q, k_t, proj, gates, q, k_t, proj, gates, bg)


def _outproj_kernel(attn_ref, hf_ref, hb_ref, om_ref, x_ref, gm_ref, w_ref, gate_ref, g2_ref, sc_ref, sh_ref,
                    x1_ref, h2_ref, *, n_sub):
    gm = gm_ref[...]
    n_attn = attn_ref.shape[1]
    sub = x_ref.shape[0] // n_sub
    for t in range(n_sub):
        rows = slice(t * sub, (t + 1) * sub)
        hsum = hf_ref[rows, :].astype(F32) + hb_ref[rows, :].astype(F32)
        parts = []
        for h in range(N_MLSTM_HEADS):
            sl = slice(h * MLSTM_DH, (h + 1) * MLSTM_DH)
            seg = hsum[:, sl]
            parts.append(seg * _rms_scale(seg, MLSTM_DH) * gm[:, sl])
        ml = (jax.nn.sigmoid(om_ref[rows, :].astype(F32)) * jnp.concatenate(parts, axis=-1)).astype(BF16)
        mixed = (jnp.dot(attn_ref[rows, :], w_ref[:n_attn, :], preferred_element_type=F32)
                 + jnp.dot(ml, w_ref[n_attn:, :], preferred_element_type=F32))
        x1 = x_ref[rows, :] + gate_ref[...] * mixed
        x1_ref[rows, :] = x1
        h2_ref[rows, :] = _modulated_norm(x1, g2_ref[...], sc_ref[...], sh_ref[...]).astype(BF16)


def _outproj(attn, hf, hb, proj, x, gm, w_out, mod, g2, tm=512, n_sub=1):
    s, d = x.shape
    wm = hf.shape[1]
    row = lambda blk: pl.BlockSpec((1, d), lambda i: (0, blk))
    return pl.pallas_call(
        functools.partial(_outproj_kernel, n_sub=n_sub),
        grid=(s // tm,),
        in_specs=[pl.BlockSpec((tm, attn.shape[1]), lambda i: (i, 0)),
                  pl.BlockSpec((tm, wm), lambda i: (i, 0)),
                  pl.BlockSpec((tm, wm), lambda i: (i, 0)),
                  pl.BlockSpec((tm, wm), lambda i: (i, 3)),
                  pl.BlockSpec((tm, d), lambda i: (i, 0)),
                  pl.BlockSpec((1, wm), lambda i: (0, 0)),
                  pl.BlockSpec(w_out.shape, lambda i: (0, 0), pipeline_mode=pl.Buffered(1)),
                  row(2), row(0), row(4), row(3)],
        out_specs=[pl.BlockSpec((tm, d), lambda i: (i, 0)), pl.BlockSpec((tm, d), lambda i: (i, 0))],
        out_shape=[jax.ShapeDtypeStruct((s, d), F32), jax.ShapeDtypeStruct((s, d), BF16)],
        compiler_params=_cparams(("arbitrary",)),
        name="outproj",
    )(attn, hf, hb, proj, x, gm, w_out, mod, g2, mod, mod)


def _ffn_kernel(h2_ref, w1_ref, w2_ref, x1_hbm, gate_ref, o_ref, x1_buf, x1_sem, *, f_last):
    m, f = pl.program_id(0), pl.program_id(1)
    tm = o_ref.shape[0]
    x1_copy = pltpu.make_async_copy(x1_hbm.at[pl.ds(pl.multiple_of(m * tm, tm), tm), :], x1_buf, x1_sem)

    def hidden_tile():
        a = jnp.maximum(jnp.dot(h2_ref[...], w1_ref[...], preferred_element_type=F32), 0.0)
        return jnp.dot((a * a).astype(BF16), w2_ref[...], preferred_element_type=F32)

    @pl.when(f == 0)
    def _():
        x1_copy.start()
        o_ref[...] = hidden_tile()

    @pl.when(f > 0)
    def _():
        o_ref[...] += hidden_tile()

    @pl.when(f == f_last)
    def _():
        x1_copy.wait()
        o_ref[...] = x1_buf[...] + gate_ref[...] * o_ref[...]


def _ffn(h2, w1, w2, x1, mod, tm=1024, tf=1024):
    s, d = h2.shape
    dff = w1.shape[1]
    return pl.pallas_call(
        functools.partial(_ffn_kernel, f_last=dff // tf - 1),
        grid=(s // tm, dff // tf),
        in_specs=[pl.BlockSpec((tm, d), lambda m, f: (m, 0)),
                  pl.BlockSpec((d, tf), lambda m, f: (0, f)),
                  pl.BlockSpec((tf, d), lambda m, f: (f, 0)),
                  pl.BlockSpec(memory_space=pl.ANY),
                  pl.BlockSpec((1, d), lambda m, f: (0, 5))],
        out_specs=pl.BlockSpec((tm, d), lambda m, f: (m, 0)),
        out_shape=jax.ShapeDtypeStruct((s, d), F32),
        scratch_shapes=[pltpu.VMEM((tm, d), F32), pltpu.SemaphoreType.DMA(())],
        compiler_params=_cparams(("arbitrary", "arbitrary")),
        name="ffn",
    )(h2, w1, w2, x1, mod)


PACK_TILE = 512
PACK_SRC = 64
PACK_NSRC = PACK_TILE // PACK_SRC
W_IN_MIX_START = 832
W_IN_GATES_START = 4928
N_GATE_COLS = 16


def _pack_src_blocks():
    rows = []
    n_mix = 4 * N_MLSTM_HEADS * MLSTM_DH // PACK_TILE
    for n in range(n_mix):
        first = (W_IN_MIX_START + n * PACK_TILE) // PACK_SRC
        rows.append([first + i for i in range(PACK_NSRC)])
    rows.append(list(range(PACK_NSRC)))
    gates_blk = W_IN_GATES_START // PACK_SRC
    rows.append([8, 9, 10, 11, 12, 12, gates_blk, gates_blk])
    return jnp.asarray(rows, jnp.int32).reshape(-1)


def _pack_kernel(tbl_ref, *refs, n_mix):
    del tbl_ref
    srcs, o_ref = refs[:-1], refs[-1]
    n = pl.program_id(0)
    blk = lambda i: slice(i * PACK_SRC, (i + 1) * PACK_SRC)

    @pl.when(n <= n_mix)
    def _():
        for i, b in enumerate(srcs):
            o_ref[blk(i), :] = b[...].astype(BF16)

    @pl.when(n == n_mix + 1)
    def _():
        zeros = jnp.zeros(srcs[0].shape, BF16)
        for i in range(5):
            o_ref[blk(i), :] = srcs[i][...].astype(BF16)
        o_ref[blk(5), :] = zeros
        rows = lax.broadcasted_iota(jnp.int32, srcs[6].shape, 0)
        o_ref[blk(6), :] = jnp.where(rows < N_GATE_COLS, srcs[6][...], 0.0).astype(BF16)
        o_ref[blk(7), :] = zeros


def _pack_w_in(w_in_t):
    d = w_in_t.shape[1]
    tbl = _pack_src_blocks()
    n_tiles = tbl.shape[0] // PACK_NSRC
    src = lambda i: pl.BlockSpec((PACK_SRC, d), lambda n, t: (t[n * PACK_NSRC + i], 0))
    return pl.pallas_call(
        functools.partial(_pack_kernel, n_mix=n_tiles - 2),
        grid_spec=pltpu.PrefetchScalarGridSpec(
            num_scalar_prefetch=1, grid=(n_tiles,),
            in_specs=[src(i) for i in range(PACK_NSRC)],
            out_specs=pl.BlockSpec((PACK_TILE, d), lambda n, t: (n, 0))),
        out_shape=jax.ShapeDtypeStruct((n_tiles * PACK_TILE, d), BF16),
        compiler_params=_cparams(("arbitrary",)),
        name="packw",
    )(tbl, *([w_in_t] * PACK_NSRC))


def _pad_lanes(v, n):
    return jnp.pad(v, ((0, 0), (0, n - v.shape[1])))


def kernel(x, c, positions, w_ada, b_ada, norm_mix_g, w_in, b_gates, conv_w, conv_b, q_lora_g, w_uq,
           kv_lora_g, w_ukv, q_norm_g, k_norm_g, mlstm_norm_g, w_out, norm_mlp_g, w_ff1, w_ff2):
    bsz, s, d = x.shape
    assert bsz == 1, "kernels are written for a single sequence"
    xs = x[0]
    pos_row = positions.reshape(1, s)
    half = jnp.arange(ROPE_HALF, dtype=F32)
    freq = (ROPE_THETA ** (-half / ROPE_HALF)).reshape(ROPE_HALF, 1)
    row = lambda v: v.reshape(1, -1).astype(F32)

    for l in range(w_ada.shape[0]):
        mod = _mod(c.reshape(d, 1), w_ada[l], row(b_ada[l]))
        proj, gates = _inproj(xs, row(norm_mix_g[l]), mod, _pack_w_in(w_in[l].T))

        wuq_t = jnp.pad(w_uq[l].reshape(-1, N_MLA_HEADS, MLA_QK), ((0, 0), (0, 0), (0, MLA_QK_PAD - MLA_QK)))
        wuq_t = wuq_t.reshape(-1, N_MLA_HEADS * MLA_QK_PAD).T.astype(BF16)
        wkv = w_ukv[l].reshape(-1, N_MLA_HEADS, MLA_NOPE + MLA_V)
        wk = wkv[:, :, :MLA_NOPE].reshape(-1, N_MLA_HEADS * MLA_NOPE).astype(BF16)
        wv_t = wkv[:, :, MLA_NOPE:].reshape(-1, N_MLA_HEADS * MLA_V).T.astype(BF16)
        g_q, g_k = row(q_norm_g[l]), row(k_norm_g[l])
        q_t, k, v_t = _mlaprep(proj, pos_row, freq, row(q_lora_g[l]), row(kv_lora_g[l]), wuq_t, wk, wv_t,
                               g_q.reshape(-1, 1), _pad_lanes(g_k, MLA_QK_PAD))
        logit_bound = 1.02 * LOG2_E * MLA_QK ** 0.5 * jnp.max(jnp.abs(g_q)) * jnp.max(jnp.abs(g_k))
        attn, (w_out_b, w_ff1_b, w_ff2_b) = lax.cond(
            logit_bound <= SAFE_LOG2,
            functools.partial(_attention, shifted=False), functools.partial(_attention, shifted=True),
            q_t, k, v_t, (w_out[l], w_ff1[l], w_ff2[l]))

        n_qk_tiles = N_MLSTM_HEADS * MLSTM_DH // LANES
        q_m = _conv(proj, conv_w[l], row(conv_b[l]), first_tile=0, n_tiles=n_qk_tiles, out_scale=1.0,
                    transpose_out=False)
        k_m_t = _conv(proj, conv_w[l], row(conv_b[l]), first_tile=n_qk_tiles, n_tiles=n_qk_tiles,
                      out_scale=MLSTM_DH ** -0.5, transpose_out=True)
        hf, hb = _mlstm(q_m, k_m_t, proj, gates, _pad_lanes(row(b_gates[l]), LANES))

        x1, h2 = _outproj(attn, hf, hb, proj, xs, row(mlstm_norm_g[l]), w_out_b, mod, row(norm_mlp_g[l]))
        xs = _ffn(h2, w_ff1_b, w_ff2_b, x1, mod)
    return xs[None]
```

```python
import functools

import jax
import jax.numpy as jnp
from jax import lax
from jax.experimental import pallas as pl
from jax.experimental.pallas import tpu as pltpu

F32 = jnp.float32
BF16 = jnp.bfloat16

LANES = 128
N_MLA_HEADS = 8
MLA_NOPE = 128
ROPE_DIM = 64
ROPE_HALF = ROPE_DIM // 2
MLA_QK = MLA_NOPE + ROPE_DIM
MLA_QK_PAD = 256
MLA_V = 128
ATTN_TK = 256
Q_LORA, KV_LORA = 512, 256
PROJ_Q_M, PROJ_K_M, PROJ_V_M, PROJ_O_M = 0, 1024, 2048, 3072
PROJ_C_Q, PROJ_C_KV, PROJ_K_PE, PROJ_GATES = 4096, 4608, 4864, 4992
NORM_ROWS = 128
ROPE_THETA = 10000.0
N_MLSTM_HEADS = 4
MLSTM_DH = 256
MLSTM_CHUNK = 256
MLSTM_AUG = MLSTM_DH + LANES
CONV_WIDTH = 5
CONV_HALO = 8
EPS = 1e-6
M_INIT = -1e30
LOG2_E = 1.4426950408889634
SAFE_LOG2 = 60.0
VMEM_LIMIT = 56 * 1024 * 1024


def _cparams(sem):
    return pltpu.CompilerParams(dimension_semantics=sem, vmem_limit_bytes=VMEM_LIMIT)


def _mod_kernel(c_ref, w_ref, b_ref, o_ref, sb_ref, *, tn):
    @pl.when(pl.program_id(0) == 0)
    def _():
        cc = c_ref[...]
        sb_ref[...] = jnp.broadcast_to(cc * jax.nn.sigmoid(cc), sb_ref.shape)

    sb = sb_ref[...]
    for j in range(tn // LANES):
        sl = slice(j * LANES, (j + 1) * LANES)
        o_ref[:, sl] = jnp.sum(w_ref[:, sl] * sb, axis=0, keepdims=True) + b_ref[:, sl]


def _mod(c_col, w_ada, b_ada, tn=1024):
    d, n = w_ada.shape
    return pl.pallas_call(
        functools.partial(_mod_kernel, tn=tn),
        grid=(n // tn,),
        in_specs=[pl.BlockSpec((d, 1), lambda j: (0, 0)),
                  pl.BlockSpec((d, tn), lambda j: (0, j)),
                  pl.BlockSpec((1, tn), lambda j: (0, j))],
        out_specs=pl.BlockSpec((1, tn), lambda j: (0, j)),
        out_shape=jax.ShapeDtypeStruct((1, n), F32),
        scratch_shapes=[pltpu.VMEM((d, LANES), F32)],
        compiler_params=_cparams(("arbitrary",)),
        name="mod",
    )(c_col, w_ada, b_ada)


def _modulated_norm(x, g, scale, shift):
    ms = jnp.mean(x * x, axis=-1, keepdims=True)
    return (x * lax.rsqrt(ms + EPS) * g) * (1.0 + scale) + shift


def _inproj_kernel(x_ref, g_ref, sc_ref, sh_ref, w_ref, o_ref, gate_ref, h_ref, *, n_last):
    n = pl.program_id(1)
    project = lambda h: lax.dot_general(h, w_ref[...], (((1,), (1,)), ((), ())), preferred_element_type=F32)

    @pl.when(n == 0)
    def _():
        h = _modulated_norm(x_ref[...], g_ref[...], sc_ref[...], sh_ref[...]).astype(BF16)
        h_ref[...] = h
        o_ref[...] = project(h).astype(BF16)

    @pl.when(n > 0)
    def _():
        acc = project(h_ref[...])
        o_ref[...] = acc.astype(BF16)

        @pl.when(n == n_last)
        def _():
            gate_ref[...] = acc[:, -LANES:]


def _inproj(x, g, mod, w_t, tm=1024, tn=1024):
    s, d = x.shape
    n_tot = w_t.shape[0]
    return pl.pallas_call(
        functools.partial(_inproj_kernel, n_last=n_tot // tn - 1),
        grid=(s // tm, n_tot // tn),
        in_specs=[pl.BlockSpec((tm, d), lambda m, n: (m, 0)),
                  pl.BlockSpec((1, d), lambda m, n: (0, 0)),
                  pl.BlockSpec((1, d), lambda m, n: (0, 1)),
                  pl.BlockSpec((1, d), lambda m, n: (0, 0)),
                  pl.BlockSpec((tn, d), lambda m, n: (n, 0))],
        out_specs=[pl.BlockSpec((tm, tn), lambda m, n: (m, n)),
                   pl.BlockSpec((tm, LANES), lambda m, n: (m, 0))],
        out_shape=[jax.ShapeDtypeStruct((s, n_tot), BF16),
                   jax.ShapeDtypeStruct((s, LANES), F32)],
        scratch_shapes=[pltpu.VMEM((tm, d), BF16)],
        compiler_params=_cparams(("arbitrary", "arbitrary")),
        name="inproj",
    )(x, g, mod, mod, w_t)


def _rms_scale(x, n):
    return lax.rsqrt(jnp.sum(x * x, axis=-1, keepdims=True) * (1.0 / n) + EPS)


def _mlaprep_kernel(cq_ref, ckv_ref, kpe_ref, pos_ref, freq_ref, gql_ref, gkvl_ref, wuqt_ref, wk_ref, wvt_ref,
                    gqc_ref, gk_ref, q_ref, k_ref, v_ref):
    tm = cq_ref.shape[0]
    nt = (((1,), (1,)), ((), ()))
    cq = cq_ref[...].astype(F32)
    cqn = (cq * _rms_scale(cq, cq.shape[-1]) * gql_ref[...]).astype(BF16)
    ckv = ckv_ref[...].astype(F32)
    ckvn = (ckv * _rms_scale(ckv, ckv.shape[-1]) * gkvl_ref[...]).astype(BF16)
    qf_t = lax.dot_general(wuqt_ref[...], cqn, nt, preferred_element_type=F32)
    kf = jnp.dot(ckvn, wk_ref[...], preferred_element_type=F32)
    vf_t = lax.dot_general(wvt_ref[...], ckvn, nt, preferred_element_type=F32)

    ang_t = freq_ref[...] * pos_ref[...].astype(F32)
    cos_t, sin_t = jnp.cos(ang_t), jnp.sin(ang_t)

    gq_b = jnp.broadcast_to(gqc_ref[...] * (LOG2_E * MLA_QK ** -0.5), (MLA_QK, tm))
    q_pad = jnp.zeros((MLA_QK_PAD - MLA_QK, tm), F32)
    for h in range(N_MLA_HEADS):
        qh = qf_t[h * MLA_QK_PAD:h * MLA_QK_PAD + MLA_QK]
        r = lax.rsqrt(jnp.sum(qh * qh, axis=0, keepdims=True) * (1.0 / MLA_QK) + EPS)
        qn = qh * r * gq_b
        x1, x2 = qn[MLA_NOPE:MLA_NOPE + ROPE_HALF], qn[MLA_NOPE + ROPE_HALF:]
        q_ref[h] = jnp.concatenate([qn[:MLA_NOPE], x1 * cos_t - x2 * sin_t, x1 * sin_t + x2 * cos_t, q_pad],
                                   axis=0).astype(BF16)

    for h in range(N_MLA_HEADS):
        for c in range(tm // ATTN_TK):
            v_ref[h, c] = vf_t[h * MLA_V:(h + 1) * MLA_V, c * ATTN_TK:(c + 1) * ATTN_TK].astype(BF16)

    z_half, z_pad = jnp.zeros_like(cos_t), jnp.zeros((LANES - ROPE_DIM, tm), F32)
    c_tab = jnp.concatenate([cos_t, cos_t, z_pad], axis=0).T
    s_up = jnp.concatenate([z_half, sin_t, z_pad], axis=0).T
    s_dn = jnp.concatenate([-sin_t, z_half, z_pad], axis=0).T
    gk = gk_ref[...]
    kpe = kpe_ref[...].astype(F32)
    kpe_g = kpe * gk[:, MLA_NOPE:]
    kpe_rot = (kpe_g * c_tab + pltpu.roll(kpe_g, ROPE_HALF, 1) * s_up
               + pltpu.roll(kpe_g, LANES - ROPE_HALF, 1) * s_dn)
    kpe_ssq = jnp.sum(kpe * kpe, axis=-1, keepdims=True)
    for h in range(N_MLA_HEADS):
        kn = kf[:, h * MLA_NOPE:(h + 1) * MLA_NOPE]
        r = lax.rsqrt((jnp.sum(kn * kn, axis=-1, keepdims=True) + kpe_ssq) * (1.0 / MLA_QK) + EPS)
        k_ref[h] = jnp.concatenate([kn * r * gk[:, :MLA_NOPE], kpe_rot * r], axis=-1).astype(BF16)


def _mlaprep(proj, pos_row, freq, gql, gkvl, wuq_t, wk, wv_t, gq_col, gk, tm=512):
    s = proj.shape[0]
    hq = N_MLA_HEADS
    full = lambda a: pl.BlockSpec(a.shape, lambda i: (0,) * a.ndim)
    return pl.pallas_call(
        _mlaprep_kernel,
        grid=(s // tm,),
        in_specs=[pl.BlockSpec((tm, Q_LORA), lambda i: (i, PROJ_C_Q // Q_LORA)),
                  pl.BlockSpec((tm, KV_LORA), lambda i: (i, PROJ_C_KV // KV_LORA)),
                  pl.BlockSpec((tm, LANES), lambda i: (i, PROJ_K_PE // LANES)),
                  pl.BlockSpec((1, tm), lambda i: (0, i)),
                  full(freq), full(gql), full(gkvl), full(wuq_t), full(wk), full(wv_t), full(gq_col), full(gk)],
        out_specs=[pl.BlockSpec((hq, MLA_QK_PAD, tm), lambda i: (0, 0, i)),
                   pl.BlockSpec((hq, tm, MLA_QK_PAD), lambda i: (0, i, 0)),
                   pl.BlockSpec((hq, tm // ATTN_TK, MLA_V, ATTN_TK), lambda i: (0, i, 0, 0))],
        out_shape=[jax.ShapeDtypeStruct((hq, MLA_QK_PAD, s), BF16),
                   jax.ShapeDtypeStruct((hq, s, MLA_QK_PAD), BF16),
                   jax.ShapeDtypeStruct((hq, s // ATTN_TK, MLA_V, ATTN_TK), BF16)],
        compiler_params=_cparams(("arbitrary",)),
        name="mlaprep",
    )(proj, proj, proj, pos_row, freq, gql, gkvl, wuq_t, wk, wv_t, gq_col, gk)


def _attn_kernel(qt_ref, k_ref, vt_ref, *refs, n_side, shifted, tk, ahead, n_sub):
    side_in, o_ref, side_out = refs[:n_side], refs[n_side], refs[n_side + 1:]
    for src, dst in zip(side_in, side_out):
        dst[...] = src[...].astype(BF16)

    n_chunks = k_ref.shape[1] // tk
    tq = qt_ref.shape[2] // n_sub

    def logits_t(item):
        t, j = item
        return jnp.dot(k_ref[0, j * tk:(j + 1) * tk, :], qt_ref[0, :, t * tq:(t + 1) * tq],
                       preferred_element_type=F32)

    def values_t(j, p_t):
        return jnp.dot(vt_ref[0, j], p_t.astype(BF16), preferred_element_type=F32)

    items = [(t, j) for t in range(n_sub) for j in range(n_chunks)]
    pending = [logits_t(it) for it in items[:ahead]]
    for i, (t, j) in enumerate(items):
        if i + ahead < len(items):
            pending.append(logits_t(items[i + ahead]))
        s2 = pending.pop(0)
        if j == 0:
            m, acc, l = jnp.full((1, tq), -jnp.inf, F32), None, None
        if shifted:
            m_new = jnp.maximum(m, jnp.max(s2, axis=0, keepdims=True))
            p_t, alpha = jnp.exp2(s2 - m_new), jnp.exp2(m - m_new)
            pv, ps = values_t(j, p_t), jnp.sum(p_t, axis=0, keepdims=True)
            acc, l = (pv, ps) if acc is None else (acc * alpha + pv, l * alpha + ps)
            m = m_new
        else:
            p_t = jnp.exp2(s2)
            pv, ps = values_t(j, p_t), jnp.sum(p_t, axis=0, keepdims=True)
            acc, l = (pv, ps) if acc is None else (acc + pv, l + ps)
        if j == n_chunks - 1:
            o_ref[t * tq:(t + 1) * tq, :] = (acc / l).T.astype(BF16)


def _attention(q_t, k, v_t, side_weights, *, shifted, tq=1024, n_sub=2, tk=ATTN_TK, ahead=2):
    hq, s, _ = k.shape
    n_q = s // tq
    n_steps = hq * n_q
    slab = lambda w: pl.BlockSpec((w.shape[0] // n_steps, w.shape[1]), lambda h, i: (h * n_q + i, 0))
    outs = pl.pallas_call(
        functools.partial(_attn_kernel, n_side=len(side_weights), shifted=shifted, tk=tk, ahead=ahead,
                          n_sub=n_sub),
        grid=(hq, n_q),
        in_specs=[pl.BlockSpec((1, MLA_QK_PAD, tq), lambda h, i: (h, 0, i)),
                  pl.BlockSpec((1, s, MLA_QK_PAD), lambda h, i: (h, 0, 0)),
                  pl.BlockSpec((1, s // tk, MLA_V, tk), lambda h, i: (h, 0, 0, 0))]
                 + [slab(w) for w in side_weights],
        out_specs=[pl.BlockSpec((tq, MLA_V), lambda h, i: (i, h))] + [slab(w) for w in side_weights],
        out_shape=[jax.ShapeDtypeStruct((s, hq * MLA_V), BF16)]
                  + [jax.ShapeDtypeStruct(w.shape, BF16) for w in side_weights],
        compiler_params=_cparams(("arbitrary", "arbitrary")),
        name="attn_shifted" if shifted else "attn",
    )(q_t, k, v_t, *side_weights)
    return outs[0], tuple(outs[1:])


def _conv_kernel(x_ref, w_ref, b_ref, o_ref, pad_ref, *, rows, out_scale, transpose_out):
    s = x_ref.shape[0]
    zeros = jnp.zeros((CONV_HALO, LANES), F32)
    pad_ref[0:CONV_HALO, :] = zeros
    pad_ref[CONV_HALO + s:CONV_HALO + s + CONV_HALO, :] = zeros
    pad_ref[CONV_HALO:CONV_HALO + s, :] = x_ref[...].astype(F32)
    w = w_ref[...]
    b = b_ref[...]
    for r in range(s // rows):
        base = r * rows
        acc = b
        for j in range(CONV_WIDTH):
            off = CONV_HALO + j - CONV_WIDTH // 2
            acc = acc + w[j:j + 1, :] * pad_ref[base + off:base + off + rows, :]
        y = acc * jax.nn.sigmoid(acc) * out_scale
        if transpose_out:
            o_ref[:, base:base + rows] = y.T.astype(BF16)
        else:
            o_ref[base:base + rows, :] = y.astype(BF16)


def _conv(proj, conv_w, conv_b, *, first_tile, n_tiles, out_scale, transpose_out, rows=256):
    s = proj.shape[0]
    n_ch = n_tiles * LANES
    return pl.pallas_call(
        functools.partial(_conv_kernel, rows=rows, out_scale=out_scale, transpose_out=transpose_out),
        grid=(n_tiles,),
        in_specs=[pl.BlockSpec((s, LANES), lambda j: (0, first_tile + j)),
                  pl.BlockSpec((CONV_WIDTH, LANES), lambda j: (0, first_tile + j)),
                  pl.BlockSpec((1, LANES), lambda j: (0, first_tile + j))],
        out_specs=pl.BlockSpec((LANES, s), lambda j: (j, 0)) if transpose_out
                  else pl.BlockSpec((s, LANES), lambda j: (0, j)),
        out_shape=jax.ShapeDtypeStruct((n_ch, s) if transpose_out else (s, n_ch), BF16),
        scratch_shapes=[pltpu.VMEM((s + 2 * CONV_HALO, LANES), F32)],
        compiler_params=_cparams(("arbitrary",)),
        name="conv_k" if transpose_out else "conv_q",
    )(proj, conv_w, conv_b)


def _dot_f32_by_mask(mask01, x):
    hi = x.astype(BF16)
    rest = x - hi.astype(F32)
    mid = rest.astype(BF16)
    lo = (rest - mid.astype(F32)).astype(BF16)
    return (jnp.dot(mask01, hi, preferred_element_type=F32) + jnp.dot(mask01, mid, preferred_element_type=F32)
            + jnp.dot(mask01, lo, preferred_element_type=F32))


def _scan_max_rows(x, row_id, reverse):
    n, s = x.shape[0], 1
    while s < n:
        shifted = pltpu.roll(x, n - s if reverse else s, 0)
        ok = (row_id < n - s) if reverse else (row_id >= s)
        x = jnp.maximum(x, jnp.where(ok, shifted, -jnp.inf))
        s *= 2
    return x


def _mlstm_kernel(qf_ref, kf_ref, vf_ref, gf_ref, qb_ref, kb_ref, vb_ref, gb_ref, bg_ref,
                  hf_ref, hb_ref, c_ref, m_ref):
    @pl.when(pl.program_id(0) == 0)
    def _():
        c_ref[...] = jnp.zeros(c_ref.shape, F32)
        m_ref[...] = jnp.full(m_ref.shape, M_INIT, F32)

    L = MLSTM_CHUNK
    row = lax.broadcasted_iota(jnp.int32, (L, L), 0)
    col = lax.broadcasted_iota(jnp.int32, (L, L), 1)
    row_id = lax.broadcasted_iota(jnp.int32, (L, LANES), 0)
    ones_col = (lax.broadcasted_iota(jnp.int32, (L, LANES), 1) == 0).astype(BF16)
    hm = N_MLSTM_HEADS
    for d, (q_ref, k_ref, v_ref, g_ref, o_ref) in enumerate(
            ((qf_ref, kf_ref, vf_ref, gf_ref, hf_ref), (qb_ref, kb_ref, vb_ref, gb_ref, hb_ref))):
        backward = d == 1
        mask = (row <= col) if backward else (row >= col)
        g = g_ref[...] + bg_ref[...]
        lf = jax.nn.log_sigmoid(g)
        b_c = _dot_f32_by_mask(mask.astype(BF16), lf)
        to_lane0 = lambda t, first: t if first == 0 else pltpu.roll(t, LANES - first, 1)
        log_i = to_lane0(g, 2 * d * hm)
        b = to_lane0(b_c, (2 * d + 1) * hm)
        b_tot = jnp.sum(to_lane0(lf, (2 * d + 1) * hm), axis=0, keepdims=True)
        a = log_i - b
        m_prev = m_ref[d:d + 1, :]
        mx = jnp.maximum(m_prev, _scan_max_rows(a, row_id, backward))
        u2 = mx * (-LOG2_E)
        w_inter = jnp.exp(m_prev - mx)
        floor = jnp.exp(-(b + mx))
        mn = jnp.maximum(m_prev, jnp.max(a, axis=0, keepdims=True))
        decay = jnp.exp(m_prev - mn)
        m_ref[d:d + 1, :] = b_tot + mn
        a_t = a.T
        for h in range(hm):
            sl = slice(h * MLSTM_DH, (h + 1) * MLSTM_DH)
            chain = d * hm + h
            q, k_t = q_ref[:, sl], k_ref[sl, :]
            v_aug = jnp.concatenate([v_ref[:, sl], ones_col], axis=-1)
            a_row = a_t[h:h + 1, :]
            d_m = jnp.where(mask, jnp.exp2(u2[:, h:h + 1] + a_row * LOG2_E), 0.0)
            c_aug = c_ref[chain]
            scores = jnp.dot(q, k_t, preferred_element_type=F32) * d_m
            r = (jnp.dot(scores.astype(BF16), v_aug, preferred_element_type=F32)
                 + w_inter[:, h:h + 1] * jnp.dot(q, c_aug.astype(BF16), preferred_element_type=F32))
            num, den = r[:, :MLSTM_DH], r[:, MLSTM_DH:MLSTM_DH + 1]
            o_ref[:, sl] = (num / jnp.maximum(jnp.abs(den), floor[:, h:h + 1])).astype(o_ref.dtype)

            kw_t = (k_t.astype(F32) * jnp.exp(a_row - mn[:, h:h + 1])).astype(BF16)
            c_ref[chain] = decay[:, h:h + 1] * c_aug + jnp.dot(kw_t, v_aug, preferred_element_type=F32)


def _mlstm(q, k_t, proj, gates, bg):
    s = q.shape[0]
    L = MLSTM_CHUNK
    nc = s // L
    w = N_MLSTM_HEADS * MLSTM_DH
    fwd = lambda blk: (lambda i: (i, blk))
    bwd = lambda blk: (lambda i: (nc - 1 - i, blk))
    specs = lambda ix: [pl.BlockSpec((L, w), ix(0)),
                        pl.BlockSpec((w, L), lambda i: ix(0)(i)[::-1]),
                        pl.BlockSpec((L, w), ix(PROJ_V_M // w)),
                        pl.BlockSpec((L, LANES), ix(0))]
    return pl.pallas_call(
        _mlstm_kernel,
        grid=(nc,),
        in_specs=specs(fwd) + specs(bwd) + [pl.BlockSpec((1, LANES), lambda i: (0, 0))],
        out_specs=[pl.BlockSpec((L, w), fwd(0)), pl.BlockSpec((L, w), bwd(0))],
        out_shape=[jax.ShapeDtypeStruct((s, w), BF16), jax.ShapeDtypeStruct((s, w), BF16)],
        scratch_shapes=[pltpu.VMEM((2 * N_MLSTM_HEADS, MLSTM_DH, MLSTM_AUG), F32),
                        pltpu.VMEM((2 * N_MLSTM_HEADS, LANES), F32)],
        compiler_params=_cparams(("arbitrary",)),
        name="mlstm",
    )(q, k_t, proj, gates, q, k_t, proj, gates, bg)


def _outproj_kernel(attn_ref, hf_ref, hb_ref, om_ref, x_ref, gm_ref, w_ref, gate_ref, g2_ref, sc_ref, sh_ref,
                    x1_ref, h2_ref, *, n_sub):
    gm = gm_ref[...]
    n_attn = attn_ref.shape[1]
    sub = x_ref.shape[0] // n_sub
    for t in range(n_sub):
        rows = slice(t * sub, (t + 1) * sub)
        hsum = hf_ref[rows, :].astype(F32) + hb_ref[rows, :].astype(F32)
        parts = []
        for h in range(N_MLSTM_HEADS):
            sl = slice(h * MLSTM_DH, (h + 1) * MLSTM_DH)
            seg = hsum[:, sl]
            parts.append(seg * _rms_scale(seg, MLSTM_DH) * gm[:, sl])
        ml = (jax.nn.sigmoid(om_ref[rows, :].astype(F32)) * jnp.concatenate(parts, axis=-1)).astype(BF16)
        mixed = (jnp.dot(attn_ref[rows, :], w_ref[:n_attn, :], preferred_element_type=F32)
                 + jnp.dot(ml, w_ref[n_attn:, :], preferred_element_type=F32))
        x1 = x_ref[rows, :] + gate_ref[...] * mixed
        x1_ref[rows, :] = x1
        h2_ref[rows, :] = _modulated_norm(x1, g2_ref[...], sc_ref[...], sh_ref[...]).astype(BF16)


def _outproj(attn, hf, hb, proj, x, gm, w_out, mod, g2, tm=512, n_sub=1):
    s, d = x.shape
    wm = hf.shape[1]
    row = lambda blk: pl.BlockSpec((1, d), lambda i: (0, blk))
    return pl.pallas_call(
        functools.partial(_outproj_kernel, n_sub=n_sub),
        grid=(s // tm,),
        in_specs=[pl.BlockSpec((tm, attn.shape[1]), lambda i: (i, 0)),
                  pl.BlockSpec((tm, wm), lambda i: (i, 0)),
                  pl.BlockSpec((tm, wm), lambda i: (i, 0)),
                  pl.BlockSpec((tm, wm), lambda i: (i, PROJ_O_M // wm)),
                  pl.BlockSpec((tm, d), lambda i: (i, 0)),
                  pl.BlockSpec((1, wm), lambda i: (0, 0)),
                  pl.BlockSpec(w_out.shape, lambda i: (0, 0), pipeline_mode=pl.Buffered(1)),
                  row(2), row(0), row(4), row(3)],
        out_specs=[pl.BlockSpec((tm, d), lambda i: (i, 0)), pl.BlockSpec((tm, d), lambda i: (i, 0))],
        out_shape=[jax.ShapeDtypeStruct((s, d), F32), jax.ShapeDtypeStruct((s, d), BF16)],
        compiler_params=_cparams(("arbitrary",)),
        name="outproj",
    )(attn, hf, hb, proj, x, gm, w_out, mod, g2, mod, mod)


def _ffn_kernel(h2_ref, w1_ref, w2_ref, x1_hbm, gate_ref, o_ref, x1_buf, x1_sem, *, f_last):
    m, f = pl.program_id(0), pl.program_id(1)
    tm = o_ref.shape[0]
    x1_copy = pltpu.make_async_copy(x1_hbm.at[pl.ds(pl.multiple_of(m * tm, tm), tm), :], x1_buf, x1_sem)

    def hidden_tile():
        a = jnp.maximum(jnp.dot(h2_ref[...], w1_ref[...], preferred_element_type=F32), 0.0)
        return jnp.dot((a * a).astype(BF16), w2_ref[...], preferred_element_type=F32)

    @pl.when(f == 0)
    def _():
        x1_copy.start()
        o_ref[...] = hidden_tile()

    @pl.when(f > 0)
    def _():
        o_ref[...] += hidden_tile()

    @pl.when(f == f_last)
    def _():
        x1_copy.wait()
        o_ref[...] = x1_buf[...] + gate_ref[...] * o_ref[...]


def _ffn(h2, w1, w2, x1, mod, tm=1024, tf=1024):
    s, d = h2.shape
    dff = w1.shape[1]
    return pl.pallas_call(
        functools.partial(_ffn_kernel, f_last=dff // tf - 1),
        grid=(s // tm, dff // tf),
        in_specs=[pl.BlockSpec((tm, d), lambda m, f: (m, 0)),
                  pl.BlockSpec((d, tf), lambda m, f: (0, f)),
                  pl.BlockSpec((tf, d), lambda m, f: (f, 0)),
                  pl.BlockSpec(memory_space=pl.ANY),
                  pl.BlockSpec((1, d), lambda m, f: (0, 5))],
        out_specs=pl.BlockSpec((tm, d), lambda m, f: (m, 0)),
        out_shape=jax.ShapeDtypeStruct((s, d), F32),
        scratch_shapes=[pltpu.VMEM((tm, d), F32), pltpu.SemaphoreType.DMA(())],
        compiler_params=_cparams(("arbitrary", "arbitrary")),
        name="ffn",
    )(h2, w1, w2, x1, mod)


PACK_TILE = 512
PACK_SRC = 64
PACK_NSRC = PACK_TILE // PACK_SRC
W_IN_MIX_START = 832
W_IN_GATES_START = 4928
N_GATE_COLS = 16


def _pack_src_blocks():
    rows = []
    n_mix = 4 * N_MLSTM_HEADS * MLSTM_DH // PACK_TILE
    for n in range(n_mix):
        first = (W_IN_MIX_START + n * PACK_TILE) // PACK_SRC
        rows.append([first + i for i in range(PACK_NSRC)])
    rows.append(list(range(PACK_NSRC)))
    gates_blk = W_IN_GATES_START // PACK_SRC
    rows.append([8, 9, 10, 11, 12, 12, gates_blk, gates_blk])
    return jnp.asarray(rows, jnp.int32).reshape(-1)


def _pack_kernel(tbl_ref, *refs, n_mix):
    del tbl_ref
    srcs, o_ref = refs[:-1], refs[-1]
    n = pl.program_id(0)
    blk = lambda i: slice(i * PACK_SRC, (i + 1) * PACK_SRC)

    @pl.when(n <= n_mix)
    def _():
        for i, b in enumerate(srcs):
            o_ref[blk(i), :] = b[...].astype(BF16)

    @pl.when(n == n_mix + 1)
    def _():
        zeros = jnp.zeros(srcs[0].shape, BF16)
        for i in range(5):
            o_ref[blk(i), :] = srcs[i][...].astype(BF16)
        o_ref[blk(5), :] = zeros
        rows = lax.broadcasted_iota(jnp.int32, srcs[6].shape, 0)
        o_ref[blk(6), :] = jnp.where(rows < N_GATE_COLS, srcs[6][...], 0.0).astype(BF16)
        o_ref[blk(7), :] = zeros


def _pack_w_in(w_in_t):
    d = w_in_t.shape[1]
    tbl = _pack_src_blocks()
    n_tiles = tbl.shape[0] // PACK_NSRC
    src = lambda i: pl.BlockSpec((PACK_SRC, d), lambda n, t: (t[n * PACK_NSRC + i], 0))
    return pl.pallas_call(
        functools.partial(_pack_kernel, n_mix=n_tiles - 2),
        grid_spec=pltpu.PrefetchScalarGridSpec(
            num_scalar_prefetch=1, grid=(n_tiles,),
            in_specs=[src(i) for i in range(PACK_NSRC)],
            out_specs=pl.BlockSpec((PACK_TILE, d), lambda n, t: (n, 0))),
        out_shape=jax.ShapeDtypeStruct((n_tiles * PACK_TILE, d), BF16),
        compiler_params=_cparams(("arbitrary",)),
        name="packw",
    )(tbl, *([w_in_t] * PACK_NSRC))


def _pad_lanes(v, n):
    return jnp.pad(v, ((0, 0), (0, n - v.shape[1])))


def kernel(x, c, positions, w_ada, b_ada, norm_mix_g, w_in, b_gates, conv_w, conv_b, q_lora_g, w_uq,
           kv_lora_g, w_ukv, q_norm_g, k_norm_g, mlstm_norm_g, w_out, norm_mlp_g, w_ff1, w_ff2):
    bsz, s, d = x.shape
    assert bsz == 1, "kernels are written for a single sequence"
    xs = x[0]
    pos_row = positions.reshape(1, s)
    half = jnp.arange(ROPE_HALF, dtype=F32)
    freq = (ROPE_THETA ** (-half / ROPE_HALF)).reshape(ROPE_HALF, 1)
    row = lambda v: v.reshape(1, -1).astype(F32)

    for l in range(w_ada.shape[0]):
        mod = _mod(c.reshape(d, 1), w_ada[l], row(b_ada[l]))
        proj, gates = _inproj(xs, row(norm_mix_g[l]), mod, _pack_w_in(w_in[l].T))

        wuq_t = jnp.pad(w_uq[l].reshape(-1, N_MLA_HEADS, MLA_QK), ((0, 0), (0, 0), (0, MLA_QK_PAD - MLA_QK)))
        wuq_t = wuq_t.reshape(-1, N_MLA_HEADS * MLA_QK_PAD).T.astype(BF16)
        wkv = w_ukv[l].reshape(-1, N_MLA_HEADS, MLA_NOPE + MLA_V)
        wk = wkv[:, :, :MLA_NOPE].reshape(-1, N_MLA_HEADS * MLA_NOPE).astype(BF16)
        wv_t = wkv[:, :, MLA_NOPE:].reshape(-1, N_MLA_HEADS * MLA_V).T.astype(BF16)
        g_q, g_k = row(q_norm_g[l]), row(k_norm_g[l])
        q_t, k, v_t = _mlaprep(proj, pos_row, freq, row(q_lora_g[l]), row(kv_lora_g[l]), wuq_t, wk, wv_t,
                               g_q.reshape(-1, 1), _pad_lanes(g_k, MLA_QK_PAD))
        logit_bound = 1.02 * LOG2_E * MLA_QK ** 0.5 * jnp.max(jnp.abs(g_q)) * jnp.max(jnp.abs(g_k))
        attn, (w_out_b, w_ff1_b, w_ff2_b) = lax.cond(
            logit_bound <= SAFE_LOG2,
            functools.partial(_attention, shifted=False), functools.partial(_attention, shifted=True),
            q_t, k, v_t, (w_out[l], w_ff1[l], w_ff2[l]))

        n_qk_tiles = N_MLSTM_HEADS * MLSTM_DH // LANES
        assert (PROJ_Q_M, PROJ_K_M) == (0, n_qk_tiles * LANES)
        q_m = _conv(proj, conv_w[l], row(conv_b[l]), first_tile=0, n_tiles=n_qk_tiles, out_scale=1.0,
                    transpose_out=False)
        k_m_t = _conv(proj, conv_w[l], row(conv_b[l]), first_tile=n_qk_tiles, n_tiles=n_qk_tiles,
                      out_scale=MLSTM_DH ** -0.5, transpose_out=True)
        hf, hb = _mlstm(q_m, k_m_t, proj, gates, _pad_lanes(row(b_gates[l]), LANES))

        x1, h2 = _outproj(attn, hf, hb, proj, xs, row(mlstm_norm_g[l]), w_out_b, mod, row(norm_mlp_g[l]))
        xs = _ffn(h2, w_ff1_b, w_ff2_b, x1, mod)
    return xs[None]
```

```python
import functools

import jax
import jax.numpy as jnp
from jax import lax
from jax.experimental import pallas as pl
from jax.experimental.pallas import tpu as pltpu

F32 = jnp.float32
BF16 = jnp.bfloat16

LANES = 128
N_MLA_HEADS = 8
MLA_NOPE = 128
ROPE_DIM = 64
ROPE_HALF = ROPE_DIM // 2
MLA_QK = MLA_NOPE + ROPE_DIM
MLA_QK_PAD = 256
MLA_V = 128
ATTN_TK = 256
Q_LORA, KV_LORA = 512, 256
PROJ_Q_M, PROJ_K_M, PROJ_V_M, PROJ_O_M = 0, 1024, 2048, 3072
PROJ_C_Q, PROJ_C_KV, PROJ_K_PE, PROJ_GATES = 4096, 4608, 4864, 4992
MOD_SHIFT1, MOD_SCALE1, MOD_GATE1, MOD_SHIFT2, MOD_SCALE2, MOD_GATE2 = range(6)
ROPE_THETA = 10000.0
N_MLSTM_HEADS = 4
MLSTM_DH = 256
MLSTM_CHUNK = 256
MLSTM_AUG = MLSTM_DH + LANES
CONV_WIDTH = 5
CONV_HALO = 8
EPS = 1e-6
M_INIT = -1e30
LOG2_E = 1.4426950408889634
SAFE_LOG2 = 60.0
VMEM_LIMIT = 56 * 1024 * 1024


def _cparams(sem):
    return pltpu.CompilerParams(dimension_semantics=sem, vmem_limit_bytes=VMEM_LIMIT)


def _mod_kernel(c_ref, w_ref, b_ref, o_ref, sb_ref, *, tn):
    @pl.when(pl.program_id(0) == 0)
    def _():
        cc = c_ref[...]
        sb_ref[...] = jnp.broadcast_to(cc * jax.nn.sigmoid(cc), sb_ref.shape)

    sb = sb_ref[...]
    for j in range(tn // LANES):
        sl = slice(j * LANES, (j + 1) * LANES)
        o_ref[:, sl] = jnp.sum(w_ref[:, sl] * sb, axis=0, keepdims=True) + b_ref[:, sl]


def _mod(c_col, w_ada, b_ada, tn=1024):
    d, n = w_ada.shape
    return pl.pallas_call(
        functools.partial(_mod_kernel, tn=tn),
        grid=(n // tn,),
        in_specs=[pl.BlockSpec((d, 1), lambda j: (0, 0)),
                  pl.BlockSpec((d, tn), lambda j: (0, j)),
                  pl.BlockSpec((1, tn), lambda j: (0, j))],
        out_specs=pl.BlockSpec((1, tn), lambda j: (0, j)),
        out_shape=jax.ShapeDtypeStruct((1, n), F32),
        scratch_shapes=[pltpu.VMEM((d, LANES), F32)],
        compiler_params=_cparams(("arbitrary",)),
        name="mod",
    )(c_col, w_ada, b_ada)


def _modulated_norm(x, g, scale, shift):
    ms = jnp.mean(x * x, axis=-1, keepdims=True)
    return (x * lax.rsqrt(ms + EPS) * g) * (1.0 + scale) + shift


def _inproj_kernel(x_ref, g_ref, sc_ref, sh_ref, w_ref, o_ref, gate_ref, h_ref, *, n_last):
    n = pl.program_id(1)
    project = lambda h: lax.dot_general(h, w_ref[...], (((1,), (1,)), ((), ())), preferred_element_type=F32)

    @pl.when(n == 0)
    def _():
        h = _modulated_norm(x_ref[...], g_ref[...], sc_ref[...], sh_ref[...]).astype(BF16)
        h_ref[...] = h
        o_ref[...] = project(h).astype(BF16)

    @pl.when(n > 0)
    def _():
        acc = project(h_ref[...])
        o_ref[...] = acc.astype(BF16)

        @pl.when(n == n_last)
        def _():
            gate_ref[...] = acc[:, -LANES:]


def _inproj(x, g, mod, w_t, tm=1024, tn=1024):
    s, d = x.shape
    n_tot = w_t.shape[0]
    return pl.pallas_call(
        functools.partial(_inproj_kernel, n_last=n_tot // tn - 1),
        grid=(s // tm, n_tot // tn),
        in_specs=[pl.BlockSpec((tm, d), lambda m, n: (m, 0)),
                  pl.BlockSpec((1, d), lambda m, n: (0, 0)),
                  pl.BlockSpec((1, d), lambda m, n: (0, MOD_SCALE1)),
                  pl.BlockSpec((1, d), lambda m, n: (0, MOD_SHIFT1)),
                  pl.BlockSpec((tn, d), lambda m, n: (n, 0))],
        out_specs=[pl.BlockSpec((tm, tn), lambda m, n: (m, n)),
                   pl.BlockSpec((tm, LANES), lambda m, n: (m, 0))],
        out_shape=[jax.ShapeDtypeStruct((s, n_tot), BF16),
                   jax.ShapeDtypeStruct((s, LANES), F32)],
        scratch_shapes=[pltpu.VMEM((tm, d), BF16)],
        compiler_params=_cparams(("arbitrary", "arbitrary")),
        name="inproj",
    )(x, g, mod, mod, w_t)


def _rms_scale(x, n):
    return lax.rsqrt(jnp.sum(x * x, axis=-1, keepdims=True) * (1.0 / n) + EPS)


def _mlaprep_kernel(cq_ref, ckv_ref, kpe_ref, pos_ref, freq_ref, gql_ref, gkvl_ref, wuqt_ref, wk_ref, wvt_ref,
                    gqc_ref, gk_ref, q_ref, k_ref, v_ref):
    tm = cq_ref.shape[0]
    nt = (((1,), (1,)), ((), ()))
    cq = cq_ref[...].astype(F32)
    cqn = (cq * _rms_scale(cq, cq.shape[-1]) * gql_ref[...]).astype(BF16)
    ckv = ckv_ref[...].astype(F32)
    ckvn = (ckv * _rms_scale(ckv, ckv.shape[-1]) * gkvl_ref[...]).astype(BF16)
    qf_t = lax.dot_general(wuqt_ref[...], cqn, nt, preferred_element_type=F32)
    kf = jnp.dot(ckvn, wk_ref[...], preferred_element_type=F32)
    vf_t = lax.dot_general(wvt_ref[...], ckvn, nt, preferred_element_type=F32)

    ang_t = freq_ref[...] * pos_ref[...].astype(F32)
    cos_t, sin_t = jnp.cos(ang_t), jnp.sin(ang_t)

    gq_b = jnp.broadcast_to(gqc_ref[...] * (LOG2_E * MLA_QK ** -0.5), (MLA_QK, tm))
    q_pad = jnp.zeros((MLA_QK_PAD - MLA_QK, tm), F32)
    for h in range(N_MLA_HEADS):
        qh = qf_t[h * MLA_QK_PAD:h * MLA_QK_PAD + MLA_QK]
        r = lax.rsqrt(jnp.sum(qh * qh, axis=0, keepdims=True) * (1.0 / MLA_QK) + EPS)
        qn = qh * r * gq_b
        x1, x2 = qn[MLA_NOPE:MLA_NOPE + ROPE_HALF], qn[MLA_NOPE + ROPE_HALF:]
        q_ref[h] = jnp.concatenate([qn[:MLA_NOPE], x1 * cos_t - x2 * sin_t, x1 * sin_t + x2 * cos_t, q_pad],
                                   axis=0).astype(BF16)

    for h in range(N_MLA_HEADS):
        for c in range(tm // ATTN_TK):
            v_ref[h, c] = vf_t[h * MLA_V:(h + 1) * MLA_V, c * ATTN_TK:(c + 1) * ATTN_TK].astype(BF16)

    z_half, z_pad = jnp.zeros_like(cos_t), jnp.zeros((LANES - ROPE_DIM, tm), F32)
    c_tab = jnp.concatenate([cos_t, cos_t, z_pad], axis=0).T
    s_up = jnp.concatenate([z_half, sin_t, z_pad], axis=0).T
    s_dn = jnp.concatenate([-sin_t, z_half, z_pad], axis=0).T
    gk = gk_ref[...]
    kpe = kpe_ref[...].astype(F32)
    kpe_g = kpe * gk[:, MLA_NOPE:]
    kpe_rot = (kpe_g * c_tab + pltpu.roll(kpe_g, ROPE_HALF, 1) * s_up
               + pltpu.roll(kpe_g, LANES - ROPE_HALF, 1) * s_dn)
    kpe_ssq = jnp.sum(kpe * kpe, axis=-1, keepdims=True)
    for h in range(N_MLA_HEADS):
        kn = kf[:, h * MLA_NOPE:(h + 1) * MLA_NOPE]
        r = lax.rsqrt((jnp.sum(kn * kn, axis=-1, keepdims=True) + kpe_ssq) * (1.0 / MLA_QK) + EPS)
        k_ref[h] = jnp.concatenate([kn * r * gk[:, :MLA_NOPE], kpe_rot * r], axis=-1).astype(BF16)


def _mlaprep(proj, pos_row, freq, gql, gkvl, wuq_t, wk, wv_t, gq_col, gk, tm=1024):
    s = proj.shape[0]
    hq = N_MLA_HEADS
    full = lambda a: pl.BlockSpec(a.shape, lambda i: (0,) * a.ndim)
    return pl.pallas_call(
        _mlaprep_kernel,
        grid=(s // tm,),
        in_specs=[pl.BlockSpec((tm, Q_LORA), lambda i: (i, PROJ_C_Q // Q_LORA)),
                  pl.BlockSpec((tm, KV_LORA), lambda i: (i, PROJ_C_KV // KV_LORA)),
                  pl.BlockSpec((tm, LANES), lambda i: (i, PROJ_K_PE // LANES)),
                  pl.BlockSpec((1, tm), lambda i: (0, i)),
                  full(freq), full(gql), full(gkvl), full(wuq_t), full(wk), full(wv_t), full(gq_col), full(gk)],
        out_specs=[pl.BlockSpec((hq, MLA_QK_PAD, tm), lambda i: (0, 0, i)),
                   pl.BlockSpec((hq, tm, MLA_QK_PAD), lambda i: (0, i, 0)),
                   pl.BlockSpec((hq, tm // ATTN_TK, MLA_V, ATTN_TK), lambda i: (0, i, 0, 0))],
        out_shape=[jax.ShapeDtypeStruct((hq, MLA_QK_PAD, s), BF16),
                   jax.ShapeDtypeStruct((hq, s, MLA_QK_PAD), BF16),
                   jax.ShapeDtypeStruct((hq, s // ATTN_TK, MLA_V, ATTN_TK), BF16)],
        compiler_params=_cparams(("arbitrary",)),
        name="mlaprep",
    )(proj, proj, proj, pos_row, freq, gql, gkvl, wuq_t, wk, wv_t, gq_col, gk)


def _attn_kernel(qt_ref, k_ref, vt_ref, *refs, n_side, shifted, tk, ahead, n_sub):
    side_in, o_ref, side_out = refs[:n_side], refs[n_side], refs[n_side + 1:]
    for src, dst in zip(side_in, side_out):
        dst[...] = src[...].astype(BF16)

    n_chunks = k_ref.shape[1] // tk
    tq = qt_ref.shape[2] // n_sub

    def logits_t(item):
        t, j = item
        return jnp.dot(k_ref[0, j * tk:(j + 1) * tk, :], qt_ref[0, :, t * tq:(t + 1) * tq],
                       preferred_element_type=F32)

    def values_t(j, p_t):
        return jnp.dot(vt_ref[0, j], p_t.astype(BF16), preferred_element_type=F32)

    items = [(t, j) for t in range(n_sub) for j in range(n_chunks)]
    pending = [logits_t(it) for it in items[:ahead]]
    for i, (t, j) in enumerate(items):
        if i + ahead < len(items):
            pending.append(logits_t(items[i + ahead]))
        s2 = pending.pop(0)
        if j == 0:
            m, acc, l = jnp.full((1, tq), -jnp.inf, F32), None, None
        if shifted:
            m_new = jnp.maximum(m, jnp.max(s2, axis=0, keepdims=True))
            p_t, alpha = jnp.exp2(s2 - m_new), jnp.exp2(m - m_new)
            pv, ps = values_t(j, p_t), jnp.sum(p_t, axis=0, keepdims=True)
            acc, l = (pv, ps) if acc is None else (acc * alpha + pv, l * alpha + ps)
            m = m_new
        else:
            p_t = jnp.exp2(s2)
            pv, ps = values_t(j, p_t), jnp.sum(p_t, axis=0, keepdims=True)
            acc, l = (pv, ps) if acc is None else (acc + pv, l + ps)
        if j == n_chunks - 1:
            o_ref[t * tq:(t + 1) * tq, :] = (acc / l).T.astype(BF16)


def _attention(q_t, k, v_t, side_weights, *, shifted, tq=1024, n_sub=2, tk=ATTN_TK, ahead=2):
    hq, s, _ = k.shape
    n_q = s // tq
    n_steps = hq * n_q
    slab = lambda w: pl.BlockSpec((w.shape[0] // n_steps, w.shape[1]), lambda h, i: (h * n_q + i, 0))
    outs = pl.pallas_call(
        functools.partial(_attn_kernel, n_side=len(side_weights), shifted=shifted, tk=tk, ahead=ahead,
                          n_sub=n_sub),
        grid=(hq, n_q),
        in_specs=[pl.BlockSpec((1, MLA_QK_PAD, tq), lambda h, i: (h, 0, i)),
                  pl.BlockSpec((1, s, MLA_QK_PAD), lambda h, i: (h, 0, 0)),
                  pl.BlockSpec((1, s // tk, MLA_V, tk), lambda h, i: (h, 0, 0, 0))]
                 + [slab(w) for w in side_weights],
        out_specs=[pl.BlockSpec((tq, MLA_V), lambda h, i: (i, h))] + [slab(w) for w in side_weights],
        out_shape=[jax.ShapeDtypeStruct((s, hq * MLA_V), BF16)]
                  + [jax.ShapeDtypeStruct(w.shape, BF16) for w in side_weights],
        compiler_params=_cparams(("arbitrary", "arbitrary")),
        name="attn_shifted" if shifted else "attn",
    )(q_t, k, v_t, *side_weights)
    return outs[0], tuple(outs[1:])


def _conv_kernel(x_ref, w_ref, b_ref, o_ref, pad_ref, *, rows, out_scale, transpose_out):
    s = x_ref.shape[0]
    zeros = jnp.zeros((CONV_HALO, LANES), F32)
    pad_ref[0:CONV_HALO, :] = zeros
    pad_ref[CONV_HALO + s:CONV_HALO + s + CONV_HALO, :] = zeros
    pad_ref[CONV_HALO:CONV_HALO + s, :] = x_ref[...].astype(F32)
    w = w_ref[...]
    b = b_ref[...]
    for r in range(s // rows):
        base = r * rows
        acc = b
        for j in range(CONV_WIDTH):
            off = CONV_HALO + j - CONV_WIDTH // 2
            acc = acc + w[j:j + 1, :] * pad_ref[base + off:base + off + rows, :]
        y = acc * jax.nn.sigmoid(acc) * out_scale
        if transpose_out:
            o_ref[:, base:base + rows] = y.T.astype(BF16)
        else:
            o_ref[base:base + rows, :] = y.astype(BF16)


def _conv(proj, conv_w, conv_b, *, first_tile, n_tiles, out_scale, transpose_out, rows=256):
    s = proj.shape[0]
    n_ch = n_tiles * LANES
    return pl.pallas_call(
        functools.partial(_conv_kernel, rows=rows, out_scale=out_scale, transpose_out=transpose_out),
        grid=(n_tiles,),
        in_specs=[pl.BlockSpec((s, LANES), lambda j: (0, first_tile + j)),
                  pl.BlockSpec((CONV_WIDTH, LANES), lambda j: (0, first_tile + j)),
                  pl.BlockSpec((1, LANES), lambda j: (0, first_tile + j))],
        out_specs=pl.BlockSpec((LANES, s), lambda j: (j, 0)) if transpose_out
                  else pl.BlockSpec((s, LANES), lambda j: (0, j)),
        out_shape=jax.ShapeDtypeStruct((n_ch, s) if transpose_out else (s, n_ch), BF16),
        scratch_shapes=[pltpu.VMEM((s + 2 * CONV_HALO, LANES), F32)],
        compiler_params=_cparams(("arbitrary",)),
        name="conv_k" if transpose_out else "conv_q",
    )(proj, conv_w, conv_b)


def _dot_f32_by_mask(mask01, x):
    hi = x.astype(BF16)
    rest = x - hi.astype(F32)
    mid = rest.astype(BF16)
    lo = (rest - mid.astype(F32)).astype(BF16)
    return (jnp.dot(mask01, hi, preferred_element_type=F32) + jnp.dot(mask01, mid, preferred_element_type=F32)
            + jnp.dot(mask01, lo, preferred_element_type=F32))


def _scan_max_rows(x, row_id, reverse):
    n, s = x.shape[0], 1
    while s < n:
        shifted = pltpu.roll(x, n - s if reverse else s, 0)
        ok = (row_id < n - s) if reverse else (row_id >= s)
        x = jnp.maximum(x, jnp.where(ok, shifted, -jnp.inf))
        s *= 2
    return x


def _mlstm_kernel(qf_ref, kf_ref, vf_ref, gf_ref, qb_ref, kb_ref, vb_ref, gb_ref, bg_ref,
                  hf_ref, hb_ref, c_ref, m_ref):
    @pl.when(pl.program_id(0) == 0)
    def _():
        c_ref[...] = jnp.zeros(c_ref.shape, F32)
        m_ref[...] = jnp.full(m_ref.shape, M_INIT, F32)

    L = MLSTM_CHUNK
    row = lax.broadcasted_iota(jnp.int32, (L, L), 0)
    col = lax.broadcasted_iota(jnp.int32, (L, L), 1)
    row_id = lax.broadcasted_iota(jnp.int32, (L, LANES), 0)
    ones_col = (lax.broadcasted_iota(jnp.int32, (L, LANES), 1) == 0).astype(BF16)
    hm = N_MLSTM_HEADS
    for d, (q_ref, k_ref, v_ref, g_ref, o_ref) in enumerate(
            ((qf_ref, kf_ref, vf_ref, gf_ref, hf_ref), (qb_ref, kb_ref, vb_ref, gb_ref, hb_ref))):
        backward = d == 1
        mask = (row <= col) if backward else (row >= col)
        g = g_ref[...] + bg_ref[...]
        lf = jax.nn.log_sigmoid(g)
        b_c = _dot_f32_by_mask(mask.astype(BF16), lf)
        to_lane0 = lambda t, first: t if first == 0 else pltpu.roll(t, LANES - first, 1)
        log_i = to_lane0(g, 2 * d * hm)
        b = to_lane0(b_c, (2 * d + 1) * hm)
        b_tot = jnp.sum(to_lane0(lf, (2 * d + 1) * hm), axis=0, keepdims=True)
        a = log_i - b
        m_prev = m_ref[d:d + 1, :]
        mx = jnp.maximum(m_prev, _scan_max_rows(a, row_id, backward))
        u2 = mx * (-LOG2_E)
        w_inter = jnp.exp(m_prev - mx)
        floor = jnp.exp(-(b + mx))
        mn = jnp.maximum(m_prev, jnp.max(a, axis=0, keepdims=True))
        decay = jnp.exp(m_prev - mn)
        m_ref[d:d + 1, :] = b_tot + mn
        a_t = a.T
        for h in range(hm):
            sl = slice(h * MLSTM_DH, (h + 1) * MLSTM_DH)
            chain = d * hm + h
            q, k_t = q_ref[:, sl], k_ref[sl, :]
            v_aug = jnp.concatenate([v_ref[:, sl], ones_col], axis=-1)
            a_row = a_t[h:h + 1, :]
            d_m = jnp.where(mask, jnp.exp2(u2[:, h:h + 1] + a_row * LOG2_E), 0.0)
            c_aug = c_ref[chain]
            scores = jnp.dot(q, k_t, preferred_element_type=F32) * d_m
            r = (jnp.dot(scores.astype(BF16), v_aug, preferred_element_type=F32)
                 + w_inter[:, h:h + 1] * jnp.dot(q, c_aug.astype(BF16), preferred_element_type=F32))
            num, den = r[:, :MLSTM_DH], r[:, MLSTM_DH:MLSTM_DH + 1]
            o_ref[:, sl] = (num / jnp.maximum(jnp.abs(den), floor[:, h:h + 1])).astype(o_ref.dtype)

            kw_t = (k_t.astype(F32) * jnp.exp(a_row - mn[:, h:h + 1])).astype(BF16)
            c_ref[chain] = decay[:, h:h + 1] * c_aug + jnp.dot(kw_t, v_aug, preferred_element_type=F32)


def _mlstm(q, k_t, proj, gates, bg):
    s = q.shape[0]
    L = MLSTM_CHUNK
    nc = s // L
    w = N_MLSTM_HEADS * MLSTM_DH
    fwd = lambda blk: (lambda i: (i, blk))
    bwd = lambda blk: (lambda i: (nc - 1 - i, blk))
    specs = lambda ix: [pl.BlockSpec((L, w), ix(0)),
                        pl.BlockSpec((w, L), lambda i: ix(0)(i)[::-1]),
                        pl.BlockSpec((L, w), ix(PROJ_V_M // w)),
                        pl.BlockSpec((L, LANES), ix(0))]
    return pl.pallas_call(
        _mlstm_kernel,
        grid=(nc,),
        in_specs=specs(fwd) + specs(bwd) + [pl.BlockSpec((1, LANES), lambda i: (0, 0))],
        out_specs=[pl.BlockSpec((L, w), fwd(0)), pl.BlockSpec((L, w), bwd(0))],
        out_shape=[jax.ShapeDtypeStruct((s, w), BF16), jax.ShapeDtypeStruct((s, w), BF16)],
        scratch_shapes=[pltpu.VMEM((2 * N_MLSTM_HEADS, MLSTM_DH, MLSTM_AUG), F32),
                        pltpu.VMEM((2 * N_MLSTM_HEADS, LANES), F32)],
        compiler_params=_cparams(("arbitrary",)),
        name="mlstm",
    )(q, k_t, proj, gates, q, k_t, proj, gates, bg)


def _outproj_kernel(attn_ref, hf_ref, hb_ref, om_ref, x_ref, gm_ref, w_ref, gate_ref, g2_ref, sc_ref, sh_ref,
                    x1_ref, h2_ref):
    hsum = hf_ref[...].astype(F32) + hb_ref[...].astype(F32)
    gm = gm_ref[...]
    parts = []
    for h in range(N_MLSTM_HEADS):
        sl = slice(h * MLSTM_DH, (h + 1) * MLSTM_DH)
        seg = hsum[:, sl]
        parts.append(seg * _rms_scale(seg, MLSTM_DH) * gm[:, sl])
    ml = (jax.nn.sigmoid(om_ref[...].astype(F32)) * jnp.concatenate(parts, axis=-1)).astype(BF16)
    n_attn = attn_ref.shape[1]
    mixed = (jnp.dot(attn_ref[...], w_ref[:n_attn, :], preferred_element_type=F32)
             + jnp.dot(ml, w_ref[n_attn:, :], preferred_element_type=F32))
    x1 = x_ref[...] + gate_ref[...] * mixed
    x1_ref[...] = x1
    h2_ref[...] = _modulated_norm(x1, g2_ref[...], sc_ref[...], sh_ref[...]).astype(BF16)


def _outproj(attn, hf, hb, proj, x, gm, w_out, mod, g2, tm=512):
    s, d = x.shape
    wm = hf.shape[1]
    row = lambda blk: pl.BlockSpec((1, d), lambda i: (0, blk))
    return pl.pallas_call(
        _outproj_kernel,
        grid=(s // tm,),
        in_specs=[pl.BlockSpec((tm, attn.shape[1]), lambda i: (i, 0)),
                  pl.BlockSpec((tm, wm), lambda i: (i, 0)),
                  pl.BlockSpec((tm, wm), lambda i: (i, 0)),
                  pl.BlockSpec((tm, wm), lambda i: (i, PROJ_O_M // wm)),
                  pl.BlockSpec((tm, d), lambda i: (i, 0)),
                  pl.BlockSpec((1, wm), lambda i: (0, 0)),
                  pl.BlockSpec(w_out.shape, lambda i: (0, 0), pipeline_mode=pl.Buffered(1)),
                  row(MOD_GATE1), row(0), row(MOD_SCALE2), row(MOD_SHIFT2)],
        out_specs=[pl.BlockSpec((tm, d), lambda i: (i, 0)), pl.BlockSpec((tm, d), lambda i: (i, 0))],
        out_shape=[jax.ShapeDtypeStruct((s, d), F32), jax.ShapeDtypeStruct((s, d), BF16)],
        compiler_params=_cparams(("arbitrary",)),
        name="outproj",
    )(attn, hf, hb, proj, x, gm, w_out, mod, g2, mod, mod)


def _ffn_kernel(h2_ref, w1_ref, w2_ref, x1_hbm, gate_ref, o_ref, x1_buf, x1_sem, *, f_last):
    m, f = pl.program_id(0), pl.program_id(1)
    tm = o_ref.shape[0]
    x1_copy = pltpu.make_async_copy(x1_hbm.at[pl.ds(pl.multiple_of(m * tm, tm), tm), :], x1_buf, x1_sem)

    def hidden_tile():
        a = jnp.maximum(jnp.dot(h2_ref[...], w1_ref[...], preferred_element_type=F32), 0.0)
        return jnp.dot((a * a).astype(BF16), w2_ref[...], preferred_element_type=F32)

    @pl.when(f == 0)
    def _():
        x1_copy.start()
        o_ref[...] = hidden_tile()

    @pl.when(f > 0)
    def _():
        o_ref[...] += hidden_tile()

    @pl.when(f == f_last)
    def _():
        x1_copy.wait()
        o_ref[...] = x1_buf[...] + gate_ref[...] * o_ref[...]


def _ffn(h2, w1, w2, x1, mod, tm=1024, tf=1024):
    s, d = h2.shape
    dff = w1.shape[1]
    return pl.pallas_call(
        functools.partial(_ffn_kernel, f_last=dff // tf - 1),
        grid=(s // tm, dff // tf),
        in_specs=[pl.BlockSpec((tm, d), lambda m, f: (m, 0)),
                  pl.BlockSpec((d, tf), lambda m, f: (0, f)),
                  pl.BlockSpec((tf, d), lambda m, f: (f, 0)),
                  pl.BlockSpec(memory_space=pl.ANY),
                  pl.BlockSpec((1, d), lambda m, f: (0, MOD_GATE2))],
        out_specs=pl.BlockSpec((tm, d), lambda m, f: (m, 0)),
        out_shape=jax.ShapeDtypeStruct((s, d), F32),
        scratch_shapes=[pltpu.VMEM((tm, d), F32), pltpu.SemaphoreType.DMA(())],
        compiler_params=_cparams(("arbitrary", "arbitrary")),
        name="ffn",
    )(h2, w1, w2, x1, mod)


PACK_TILE = 512
PACK_SRC = 64
PACK_NSRC = PACK_TILE // PACK_SRC
W_IN_C_KV_START, W_IN_K_PE_START, W_IN_MIX_START, W_IN_GATES_START = 512, 768, 832, 4928
N_GATE_COLS = 16


def _pack_src_blocks():
    rows = []
    n_mix = 4 * N_MLSTM_HEADS * MLSTM_DH // PACK_TILE
    for n in range(n_mix):
        first = (W_IN_MIX_START + n * PACK_TILE) // PACK_SRC
        rows.append([first + i for i in range(PACK_NSRC)])
    rows.append(list(range(PACK_NSRC)))
    gates_blk = W_IN_GATES_START // PACK_SRC
    c_kv = list(range(W_IN_C_KV_START // PACK_SRC, W_IN_K_PE_START // PACK_SRC))
    k_pe = W_IN_K_PE_START // PACK_SRC
    rows.append(c_kv + [k_pe, k_pe, gates_blk, gates_blk])
    return jnp.asarray(rows, jnp.int32).reshape(-1)


def _pack_kernel(tbl_ref, *refs, n_mix):
    del tbl_ref
    srcs, o_ref = refs[:-1], refs[-1]
    n = pl.program_id(0)
    blk = lambda i: slice(i * PACK_SRC, (i + 1) * PACK_SRC)

    @pl.when(n <= n_mix)
    def _():
        for i, b in enumerate(srcs):
            o_ref[blk(i), :] = b[...].astype(BF16)

    @pl.when(n == n_mix + 1)
    def _():
        zeros = jnp.zeros(srcs[0].shape, BF16)
        for i in range(5):
            o_ref[blk(i), :] = srcs[i][...].astype(BF16)
        o_ref[blk(5), :] = zeros
        rows = lax.broadcasted_iota(jnp.int32, srcs[6].shape, 0)
        o_ref[blk(6), :] = jnp.where(rows < N_GATE_COLS, srcs[6][...], 0.0).astype(BF16)
        o_ref[blk(7), :] = zeros


def _pack_w_in(w_in_t):
    d = w_in_t.shape[1]
    tbl = _pack_src_blocks()
    n_tiles = tbl.shape[0] // PACK_NSRC
    src = lambda i: pl.BlockSpec((PACK_SRC, d), lambda n, t: (t[n * PACK_NSRC + i], 0))
    return pl.pallas_call(
        functools.partial(_pack_kernel, n_mix=n_tiles - 2),
        grid_spec=pltpu.PrefetchScalarGridSpec(
            num_scalar_prefetch=1, grid=(n_tiles,),
            in_specs=[src(i) for i in range(PACK_NSRC)],
            out_specs=pl.BlockSpec((PACK_TILE, d), lambda n, t: (n, 0))),
        out_shape=jax.ShapeDtypeStruct((n_tiles * PACK_TILE, d), BF16),
        compiler_params=_cparams(("arbitrary",)),
        name="packw",
    )(tbl, *([w_in_t] * PACK_NSRC))


def _pad_lanes(v, n):
    return jnp.pad(v, ((0, 0), (0, n - v.shape[1])))


def kernel(x, c, positions, w_ada, b_ada, norm_mix_g, w_in, b_gates, conv_w, conv_b, q_lora_g, w_uq,
           kv_lora_g, w_ukv, q_norm_g, k_norm_g, mlstm_norm_g, w_out, norm_mlp_g, w_ff1, w_ff2):
    bsz, s, d = x.shape
    assert bsz == 1, "kernels are written for a single sequence"
    xs = x[0]
    pos_row = positions.reshape(1, s)
    half = jnp.arange(ROPE_HALF, dtype=F32)
    freq = (ROPE_THETA ** (-half / ROPE_HALF)).reshape(ROPE_HALF, 1)
    row = lambda v: v.reshape(1, -1).astype(F32)

    for l in range(w_ada.shape[0]):
        mod = _mod(c.reshape(d, 1), w_ada[l], row(b_ada[l]))
        proj, gates = _inproj(xs, row(norm_mix_g[l]), mod, _pack_w_in(w_in[l].T))

        wuq_t = jnp.pad(w_uq[l].reshape(-1, N_MLA_HEADS, MLA_QK), ((0, 0), (0, 0), (0, MLA_QK_PAD - MLA_QK)))
        wuq_t = wuq_t.reshape(-1, N_MLA_HEADS * MLA_QK_PAD).T.astype(BF16)
        wkv = w_ukv[l].reshape(-1, N_MLA_HEADS, MLA_NOPE + MLA_V)
        wk = wkv[:, :, :MLA_NOPE].reshape(-1, N_MLA_HEADS * MLA_NOPE).astype(BF16)
        wv_t = wkv[:, :, MLA_NOPE:].reshape(-1, N_MLA_HEADS * MLA_V).T.astype(BF16)
        g_q, g_k = row(q_norm_g[l]), row(k_norm_g[l])
        q_t, k, v_t = _mlaprep(proj, pos_row, freq, row(q_lora_g[l]), row(kv_lora_g[l]), wuq_t, wk, wv_t,
                               g_q.reshape(-1, 1), _pad_lanes(g_k, MLA_QK_PAD))
        logit_bound = 1.02 * LOG2_E * MLA_QK ** 0.5 * jnp.max(jnp.abs(g_q)) * jnp.max(jnp.abs(g_k))
        attn, (w_out_b, w_ff1_b, w_ff2_b) = lax.cond(
            logit_bound <= SAFE_LOG2,
            functools.partial(_attention, shifted=False), functools.partial(_attention, shifted=True),
            q_t, k, v_t, (w_out[l], w_ff1[l], w_ff2[l]))

        n_qk_tiles = N_MLSTM_HEADS * MLSTM_DH // LANES
        assert (PROJ_Q_M, PROJ_K_M) == (0, n_qk_tiles * LANES)
        q_m = _conv(proj, conv_w[l], row(conv_b[l]), first_tile=0, n_tiles=n_qk_tiles, out_scale=1.0,
                    transpose_out=False)
        k_m_t = _conv(proj, conv_w[l], row(conv_b[l]), first_tile=n_qk_tiles, n_tiles=n_qk_tiles,
                      out_scale=MLSTM_DH ** -0.5, transpose_out=True)
        hf, hb = _mlstm(q_m, k_m_t, proj, gates, _pad_lanes(row(b_gates[l]), LANES))

        x1, h2 = _outproj(attn, hf, hb, proj, xs, row(mlstm_norm_g[l]), w_out_b, mod, row(norm_mlp_g[l]))
        xs = _ffn(h2, w_ff1_b, w_ff2_b, x1, mod)
    return xs[None]
```

```python
import functools

import jax
import jax.numpy as jnp
from jax import lax
from jax.experimental import pallas as pl
from jax.experimental.pallas import tpu as pltpu

F32 = jnp.float32
BF16 = jnp.bfloat16

LANES = 128
N_MLA_HEADS = 8
MLA_NOPE = 128
ROPE_DIM = 64
ROPE_HALF = ROPE_DIM // 2
MLA_QK = MLA_NOPE + ROPE_DIM
MLA_QK_PAD = 256
MLA_V = 128
ATTN_TK = 256
Q_LORA, KV_LORA = 512, 256
PROJ_Q_M, PROJ_K_M, PROJ_V_M, PROJ_O_M = 0, 1024, 2048, 3072
PROJ_C_Q, PROJ_C_KV, PROJ_K_PE, PROJ_GATES = 4096, 4608, 4864, 4992
MOD_SHIFT1, MOD_SCALE1, MOD_GATE1, MOD_SHIFT2, MOD_SCALE2, MOD_GATE2 = range(6)
ROPE_THETA = 10000.0
N_MLSTM_HEADS = 4
MLSTM_DH = 256
MLSTM_CHUNK = 256
MLSTM_AUG = MLSTM_DH + LANES
CONV_WIDTH = 5
CONV_HALO = 8
EPS = 1e-6
M_INIT = -1e30
LOG2_E = 1.4426950408889634
SAFE_LOG2 = 60.0
VMEM_LIMIT = 56 * 1024 * 1024


def _cparams(sem):
    return pltpu.CompilerParams(dimension_semantics=sem, vmem_limit_bytes=VMEM_LIMIT)


def _mod_kernel(c_ref, w_ref, b_ref, o_ref, sb_ref, *, tn):
    @pl.when(pl.program_id(0) == 0)
    def _():
        cc = c_ref[...]
        sb_ref[...] = jnp.broadcast_to(cc * jax.nn.sigmoid(cc), sb_ref.shape)

    sb = sb_ref[...]
    for j in range(tn // LANES):
        sl = slice(j * LANES, (j + 1) * LANES)
        o_ref[:, sl] = jnp.sum(w_ref[:, sl] * sb, axis=0, keepdims=True) + b_ref[:, sl]


def _mod(c_col, w_ada, b_ada, tn=2048):
    d, n = w_ada.shape
    return pl.pallas_call(
        functools.partial(_mod_kernel, tn=tn),
        grid=(n // tn,),
        in_specs=[pl.BlockSpec((d, 1), lambda j: (0, 0)),
                  pl.BlockSpec((d, tn), lambda j: (0, j)),
                  pl.BlockSpec((1, tn), lambda j: (0, j))],
        out_specs=pl.BlockSpec((1, tn), lambda j: (0, j)),
        out_shape=jax.ShapeDtypeStruct((1, n), F32),
        scratch_shapes=[pltpu.VMEM((d, LANES), F32)],
        compiler_params=_cparams(("arbitrary",)),
        name="mod",
    )(c_col, w_ada, b_ada)


def _modulated_norm(x, g, scale, shift):
    ms = jnp.mean(x * x, axis=-1, keepdims=True)
    return (x * lax.rsqrt(ms + EPS) * g) * (1.0 + scale) + shift


def _inproj_kernel(x_ref, g_ref, sc_ref, sh_ref, w_ref, o_ref, gate_ref, h_ref, *, n_last):
    n = pl.program_id(1)
    project = lambda h: lax.dot_general(h, w_ref[...], (((1,), (1,)), ((), ())), preferred_element_type=F32)

    @pl.when(n == 0)
    def _():
        h = _modulated_norm(x_ref[...], g_ref[...], sc_ref[...], sh_ref[...]).astype(BF16)
        h_ref[...] = h
        o_ref[...] = project(h).astype(BF16)

    @pl.when(n > 0)
    def _():
        acc = project(h_ref[...])
        o_ref[...] = acc.astype(BF16)

        @pl.when(n == n_last)
        def _():
            gate_ref[...] = acc[:, -LANES:]


def _inproj(x, g, mod, w_t, tm=1024, tn=1024):
    s, d = x.shape
    n_tot = w_t.shape[0]
    return pl.pallas_call(
        functools.partial(_inproj_kernel, n_last=n_tot // tn - 1),
        grid=(s // tm, n_tot // tn),
        in_specs=[pl.BlockSpec((tm, d), lambda m, n: (m, 0)),
                  pl.BlockSpec((1, d), lambda m, n: (0, 0)),
                  pl.BlockSpec((1, d), lambda m, n: (0, MOD_SCALE1)),
                  pl.BlockSpec((1, d), lambda m, n: (0, MOD_SHIFT1)),
                  pl.BlockSpec((tn, d), lambda m, n: (n, 0))],
        out_specs=[pl.BlockSpec((tm, tn), lambda m, n: (m, n)),
                   pl.BlockSpec((tm, LANES), lambda m, n: (m, 0))],
        out_shape=[jax.ShapeDtypeStruct((s, n_tot), BF16),
                   jax.ShapeDtypeStruct((s, LANES), F32)],
        scratch_shapes=[pltpu.VMEM((tm, d), BF16)],
        compiler_params=_cparams(("arbitrary", "arbitrary")),
        name="inproj",
    )(x, g, mod, mod, w_t)


def _rms_scale(x, n):
    return lax.rsqrt(jnp.sum(x * x, axis=-1, keepdims=True) * (1.0 / n) + EPS)


def _mlaprep_kernel(cq_ref, ckv_ref, kpe_ref, pos_ref, freq_ref, gql_ref, gkvl_ref, wuqt_ref, wk_ref, wvt_ref,
                    gqc_ref, gk_ref, q_ref, k_ref, v_ref):
    tm = cq_ref.shape[0]
    nt = (((1,), (1,)), ((), ()))
    cq = cq_ref[...].astype(F32)
    cqn = (cq * _rms_scale(cq, cq.shape[-1]) * gql_ref[...]).astype(BF16)
    ckv = ckv_ref[...].astype(F32)
    ckvn = (ckv * _rms_scale(ckv, ckv.shape[-1]) * gkvl_ref[...]).astype(BF16)
    qf_t = lax.dot_general(wuqt_ref[...], cqn, nt, preferred_element_type=F32)
    kf = jnp.dot(ckvn, wk_ref[...], preferred_element_type=F32)
    vf_t = lax.dot_general(wvt_ref[...], ckvn, nt, preferred_element_type=F32)

    ang_t = freq_ref[...] * pos_ref[...].astype(F32)
    cos_t, sin_t = jnp.cos(ang_t), jnp.sin(ang_t)

    gq_b = jnp.broadcast_to(gqc_ref[...] * (LOG2_E * MLA_QK ** -0.5), (MLA_QK, tm))
    q_pad = jnp.zeros((MLA_QK_PAD - MLA_QK, tm), F32)
    for h in range(N_MLA_HEADS):
        qh = qf_t[h * MLA_QK_PAD:h * MLA_QK_PAD + MLA_QK]
        r = lax.rsqrt(jnp.sum(qh * qh, axis=0, keepdims=True) * (1.0 / MLA_QK) + EPS)
        qn = qh * r * gq_b
        x1, x2 = qn[MLA_NOPE:MLA_NOPE + ROPE_HALF], qn[MLA_NOPE + ROPE_HALF:]
        q_ref[h] = jnp.concatenate([qn[:MLA_NOPE], x1 * cos_t - x2 * sin_t, x1 * sin_t + x2 * cos_t, q_pad],
                                   axis=0).astype(BF16)

    for h in range(N_MLA_HEADS):
        for c in range(tm // ATTN_TK):
            v_ref[h, c] = vf_t[h * MLA_V:(h + 1) * MLA_V, c * ATTN_TK:(c + 1) * ATTN_TK].astype(BF16)

    z_half, z_pad = jnp.zeros_like(cos_t), jnp.zeros((LANES - ROPE_DIM, tm), F32)
    c_tab = jnp.concatenate([cos_t, cos_t, z_pad], axis=0).T
    s_up = jnp.concatenate([z_half, sin_t, z_pad], axis=0).T
    s_dn = jnp.concatenate([-sin_t, z_half, z_pad], axis=0).T
    gk = gk_ref[...]
    kpe = kpe_ref[...].astype(F32)
    kpe_g = kpe * gk[:, MLA_NOPE:]
    kpe_rot = (kpe_g * c_tab + pltpu.roll(kpe_g, ROPE_HALF, 1) * s_up
               + pltpu.roll(kpe_g, LANES - ROPE_HALF, 1) * s_dn)
    kpe_ssq = jnp.sum(kpe * kpe, axis=-1, keepdims=True)
    for h in range(N_MLA_HEADS):
        kn = kf[:, h * MLA_NOPE:(h + 1) * MLA_NOPE]
        r = lax.rsqrt((jnp.sum(kn * kn, axis=-1, keepdims=True) + kpe_ssq) * (1.0 / MLA_QK) + EPS)
        k_ref[h] = jnp.concatenate([kn * r * gk[:, :MLA_NOPE], kpe_rot * r], axis=-1).astype(BF16)


def _mlaprep(proj, pos_row, freq, gql, gkvl, wuq_t, wk, wv_t, gq_col, gk, tm=1024):
    s = proj.shape[0]
    hq = N_MLA_HEADS
    full = lambda a: pl.BlockSpec(a.shape, lambda i: (0,) * a.ndim)
    return pl.pallas_call(
        _mlaprep_kernel,
        grid=(s // tm,),
        in_specs=[pl.BlockSpec((tm, Q_LORA), lambda i: (i, PROJ_C_Q // Q_LORA)),
                  pl.BlockSpec((tm, KV_LORA), lambda i: (i, PROJ_C_KV // KV_LORA)),
                  pl.BlockSpec((tm, LANES), lambda i: (i, PROJ_K_PE // LANES)),
                  pl.BlockSpec((1, tm), lambda i: (0, i)),
                  full(freq), full(gql), full(gkvl), full(wuq_t), full(wk), full(wv_t), full(gq_col), full(gk)],
        out_specs=[pl.BlockSpec((hq, MLA_QK_PAD, tm), lambda i: (0, 0, i)),
                   pl.BlockSpec((hq, tm, MLA_QK_PAD), lambda i: (0, i, 0)),
                   pl.BlockSpec((hq, tm // ATTN_TK, MLA_V, ATTN_TK), lambda i: (0, i, 0, 0))],
        out_shape=[jax.ShapeDtypeStruct((hq, MLA_QK_PAD, s), BF16),
                   jax.ShapeDtypeStruct((hq, s, MLA_QK_PAD), BF16),
                   jax.ShapeDtypeStruct((hq, s // ATTN_TK, MLA_V, ATTN_TK), BF16)],
        compiler_params=_cparams(("arbitrary",)),
        name="mlaprep",
    )(proj, proj, proj, pos_row, freq, gql, gkvl, wuq_t, wk, wv_t, gq_col, gk)


def _attn_kernel(qt_ref, k_ref, vt_ref, *refs, n_side, shifted, tk, ahead, n_sub):
    side_in, o_ref, side_out = refs[:n_side], refs[n_side], refs[n_side + 1:]
    for src, dst in zip(side_in, side_out):
        dst[...] = src[...].astype(BF16)

    n_chunks = k_ref.shape[1] // tk
    tq = qt_ref.shape[2] // n_sub

    def logits_t(item):
        t, j = item
        return jnp.dot(k_ref[0, j * tk:(j + 1) * tk, :], qt_ref[0, :, t * tq:(t + 1) * tq],
                       preferred_element_type=F32)

    def values_t(j, p_t):
        return jnp.dot(vt_ref[0, j], p_t.astype(BF16), preferred_element_type=F32)

    items = [(t, j) for t in range(n_sub) for j in range(n_chunks)]
    pending = [logits_t(it) for it in items[:ahead]]
    for i, (t, j) in enumerate(items):
        if i + ahead < len(items):
            pending.append(logits_t(items[i + ahead]))
        s2 = pending.pop(0)
        if j == 0:
            m, acc, l = jnp.full((1, tq), -jnp.inf, F32), None, None
        if shifted:
            m_new = jnp.maximum(m, jnp.max(s2, axis=0, keepdims=True))
            p_t, alpha = jnp.exp2(s2 - m_new), jnp.exp2(m - m_new)
            pv, ps = values_t(j, p_t), jnp.sum(p_t, axis=0, keepdims=True)
            acc, l = (pv, ps) if acc is None else (acc * alpha + pv, l * alpha + ps)
            m = m_new
        else:
            p_t = jnp.exp2(s2)
            pv, ps = values_t(j, p_t), jnp.sum(p_t, axis=0, keepdims=True)
            acc, l = (pv, ps) if acc is None else (acc + pv, l + ps)
        if j == n_chunks - 1:
            o_ref[t * tq:(t + 1) * tq, :] = (acc / l).T.astype(BF16)


def _attention(q_t, k, v_t, side_weights, *, shifted, tq=1024, n_sub=2, tk=ATTN_TK, ahead=2):
    hq, s, _ = k.shape
    n_q = s // tq
    n_steps = hq * n_q
    slab = lambda w: pl.BlockSpec((w.shape[0] // n_steps, w.shape[1]), lambda h, i: (h * n_q + i, 0))
    outs = pl.pallas_call(
        functools.partial(_attn_kernel, n_side=len(side_weights), shifted=shifted, tk=tk, ahead=ahead,
                          n_sub=n_sub),
        grid=(hq, n_q),
        in_specs=[pl.BlockSpec((1, MLA_QK_PAD, tq), lambda h, i: (h, 0, i)),
                  pl.BlockSpec((1, s, MLA_QK_PAD), lambda h, i: (h, 0, 0)),
                  pl.BlockSpec((1, s // tk, MLA_V, tk), lambda h, i: (h, 0, 0, 0))]
                 + [slab(w) for w in side_weights],
        out_specs=[pl.BlockSpec((tq, MLA_V), lambda h, i: (i, h))] + [slab(w) for w in side_weights],
        out_shape=[jax.ShapeDtypeStruct((s, hq * MLA_V), BF16)]
                  + [jax.ShapeDtypeStruct(w.shape, BF16) for w in side_weights],
        compiler_params=_cparams(("arbitrary", "arbitrary")),
        name="attn_shifted" if shifted else "attn",
    )(q_t, k, v_t, *side_weights)
    return outs[0], tuple(outs[1:])


def _conv_kernel(x_ref, w_ref, b_ref, o_ref, pad_ref, *, rows, out_scale, transpose_out):
    s = x_ref.shape[0]
    zeros = jnp.zeros((CONV_HALO, LANES), F32)
    pad_ref[0:CONV_HALO, :] = zeros
    pad_ref[CONV_HALO + s:CONV_HALO + s + CONV_HALO, :] = zeros
    pad_ref[CONV_HALO:CONV_HALO + s, :] = x_ref[...].astype(F32)
    w = w_ref[...]
    b = b_ref[...]
    for r in range(s // rows):
        base = r * rows
        acc = b
        for j in range(CONV_WIDTH):
            off = CONV_HALO + j - CONV_WIDTH // 2
            acc = acc + w[j:j + 1, :] * pad_ref[base + off:base + off + rows, :]
        half = acc * (0.5 * out_scale)
        y = half + half * jnp.tanh(acc * 0.5)
        if transpose_out:
            o_ref[:, base:base + rows] = y.T.astype(BF16)
        else:
            o_ref[base:base + rows, :] = y.astype(BF16)


def _conv(proj, conv_w, conv_b, *, first_tile, n_tiles, out_scale, transpose_out, rows=256):
    s = proj.shape[0]
    n_ch = n_tiles * LANES
    return pl.pallas_call(
        functools.partial(_conv_kernel, rows=rows, out_scale=out_scale, transpose_out=transpose_out),
        grid=(n_tiles,),
        in_specs=[pl.BlockSpec((s, LANES), lambda j: (0, first_tile + j)),
                  pl.BlockSpec((CONV_WIDTH, LANES), lambda j: (0, first_tile + j)),
                  pl.BlockSpec((1, LANES), lambda j: (0, first_tile + j))],
        out_specs=pl.BlockSpec((LANES, s), lambda j: (j, 0)) if transpose_out
                  else pl.BlockSpec((s, LANES), lambda j: (0, j)),
        out_shape=jax.ShapeDtypeStruct((n_ch, s) if transpose_out else (s, n_ch), BF16),
        scratch_shapes=[pltpu.VMEM((s + 2 * CONV_HALO, LANES), F32)],
        compiler_params=_cparams(("arbitrary",)),
        name="conv_k" if transpose_out else "conv_q",
    )(proj, conv_w, conv_b)


def _scan_rows(x, row_id, reverse, combine, identity):
    n, s = x.shape[0], 1
    while s < n:
        shifted = pltpu.roll(x, n - s if reverse else s, 0)
        ok = (row_id < n - s) if reverse else (row_id >= s)
        x = combine(x, jnp.where(ok, shifted, identity))
        s *= 2
    return x


def _mlstm_kernel(qf_ref, kf_ref, vf_ref, gf_ref, qb_ref, kb_ref, vb_ref, gb_ref, bg_ref,
                  hf_ref, hb_ref, c_ref, m_ref):
    @pl.when(pl.program_id(0) == 0)
    def _():
        c_ref[...] = jnp.zeros(c_ref.shape, F32)
        m_ref[...] = jnp.full(m_ref.shape, M_INIT, F32)

    L = MLSTM_CHUNK
    row = lax.broadcasted_iota(jnp.int32, (L, L), 0)
    col = lax.broadcasted_iota(jnp.int32, (L, L), 1)
    row_id = lax.broadcasted_iota(jnp.int32, (L, LANES), 0)
    ones_col = (lax.broadcasted_iota(jnp.int32, (L, LANES), 1) == 0).astype(BF16)
    hm = N_MLSTM_HEADS
    for d, (q_ref, k_ref, v_ref, g_ref, o_ref) in enumerate(
            ((qf_ref, kf_ref, vf_ref, gf_ref, hf_ref), (qb_ref, kb_ref, vb_ref, gb_ref, hb_ref))):
        backward = d == 1
        mask = (row <= col) if backward else (row >= col)
        g = g_ref[...] + bg_ref[...]
        lf = jax.nn.log_sigmoid(g)
        to_lane0 = lambda t, first: t if first == 0 else pltpu.roll(t, LANES - first, 1)
        log_i = to_lane0(g, 2 * d * hm)
        log_f = to_lane0(lf, (2 * d + 1) * hm)
        b = _scan_rows(log_f, row_id, backward, jnp.add, 0.0)
        b_tot = jnp.sum(log_f, axis=0, keepdims=True)
        a = log_i - b
        m_prev = m_ref[d:d + 1, :]
        mx = jnp.maximum(m_prev, _scan_rows(a, row_id, backward, jnp.maximum, -jnp.inf))
        u2 = mx * (-LOG2_E)
        w_inter = jnp.exp(m_prev - mx)
        floor = jnp.exp(-(b + mx))
        mn = jnp.maximum(m_prev, jnp.max(a, axis=0, keepdims=True))
        decay = jnp.exp(m_prev - mn)
        m_ref[d:d + 1, :] = b_tot + mn
        a_t = a.T
        for h in range(hm):
            sl = slice(h * MLSTM_DH, (h + 1) * MLSTM_DH)
            chain = d * hm + h
            q, k_t = q_ref[:, sl], k_ref[sl, :]
            v_aug = jnp.concatenate([v_ref[:, sl], ones_col], axis=-1)
            a_row = a_t[h:h + 1, :]
            d_m = jnp.where(mask, jnp.exp2(u2[:, h:h + 1] + a_row * LOG2_E), 0.0)
            c_aug = c_ref[chain]
            scores = jnp.dot(q, k_t, preferred_element_type=F32) * d_m
            r = (jnp.dot(scores.astype(BF16), v_aug, preferred_element_type=F32)
                 + w_inter[:, h:h + 1] * jnp.dot(q, c_aug.astype(BF16), preferred_element_type=F32))
            num, den = r[:, :MLSTM_DH], r[:, MLSTM_DH:MLSTM_DH + 1]
            o_ref[:, sl] = (num / jnp.maximum(jnp.abs(den), floor[:, h:h + 1])).astype(o_ref.dtype)

            kw_t = (k_t.astype(F32) * jnp.exp(a_row - mn[:, h:h + 1])).astype(BF16)
            c_ref[chain] = decay[:, h:h + 1] * c_aug + jnp.dot(kw_t, v_aug, preferred_element_type=F32)


def _mlstm(q, k_t, proj, gates, bg):
    s = q.shape[0]
    L = MLSTM_CHUNK
    nc = s // L
    w = N_MLSTM_HEADS * MLSTM_DH
    fwd = lambda blk: (lambda i: (i, blk))
    bwd = lambda blk: (lambda i: (nc - 1 - i, blk))
    specs = lambda ix: [pl.BlockSpec((L, w), ix(0)),
                        pl.BlockSpec((w, L), lambda i: ix(0)(i)[::-1]),
                        pl.BlockSpec((L, w), ix(PROJ_V_M // w)),
                        pl.BlockSpec((L, LANES), ix(0))]
    return pl.pallas_call(
        _mlstm_kernel,
        grid=(nc,),
        in_specs=specs(fwd) + specs(bwd) + [pl.BlockSpec((1, LANES), lambda i: (0, 0))],
        out_specs=[pl.BlockSpec((L, w), fwd(0)), pl.BlockSpec((L, w), bwd(0))],
        out_shape=[jax.ShapeDtypeStruct((s, w), BF16), jax.ShapeDtypeStruct((s, w), BF16)],
        scratch_shapes=[pltpu.VMEM((2 * N_MLSTM_HEADS, MLSTM_DH, MLSTM_AUG), F32),
                        pltpu.VMEM((2 * N_MLSTM_HEADS, LANES), F32)],
        compiler_params=_cparams(("arbitrary",)),
        name="mlstm",
    )(q, k_t, proj, gates, q, k_t, proj, gates, bg)


def _outproj_kernel(attn_ref, hf_ref, hb_ref, om_ref, x_ref, gm_ref, w_ref, gate_ref, g2_ref, sc_ref, sh_ref,
                    x1_ref, h2_ref):
    hsum = hf_ref[...].astype(F32) + hb_ref[...].astype(F32)
    gm = gm_ref[...]
    parts = []
    for h in range(N_MLSTM_HEADS):
        sl = slice(h * MLSTM_DH, (h + 1) * MLSTM_DH)
        seg = hsum[:, sl]
        parts.append(seg * _rms_scale(seg, MLSTM_DH) * gm[:, sl])
    ml = (jax.nn.sigmoid(om_ref[...].astype(F32)) * jnp.concatenate(parts, axis=-1)).astype(BF16)
    n_attn = attn_ref.shape[1]
    mixed = (jnp.dot(attn_ref[...], w_ref[:n_attn, :], preferred_element_type=F32)
             + jnp.dot(ml, w_ref[n_attn:, :], preferred_element_type=F32))
    x1 = x_ref[...] + gate_ref[...] * mixed
    x1_ref[...] = x1
    h2_ref[...] = _modulated_norm(x1, g2_ref[...], sc_ref[...], sh_ref[...]).astype(BF16)


def _outproj(attn, hf, hb, proj, x, gm, w_out, mod, g2, tm=512):
    s, d = x.shape
    wm = hf.shape[1]
    row = lambda blk: pl.BlockSpec((1, d), lambda i: (0, blk))
    return pl.pallas_call(
        _outproj_kernel,
        grid=(s // tm,),
        in_specs=[pl.BlockSpec((tm, attn.shape[1]), lambda i: (i, 0)),
                  pl.BlockSpec((tm, wm), lambda i: (i, 0)),
                  pl.BlockSpec((tm, wm), lambda i: (i, 0)),
                  pl.BlockSpec((tm, wm), lambda i: (i, PROJ_O_M // wm)),
                  pl.BlockSpec((tm, d), lambda i: (i, 0)),
                  pl.BlockSpec((1, wm), lambda i: (0, 0)),
                  pl.BlockSpec(w_out.shape, lambda i: (0, 0), pipeline_mode=pl.Buffered(1)),
                  row(MOD_GATE1), row(0), row(MOD_SCALE2), row(MOD_SHIFT2)],
        out_specs=[pl.BlockSpec((tm, d), lambda i: (i, 0)), pl.BlockSpec((tm, d), lambda i: (i, 0))],
        out_shape=[jax.ShapeDtypeStruct((s, d), F32), jax.ShapeDtypeStruct((s, d), BF16)],
        compiler_params=_cparams(("arbitrary",)),
        name="outproj",
    )(attn, hf, hb, proj, x, gm, w_out, mod, g2, mod, mod)


def _ffn_kernel(h2_ref, w1_ref, w2_ref, x1_hbm, gate_ref, o_ref, x1_buf, x1_sem, *, f_last):
    m, f = pl.program_id(0), pl.program_id(1)
    tm = o_ref.shape[0]
    x1_copy = pltpu.make_async_copy(x1_hbm.at[pl.ds(pl.multiple_of(m * tm, tm), tm), :], x1_buf, x1_sem)

    def hidden_tile():
        a = jnp.maximum(jnp.dot(h2_ref[...], w1_ref[...], preferred_element_type=F32), 0.0)
        return jnp.dot((a * a).astype(BF16), w2_ref[...], preferred_element_type=F32)

    @pl.when(f == 0)
    def _():
        x1_copy.start()
        o_ref[...] = hidden_tile()

    @pl.when(f > 0)
    def _():
        o_ref[...] += hidden_tile()

    @pl.when(f == f_last)
    def _():
        x1_copy.wait()
        o_ref[...] = x1_buf[...] + gate_ref[...] * o_ref[...]


def _ffn(h2, w1, w2, x1, mod, tm=1024, tf=1024):
    s, d = h2.shape
    dff = w1.shape[1]
    return pl.pallas_call(
        functools.partial(_ffn_kernel, f_last=dff // tf - 1),
        grid=(s // tm, dff // tf),
        in_specs=[pl.BlockSpec((tm, d), lambda m, f: (m, 0)),
                  pl.BlockSpec((d, tf), lambda m, f: (0, f)),
                  pl.BlockSpec((tf, d), lambda m, f: (f, 0)),
                  pl.BlockSpec(memory_space=pl.ANY),
                  pl.BlockSpec((1, d), lambda m, f: (0, MOD_GATE2))],
        out_specs=pl.BlockSpec((tm, d), lambda m, f: (m, 0)),
        out_shape=jax.ShapeDtypeStruct((s, d), F32),
        scratch_shapes=[pltpu.VMEM((tm, d), F32), pltpu.SemaphoreType.DMA(())],
        compiler_params=_cparams(("arbitrary", "arbitrary")),
        name="ffn",
    )(h2, w1, w2, x1, mod)


PACK_TILE = 512
PACK_SRC = 64
PACK_NSRC = PACK_TILE // PACK_SRC
W_IN_C_KV_START, W_IN_K_PE_START, W_IN_MIX_START, W_IN_GATES_START = 512, 768, 832, 4928
N_GATE_COLS = 16


def _pack_src_blocks():
    rows = []
    n_mix = 4 * N_MLSTM_HEADS * MLSTM_DH // PACK_TILE
    for n in range(n_mix):
        first = (W_IN_MIX_START + n * PACK_TILE) // PACK_SRC
        rows.append([first + i for i in range(PACK_NSRC)])
    rows.append(list(range(PACK_NSRC)))
    gates_blk = W_IN_GATES_START // PACK_SRC
    c_kv = list(range(W_IN_C_KV_START // PACK_SRC, W_IN_K_PE_START // PACK_SRC))
    k_pe = W_IN_K_PE_START // PACK_SRC
    rows.append(c_kv + [k_pe, k_pe, gates_blk, gates_blk])
    return jnp.asarray(rows, jnp.int32).reshape(-1)


def _pack_kernel(tbl_ref, *refs, n_mix):
    del tbl_ref
    srcs, o_ref = refs[:-1], refs[-1]
    n = pl.program_id(0)
    blk = lambda i: slice(i * PACK_SRC, (i + 1) * PACK_SRC)

    @pl.when(n <= n_mix)
    def _():
        for i, b in enumerate(srcs):
            o_ref[blk(i), :] = b[...].astype(BF16)

    @pl.when(n == n_mix + 1)
    def _():
        zeros = jnp.zeros(srcs[0].shape, BF16)
        for i in range(5):
            o_ref[blk(i), :] = srcs[i][...].astype(BF16)
        o_ref[blk(5), :] = zeros
        rows = lax.broadcasted_iota(jnp.int32, srcs[6].shape, 0)
        o_ref[blk(6), :] = jnp.where(rows < N_GATE_COLS, srcs[6][...], 0.0).astype(BF16)
        o_ref[blk(7), :] = zeros


def _pack_w_in(w_in_t):
    d = w_in_t.shape[1]
    tbl = _pack_src_blocks()
    n_tiles = tbl.shape[0] // PACK_NSRC
    src = lambda i: pl.BlockSpec((PACK_SRC, d), lambda n, t: (t[n * PACK_NSRC + i], 0))
    return pl.pallas_call(
        functools.partial(_pack_kernel, n_mix=n_tiles - 2),
        grid_spec=pltpu.PrefetchScalarGridSpec(
            num_scalar_prefetch=1, grid=(n_tiles,),
            in_specs=[src(i) for i in range(PACK_NSRC)],
            out_specs=pl.BlockSpec((PACK_TILE, d), lambda n, t: (n, 0))),
        out_shape=jax.ShapeDtypeStruct((n_tiles * PACK_TILE, d), BF16),
        compiler_params=_cparams(("arbitrary",)),
        name="packw",
    )(tbl, *([w_in_t] * PACK_NSRC))


def _pad_lanes(v, n):
    return jnp.pad(v, ((0, 0), (0, n - v.shape[1])))


def kernel(x, c, positions, w_ada, b_ada, norm_mix_g, w_in, b_gates, conv_w, conv_b, q_lora_g, w_uq,
           kv_lora_g, w_ukv, q_norm_g, k_norm_g, mlstm_norm_g, w_out, norm_mlp_g, w_ff1, w_ff2):
    bsz, s, d = x.shape
    assert bsz == 1, "kernels are written for a single sequence"
    xs = x[0]
    pos_row = positions.reshape(1, s)
    half = jnp.arange(ROPE_HALF, dtype=F32)
    freq = (ROPE_THETA ** (-half / ROPE_HALF)).reshape(ROPE_HALF, 1)
    row = lambda v: v.reshape(1, -1).astype(F32)

    for l in range(w_ada.shape[0]):
        mod = _mod(c.reshape(d, 1), w_ada[l], row(b_ada[l]))
        proj, gates = _inproj(xs, row(norm_mix_g[l]), mod, _pack_w_in(w_in[l].T))

        wuq_t = jnp.pad(w_uq[l].reshape(-1, N_MLA_HEADS, MLA_QK), ((0, 0), (0, 0), (0, MLA_QK_PAD - MLA_QK)))
        wuq_t = wuq_t.reshape(-1, N_MLA_HEADS * MLA_QK_PAD).T.astype(BF16)
        wkv = w_ukv[l].reshape(-1, N_MLA_HEADS, MLA_NOPE + MLA_V)
        wk = wkv[:, :, :MLA_NOPE].reshape(-1, N_MLA_HEADS * MLA_NOPE).astype(BF16)
        wv_t = wkv[:, :, MLA_NOPE:].reshape(-1, N_MLA_HEADS * MLA_V).T.astype(BF16)
        g_q, g_k = row(q_norm_g[l]), row(k_norm_g[l])
        q_t, k, v_t = _mlaprep(proj, pos_row, freq, row(q_lora_g[l]), row(kv_lora_g[l]), wuq_t, wk, wv_t,
                               g_q.reshape(-1, 1), _pad_lanes(g_k, MLA_QK_PAD))
        logit_bound = 1.02 * LOG2_E * MLA_QK ** 0.5 * jnp.max(jnp.abs(g_q)) * jnp.max(jnp.abs(g_k))
        attn, (w_out_b, w_ff1_b, w_ff2_b) = lax.cond(
            logit_bound <= SAFE_LOG2,
            functools.partial(_attention, shifted=False), functools.partial(_attention, shifted=True),
            q_t, k, v_t, (w_out[l], w_ff1[l], w_ff2[l]))

        n_qk_tiles = N_MLSTM_HEADS * MLSTM_DH // LANES
        assert (PROJ_Q_M, PROJ_K_M) == (0, n_qk_tiles * LANES)
        q_m = _conv(proj, conv_w[l], row(conv_b[l]), first_tile=0, n_tiles=n_qk_tiles, out_scale=1.0,
                    transpose_out=False)
        k_m_t = _conv(proj, conv_w[l], row(conv_b[l]), first_tile=n_qk_tiles, n_tiles=n_qk_tiles,
                      out_scale=MLSTM_DH ** -0.5, transpose_out=True)
        hf, hb = _mlstm(q_m, k_m_t, proj, gates, _pad_lanes(row(b_gates[l]), LANES))

        x1, h2 = _outproj(attn, hf, hb, proj, xs, row(mlstm_norm_g[l]), w_out_b, mod, row(norm_mlp_g[l]))
        xs = _ffn(h2, w_ff1_b, w_ff2_b, x1, mod)
    return xs[None]
```

```python
import functools

import jax
import jax.numpy as jnp
from jax import lax
from jax.experimental import pallas as pl
from jax.experimental.pallas import tpu as pltpu

F32 = jnp.float32
BF16 = jnp.bfloat16

LANES = 128
N_MLA_HEADS = 8
MLA_NOPE = 128
ROPE_DIM = 64
ROPE_HALF = ROPE_DIM // 2
MLA_QK = MLA_NOPE + ROPE_DIM
MLA_QK_PAD = 256
MLA_V = 128
ATTN_TK = 256
Q_LORA, KV_LORA = 512, 256
PROJ_Q_M, PROJ_K_M, PROJ_V_M, PROJ_O_M = 0, 1024, 2048, 3072
PROJ_C_Q, PROJ_C_KV, PROJ_K_PE, PROJ_GATES = 4096, 4608, 4864, 4992
MOD_SHIFT1, MOD_SCALE1, MOD_GATE1, MOD_SHIFT2, MOD_SCALE2, MOD_GATE2 = range(6)
ROPE_THETA = 10000.0
N_MLSTM_HEADS = 4
MLSTM_DH = 256
MLSTM_CHUNK = 256
MLSTM_AUG = MLSTM_DH + LANES
CONV_WIDTH = 5
CONV_HALO = 8
EPS = 1e-6
M_INIT = -1e30
LOG2_E = 1.4426950408889634
SAFE_LOG2 = 60.0
VMEM_LIMIT = 56 * 1024 * 1024


def _cparams(sem):
    return pltpu.CompilerParams(dimension_semantics=sem, vmem_limit_bytes=VMEM_LIMIT)


def _mod_kernel(c_ref, w_ref, b_ref, o_ref, sb_ref, *, tn):
    @pl.when(pl.program_id(0) == 0)
    def _():
        cc = c_ref[...]
        sb_ref[...] = jnp.broadcast_to(cc * jax.nn.sigmoid(cc), sb_ref.shape)

    sb = sb_ref[...]
    for j in range(tn // LANES):
        sl = slice(j * LANES, (j + 1) * LANES)
        o_ref[:, sl] = jnp.sum(w_ref[:, sl] * sb, axis=0, keepdims=True) + b_ref[:, sl]


def _mod(c_col, w_ada, b_ada, tn=1024):
    d, n = w_ada.shape
    return pl.pallas_call(
        functools.partial(_mod_kernel, tn=tn),
        grid=(n // tn,),
        in_specs=[pl.BlockSpec((d, 1), lambda j: (0, 0)),
                  pl.BlockSpec((d, tn), lambda j: (0, j)),
                  pl.BlockSpec((1, tn), lambda j: (0, j))],
        out_specs=pl.BlockSpec((1, tn), lambda j: (0, j)),
        out_shape=jax.ShapeDtypeStruct((1, n), F32),
        scratch_shapes=[pltpu.VMEM((d, LANES), F32)],
        compiler_params=_cparams(("arbitrary",)),
        name="mod",
    )(c_col, w_ada, b_ada)


def _modulated_norm(x, g, scale, shift):
    ms = jnp.mean(x * x, axis=-1, keepdims=True)
    return (x * lax.rsqrt(ms + EPS) * g) * (1.0 + scale) + shift


def _inproj_kernel(x_ref, g_ref, sc_ref, sh_ref, w_ref, o_ref, gate_ref, h_ref, *, n_last):
    n = pl.program_id(1)
    project = lambda h: lax.dot_general(h, w_ref[...], (((1,), (1,)), ((), ())), preferred_element_type=F32)

    @pl.when(n == 0)
    def _():
        h = _modulated_norm(x_ref[...], g_ref[...], sc_ref[...], sh_ref[...]).astype(BF16)
        h_ref[...] = h
        o_ref[...] = project(h).astype(BF16)

    @pl.when(n > 0)
    def _():
        acc = project(h_ref[...])
        o_ref[...] = acc.astype(BF16)

        @pl.when(n == n_last)
        def _():
            gate_ref[...] = acc[:, -LANES:]


def _inproj(x, g, mod, w_t, tm=1024, tn=1024):
    s, d = x.shape
    n_tot = w_t.shape[0]
    return pl.pallas_call(
        functools.partial(_inproj_kernel, n_last=n_tot // tn - 1),
        grid=(s // tm, n_tot // tn),
        in_specs=[pl.BlockSpec((tm, d), lambda m, n: (m, 0)),
                  pl.BlockSpec((1, d), lambda m, n: (0, 0)),
                  pl.BlockSpec((1, d), lambda m, n: (0, MOD_SCALE1)),
                  pl.BlockSpec((1, d), lambda m, n: (0, MOD_SHIFT1)),
                  pl.BlockSpec((tn, d), lambda m, n: (n, 0))],
        out_specs=[pl.BlockSpec((tm, tn), lambda m, n: (m, n)),
                   pl.BlockSpec((tm, LANES), lambda m, n: (m, 0))],
        out_shape=[jax.ShapeDtypeStruct((s, n_tot), BF16),
                   jax.ShapeDtypeStruct((s, LANES), F32)],
        scratch_shapes=[pltpu.VMEM((tm, d), BF16)],
        compiler_params=_cparams(("arbitrary", "arbitrary")),
        name="inproj",
    )(x, g, mod, mod, w_t)


def _rms_scale(x, n):
    return lax.rsqrt(jnp.sum(x * x, axis=-1, keepdims=True) * (1.0 / n) + EPS)


def _mlaprep_kernel(cq_ref, ckv_ref, kpe_ref, pos_ref, freq_ref, gql_ref, gkvl_ref, wuqt_ref, wk_ref, wvt_ref,
                    gqc_ref, gk_ref, q_ref, k_ref, v_ref):
    tm = cq_ref.shape[0]
    nt = (((1,), (1,)), ((), ()))
    cq = cq_ref[...].astype(F32)
    cqn = (cq * _rms_scale(cq, cq.shape[-1]) * gql_ref[...]).astype(BF16)
    ckv = ckv_ref[...].astype(F32)
    ckvn = (ckv * _rms_scale(ckv, ckv.shape[-1]) * gkvl_ref[...]).astype(BF16)
    qf_t = lax.dot_general(wuqt_ref[...], cqn, nt, preferred_element_type=F32)
    kf = jnp.dot(ckvn, wk_ref[...], preferred_element_type=F32)
    vf_t = lax.dot_general(wvt_ref[...], ckvn, nt, preferred_element_type=F32)

    ang_t = freq_ref[...] * pos_ref[...].astype(F32)
    cos_t, sin_t = jnp.cos(ang_t), jnp.sin(ang_t)

    gq_b = jnp.broadcast_to(gqc_ref[...] * (LOG2_E * MLA_QK ** -0.5), (MLA_QK, tm))
    q_pad = jnp.zeros((MLA_QK_PAD - MLA_QK, tm), F32)
    for h in range(N_MLA_HEADS):
        qh = qf_t[h * MLA_QK_PAD:h * MLA_QK_PAD + MLA_QK]
        r = lax.rsqrt(jnp.sum(qh * qh, axis=0, keepdims=True) * (1.0 / MLA_QK) + EPS)
        qn = qh * r * gq_b
        x1, x2 = qn[MLA_NOPE:MLA_NOPE + ROPE_HALF], qn[MLA_NOPE + ROPE_HALF:]
        q_ref[h] = jnp.concatenate([qn[:MLA_NOPE], x1 * cos_t - x2 * sin_t, x1 * sin_t + x2 * cos_t, q_pad],
                                   axis=0).astype(BF16)

    for h in range(N_MLA_HEADS):
        for c in range(tm // ATTN_TK):
            v_ref[h, c] = vf_t[h * MLA_V:(h + 1) * MLA_V, c * ATTN_TK:(c + 1) * ATTN_TK].astype(BF16)

    z_half, z_pad = jnp.zeros_like(cos_t), jnp.zeros((LANES - ROPE_DIM, tm), F32)
    c_tab = jnp.concatenate([cos_t, cos_t, z_pad], axis=0).T
    s_up = jnp.concatenate([z_half, sin_t, z_pad], axis=0).T
    s_dn = jnp.concatenate([-sin_t, z_half, z_pad], axis=0).T
    gk = gk_ref[...]
    kpe = kpe_ref[...].astype(F32)
    kpe_g = kpe * gk[:, MLA_NOPE:]
    kpe_rot = (kpe_g * c_tab + pltpu.roll(kpe_g, ROPE_HALF, 1) * s_up
               + pltpu.roll(kpe_g, LANES - ROPE_HALF, 1) * s_dn)
    kpe_ssq = jnp.sum(kpe * kpe, axis=-1, keepdims=True)
    for h in range(N_MLA_HEADS):
        kn = kf[:, h * MLA_NOPE:(h + 1) * MLA_NOPE]
        r = lax.rsqrt((jnp.sum(kn * kn, axis=-1, keepdims=True) + kpe_ssq) * (1.0 / MLA_QK) + EPS)
        k_ref[h] = jnp.concatenate([kn * r * gk[:, :MLA_NOPE], kpe_rot * r], axis=-1).astype(BF16)


def _mlaprep(proj, pos_row, freq, gql, gkvl, wuq_t, wk, wv_t, gq_col, gk, tm=1024):
    s = proj.shape[0]
    hq = N_MLA_HEADS
    full = lambda a: pl.BlockSpec(a.shape, lambda i: (0,) * a.ndim)
    return pl.pallas_call(
        _mlaprep_kernel,
        grid=(s // tm,),
        in_specs=[pl.BlockSpec((tm, Q_LORA), lambda i: (i, PROJ_C_Q // Q_LORA)),
                  pl.BlockSpec((tm, KV_LORA), lambda i: (i, PROJ_C_KV // KV_LORA)),
                  pl.BlockSpec((tm, LANES), lambda i: (i, PROJ_K_PE // LANES)),
                  pl.BlockSpec((1, tm), lambda i: (0, i)),
                  full(freq), full(gql), full(gkvl), full(wuq_t), full(wk), full(wv_t), full(gq_col), full(gk)],
        out_specs=[pl.BlockSpec((hq, MLA_QK_PAD, tm), lambda i: (0, 0, i)),
                   pl.BlockSpec((hq, tm, MLA_QK_PAD), lambda i: (0, i, 0)),
                   pl.BlockSpec((hq, tm // ATTN_TK, MLA_V, ATTN_TK), lambda i: (0, i, 0, 0))],
        out_shape=[jax.ShapeDtypeStruct((hq, MLA_QK_PAD, s), BF16),
                   jax.ShapeDtypeStruct((hq, s, MLA_QK_PAD), BF16),
                   jax.ShapeDtypeStruct((hq, s // ATTN_TK, MLA_V, ATTN_TK), BF16)],
        compiler_params=_cparams(("arbitrary",)),
        name="mlaprep",
    )(proj, proj, proj, pos_row, freq, gql, gkvl, wuq_t, wk, wv_t, gq_col, gk)


def _attn_kernel(qt_ref, k_ref, vt_ref, *refs, n_side, shifted, tk, ahead, n_sub):
    side_in, o_ref, side_out = refs[:n_side], refs[n_side], refs[n_side + 1:]
    for src, dst in zip(side_in, side_out):
        dst[...] = src[...].astype(BF16)

    n_chunks = k_ref.shape[1] // tk
    tq = qt_ref.shape[2] // n_sub

    def logits_t(item):
        t, j = item
        return jnp.dot(k_ref[0, j * tk:(j + 1) * tk, :], qt_ref[0, :, t * tq:(t + 1) * tq],
                       preferred_element_type=F32)

    def values_t(j, p_t):
        return jnp.dot(vt_ref[0, j], p_t.astype(BF16), preferred_element_type=F32)

    items = [(t, j) for t in range(n_sub) for j in range(n_chunks)]
    pending = [logits_t(it) for it in items[:ahead]]
    for i, (t, j) in enumerate(items):
        if i + ahead < len(items):
            pending.append(logits_t(items[i + ahead]))
        s2 = pending.pop(0)
        if j == 0:
            m, acc, l = jnp.full((1, tq), -jnp.inf, F32), None, None
        if shifted:
            m_new = jnp.maximum(m, jnp.max(s2, axis=0, keepdims=True))
            p_t, alpha = jnp.exp2(s2 - m_new), jnp.exp2(m - m_new)
            pv, ps = values_t(j, p_t), jnp.sum(p_t, axis=0, keepdims=True)
            acc, l = (pv, ps) if acc is None else (acc * alpha + pv, l * alpha + ps)
            m = m_new
        else:
            p_t = jnp.exp2(s2)
            pv, ps = values_t(j, p_t), jnp.sum(p_t, axis=0, keepdims=True)
            acc, l = (pv, ps) if acc is None else (acc + pv, l + ps)
        if j == n_chunks - 1:
            o_ref[t * tq:(t + 1) * tq, :] = (acc / l).T.astype(BF16)


def _attention(q_t, k, v_t, side_weights, *, shifted, tq=1024, n_sub=2, tk=ATTN_TK, ahead=2):
    hq, s, _ = k.shape
    n_q = s // tq
    n_steps = hq * n_q
    slab = lambda w: pl.BlockSpec((w.shape[0] // n_steps, w.shape[1]), lambda h, i: (h * n_q + i, 0))
    outs = pl.pallas_call(
        functools.partial(_attn_kernel, n_side=len(side_weights), shifted=shifted, tk=tk, ahead=ahead,
                          n_sub=n_sub),
        grid=(hq, n_q),
        in_specs=[pl.BlockSpec((1, MLA_QK_PAD, tq), lambda h, i: (h, 0, i)),
                  pl.BlockSpec((1, s, MLA_QK_PAD), lambda h, i: (h, 0, 0)),
                  pl.BlockSpec((1, s // tk, MLA_V, tk), lambda h, i: (h, 0, 0, 0))]
                 + [slab(w) for w in side_weights],
        out_specs=[pl.BlockSpec((tq, MLA_V), lambda h, i: (i, h))] + [slab(w) for w in side_weights],
        out_shape=[jax.ShapeDtypeStruct((s, hq * MLA_V), BF16)]
                  + [jax.ShapeDtypeStruct(w.shape, BF16) for w in side_weights],
        compiler_params=_cparams(("arbitrary", "arbitrary")),
        name="attn_shifted" if shifted else "attn",
    )(q_t, k, v_t, *side_weights)
    return outs[0], tuple(outs[1:])


def _conv_kernel(x_ref, w_ref, b_ref, o_ref, pad_ref, *, rows, out_scale, transpose_out):
    s = x_ref.shape[0]
    zeros = jnp.zeros((CONV_HALO, LANES), F32)
    pad_ref[0:CONV_HALO, :] = zeros
    pad_ref[CONV_HALO + s:CONV_HALO + s + CONV_HALO, :] = zeros
    pad_ref[CONV_HALO:CONV_HALO + s, :] = x_ref[...].astype(F32)
    w = w_ref[...]
    b = b_ref[...]
    for r in range(s // rows):
        base = r * rows
        acc = b
        for j in range(CONV_WIDTH):
            off = CONV_HALO + j - CONV_WIDTH // 2
            acc = acc + w[j:j + 1, :] * pad_ref[base + off:base + off + rows, :]
        y = acc * jax.nn.sigmoid(acc) * out_scale
        if transpose_out:
            o_ref[:, base:base + rows] = y.T.astype(BF16)
        else:
            o_ref[base:base + rows, :] = y.astype(BF16)


def _conv(proj, conv_w, conv_b, *, first_tile, n_tiles, out_scale, transpose_out, rows=256):
    s = proj.shape[0]
    n_ch = n_tiles * LANES
    return pl.pallas_call(
        functools.partial(_conv_kernel, rows=rows, out_scale=out_scale, transpose_out=transpose_out),
        grid=(n_tiles,),
        in_specs=[pl.BlockSpec((s, LANES), lambda j: (0, first_tile + j)),
                  pl.BlockSpec((CONV_WIDTH, LANES), lambda j: (0, first_tile + j)),
                  pl.BlockSpec((1, LANES), lambda j: (0, first_tile + j))],
        out_specs=pl.BlockSpec((LANES, s), lambda j: (j, 0)) if transpose_out
                  else pl.BlockSpec((s, LANES), lambda j: (0, j)),
        out_shape=jax.ShapeDtypeStruct((n_ch, s) if transpose_out else (s, n_ch), BF16),
        scratch_shapes=[pltpu.VMEM((s + 2 * CONV_HALO, LANES), F32)],
        compiler_params=_cparams(("arbitrary",)),
        name="conv_k" if transpose_out else "conv_q",
    )(proj, conv_w, conv_b)


def _scan_rows(x, row_id, reverse, combine, identity):
    n, s = x.shape[0], 1
    while s < n:
        shifted = pltpu.roll(x, n - s if reverse else s, 0)
        ok = (row_id < n - s) if reverse else (row_id >= s)
        x = combine(x, jnp.where(ok, shifted, identity))
        s *= 2
    return x


def _mlstm_kernel(qf_ref, kf_ref, vf_ref, gf_ref, qb_ref, kb_ref, vb_ref, gb_ref, bg_ref,
                  hf_ref, hb_ref, c_ref, m_ref, *, n_sub):
    @pl.when(pl.program_id(0) == 0)
    def _():
        c_ref[...] = jnp.zeros(c_ref.shape, F32)
        m_ref[...] = jnp.full(m_ref.shape, M_INIT, F32)

    L = MLSTM_CHUNK
    row = lax.broadcasted_iota(jnp.int32, (L, L), 0)
    col = lax.broadcasted_iota(jnp.int32, (L, L), 1)
    row_id = lax.broadcasted_iota(jnp.int32, (L, LANES), 0)
    ones_col = (lax.broadcasted_iota(jnp.int32, (L, LANES), 1) == 0).astype(BF16)
    hm = N_MLSTM_HEADS
    dirs = ((qf_ref, kf_ref, vf_ref, gf_ref, hf_ref), (qb_ref, kb_ref, vb_ref, gb_ref, hb_ref))
    for sub, d in [(sub, d) for sub in range(n_sub) for d in range(2)]:
        q_ref, k_ref, v_ref, g_ref, o_ref = dirs[d]
        backward = d == 1
        first_row = ((n_sub - 1 - sub) if backward else sub) * L
        rows = slice(first_row, first_row + L)
        mask = (row <= col) if backward else (row >= col)
        g = g_ref[rows, :] + bg_ref[...]
        lf = jax.nn.log_sigmoid(g)
        to_lane0 = lambda t, first: t if first == 0 else pltpu.roll(t, LANES - first, 1)
        log_i = to_lane0(g, 2 * d * hm)
        log_f = to_lane0(lf, (2 * d + 1) * hm)
        b = _scan_rows(log_f, row_id, backward, jnp.add, 0.0)
        b_tot = jnp.sum(log_f, axis=0, keepdims=True)
        a = log_i - b
        m_prev = m_ref[d:d + 1, :]
        mx = jnp.maximum(m_prev, _scan_rows(a, row_id, backward, jnp.maximum, -jnp.inf))
        u2 = mx * (-LOG2_E)
        w_inter = jnp.exp(m_prev - mx)
        floor = jnp.exp(-(b + mx))
        mn = jnp.maximum(m_prev, jnp.max(a, axis=0, keepdims=True))
        decay = jnp.exp(m_prev - mn)
        m_ref[d:d + 1, :] = b_tot + mn
        a_t = a.T
        for h in range(hm):
            sl = slice(h * MLSTM_DH, (h + 1) * MLSTM_DH)
            chain = d * hm + h
            q, k_t = q_ref[rows, sl], k_ref[sl, rows]
            v_aug = jnp.concatenate([v_ref[rows, sl], ones_col], axis=-1)
            a_row = a_t[h:h + 1, :]
            d_m = jnp.where(mask, jnp.exp2(u2[:, h:h + 1] + a_row * LOG2_E), 0.0)
            c_aug = c_ref[chain]
            scores = jnp.dot(q, k_t, preferred_element_type=F32) * d_m
            r = (jnp.dot(scores.astype(BF16), v_aug, preferred_element_type=F32)
                 + w_inter[:, h:h + 1] * jnp.dot(q, c_aug.astype(BF16), preferred_element_type=F32))
            num, den = r[:, :MLSTM_DH], r[:, MLSTM_DH:MLSTM_DH + 1]
            o_ref[rows, sl] = (num / jnp.maximum(jnp.abs(den), floor[:, h:h + 1])).astype(o_ref.dtype)

            kw_t = (k_t.astype(F32) * jnp.exp(a_row - mn[:, h:h + 1])).astype(BF16)
            c_ref[chain] = decay[:, h:h + 1] * c_aug + jnp.dot(kw_t, v_aug, preferred_element_type=F32)


def _mlstm(q, k_t, proj, gates, bg, n_sub=2):
    s = q.shape[0]
    rows = MLSTM_CHUNK * n_sub
    nb = s // rows
    w = N_MLSTM_HEADS * MLSTM_DH
    fwd = lambda blk: (lambda i: (i, blk))
    bwd = lambda blk: (lambda i: (nb - 1 - i, blk))
    specs = lambda ix: [pl.BlockSpec((rows, w), ix(0)),
                        pl.BlockSpec((w, rows), lambda i: ix(0)(i)[::-1]),
                        pl.BlockSpec((rows, w), ix(PROJ_V_M // w)),
                        pl.BlockSpec((rows, LANES), ix(0))]
    return pl.pallas_call(
        functools.partial(_mlstm_kernel, n_sub=n_sub),
        grid=(nb,),
        in_specs=specs(fwd) + specs(bwd) + [pl.BlockSpec((1, LANES), lambda i: (0, 0))],
        out_specs=[pl.BlockSpec((rows, w), fwd(0)), pl.BlockSpec((rows, w), bwd(0))],
        out_shape=[jax.ShapeDtypeStruct((s, w), BF16), jax.ShapeDtypeStruct((s, w), BF16)],
        scratch_shapes=[pltpu.VMEM((2 * N_MLSTM_HEADS, MLSTM_DH, MLSTM_AUG), F32),
                        pltpu.VMEM((2 * N_MLSTM_HEADS, LANES), F32)],
        compiler_params=_cparams(("arbitrary",)),
        name="mlstm",
    )(q, k_t, proj, gates, q, k_t, proj, gates, bg)


def _outproj_kernel(attn_ref, hf_ref, hb_ref, om_ref, x_ref, gm_ref, w_ref, gate_ref, g2_ref, sc_ref, sh_ref,
                    x1_ref, h2_ref):
    hsum = hf_ref[...].astype(F32) + hb_ref[...].astype(F32)
    gm = gm_ref[...]
    parts = []
    for h in range(N_MLSTM_HEADS):
        sl = slice(h * MLSTM_DH, (h + 1) * MLSTM_DH)
        seg = hsum[:, sl]
        parts.append(seg * _rms_scale(seg, MLSTM_DH) * gm[:, sl])
    ml = (jax.nn.sigmoid(om_ref[...].astype(F32)) * jnp.concatenate(parts, axis=-1)).astype(BF16)
    n_attn = attn_ref.shape[1]
    mixed = (jnp.dot(attn_ref[...], w_ref[:n_attn, :], preferred_element_type=F32)
             + jnp.dot(ml, w_ref[n_attn:, :], preferred_element_type=F32))
    x1 = x_ref[...] + gate_ref[...] * mixed
    x1_ref[...] = x1
    h2_ref[...] = _modulated_norm(x1, g2_ref[...], sc_ref[...], sh_ref[...]).astype(BF16)


def _outproj(attn, hf, hb, proj, x, gm, w_out, mod, g2, tm=512):
    s, d = x.shape
    wm = hf.shape[1]
    row = lambda blk: pl.BlockSpec((1, d), lambda i: (0, blk))
    return pl.pallas_call(
        _outproj_kernel,
        grid=(s // tm,),
        in_specs=[pl.BlockSpec((tm, attn.shape[1]), lambda i: (i, 0)),
                  pl.BlockSpec((tm, wm), lambda i: (i, 0)),
                  pl.BlockSpec((tm, wm), lambda i: (i, 0)),
                  pl.BlockSpec((tm, wm), lambda i: (i, PROJ_O_M // wm)),
                  pl.BlockSpec((tm, d), lambda i: (i, 0)),
                  pl.BlockSpec((1, wm), lambda i: (0, 0)),
                  pl.BlockSpec(w_out.shape, lambda i: (0, 0), pipeline_mode=pl.Buffered(1)),
                  row(MOD_GATE1), row(0), row(MOD_SCALE2), row(MOD_SHIFT2)],
        out_specs=[pl.BlockSpec((tm, d), lambda i: (i, 0)), pl.BlockSpec((tm, d), lambda i: (i, 0))],
        out_shape=[jax.ShapeDtypeStruct((s, d), F32), jax.ShapeDtypeStruct((s, d), BF16)],
        compiler_params=_cparams(("arbitrary",)),
        name="outproj",
    )(attn, hf, hb, proj, x, gm, w_out, mod, g2, mod, mod)


def _ffn_kernel(h2_ref, w1_ref, w2_ref, x1_hbm, gate_ref, o_ref, x1_buf, x1_sem, *, f_last):
    m, f = pl.program_id(0), pl.program_id(1)
    tm = o_ref.shape[0]
    x1_copy = pltpu.make_async_copy(x1_hbm.at[pl.ds(pl.multiple_of(m * tm, tm), tm), :], x1_buf, x1_sem)

    def hidden_tile():
        a = jnp.maximum(jnp.dot(h2_ref[...], w1_ref[...], preferred_element_type=F32), 0.0)
        return jnp.dot((a * a).astype(BF16), w2_ref[...], preferred_element_type=F32)

    @pl.when(f == 0)
    def _():
        x1_copy.start()
        o_ref[...] = hidden_tile()

    @pl.when(f > 0)
    def _():
        o_ref[...] += hidden_tile()

    @pl.when(f == f_last)
    def _():
        x1_copy.wait()
        o_ref[...] = x1_buf[...] + gate_ref[...] * o_ref[...]


def _ffn(h2, w1, w2, x1, mod, tm=1024, tf=1024):
    s, d = h2.shape
    dff = w1.shape[1]
    return pl.pallas_call(
        functools.partial(_ffn_kernel, f_last=dff // tf - 1),
        grid=(s // tm, dff // tf),
        in_specs=[pl.BlockSpec((tm, d), lambda m, f: (m, 0)),
                  pl.BlockSpec((d, tf), lambda m, f: (0, f)),
                  pl.BlockSpec((tf, d), lambda m, f: (f, 0)),
                  pl.BlockSpec(memory_space=pl.ANY),
                  pl.BlockSpec((1, d), lambda m, f: (0, MOD_GATE2))],
        out_specs=pl.BlockSpec((tm, d), lambda m, f: (m, 0)),
        out_shape=jax.ShapeDtypeStruct((s, d), F32),
        scratch_shapes=[pltpu.VMEM((tm, d), F32), pltpu.SemaphoreType.DMA(())],
        compiler_params=_cparams(("arbitrary", "arbitrary")),
        name="ffn",
    )(h2, w1, w2, x1, mod)


PACK_TILE = 512
PACK_SRC = 64
PACK_NSRC = PACK_TILE // PACK_SRC
W_IN_C_KV_START, W_IN_K_PE_START, W_IN_MIX_START, W_IN_GATES_START = 512, 768, 832, 4928
N_GATE_COLS = 16


def _pack_src_blocks():
    rows = []
    n_mix = 4 * N_MLSTM_HEADS * MLSTM_DH // PACK_TILE
    for n in range(n_mix):
        first = (W_IN_MIX_START + n * PACK_TILE) // PACK_SRC
        rows.append([first + i for i in range(PACK_NSRC)])
    rows.append(list(range(PACK_NSRC)))
    gates_blk = W_IN_GATES_START // PACK_SRC
    c_kv = list(range(W_IN_C_KV_START // PACK_SRC, W_IN_K_PE_START // PACK_SRC))
    k_pe = W_IN_K_PE_START // PACK_SRC
    rows.append(c_kv + [k_pe, k_pe, gates_blk, gates_blk])
    return jnp.asarray(rows, jnp.int32).reshape(-1)


def _pack_kernel(tbl_ref, *refs, n_mix):
    del tbl_ref
    srcs, o_ref = refs[:-1], refs[-1]
    n = pl.program_id(0)
    blk = lambda i: slice(i * PACK_SRC, (i + 1) * PACK_SRC)

    @pl.when(n <= n_mix)
    def _():
        for i, b in enumerate(srcs):
            o_ref[blk(i), :] = b[...].astype(BF16)

    @pl.when(n == n_mix + 1)
    def _():
        zeros = jnp.zeros(srcs[0].shape, BF16)
        for i in range(5):
            o_ref[blk(i), :] = srcs[i][...].astype(BF16)
        o_ref[blk(5), :] = zeros
        rows = lax.broadcasted_iota(jnp.int32, srcs[6].shape, 0)
        o_ref[blk(6), :] = jnp.where(rows < N_GATE_COLS, srcs[6][...], 0.0).astype(BF16)
        o_ref[blk(7), :] = zeros


def _pack_w_in(w_in_t):
    d = w_in_t.shape[1]
    tbl = _pack_src_blocks()
    n_tiles = tbl.shape[0] // PACK_NSRC
    src = lambda i: pl.BlockSpec((PACK_SRC, d), lambda n, t: (t[n * PACK_NSRC + i], 0))
    return pl.pallas_call(
        functools.partial(_pack_kernel, n_mix=n_tiles - 2),
        grid_spec=pltpu.PrefetchScalarGridSpec(
            num_scalar_prefetch=1, grid=(n_tiles,),
            in_specs=[src(i) for i in range(PACK_NSRC)],
            out_specs=pl.BlockSpec((PACK_TILE, d), lambda n, t: (n, 0))),
        out_shape=jax.ShapeDtypeStruct((n_tiles * PACK_TILE, d), BF16),
        compiler_params=_cparams(("arbitrary",)),
        name="packw",
    )(tbl, *([w_in_t] * PACK_NSRC))


def _pad_lanes(v, n):
    return jnp.pad(v, ((0, 0), (0, n - v.shape[1])))


def kernel(x, c, positions, w_ada, b_ada, norm_mix_g, w_in, b_gates, conv_w, conv_b, q_lora_g, w_uq,
           kv_lora_g, w_ukv, q_norm_g, k_norm_g, mlstm_norm_g, w_out, norm_mlp_g, w_ff1, w_ff2):
    bsz, s, d = x.shape
    assert bsz == 1, "kernels are written for a single sequence"
    xs = x[0]
    pos_row = positions.reshape(1, s)
    half = jnp.arange(ROPE_HALF, dtype=F32)
    freq = (ROPE_THETA ** (-half / ROPE_HALF)).reshape(ROPE_HALF, 1)
    row = lambda v: v.reshape(1, -1).astype(F32)

    for l in range(w_ada.shape[0]):
        mod = _mod(c.reshape(d, 1), w_ada[l], row(b_ada[l]))
        proj, gates = _inproj(xs, row(norm_mix_g[l]), mod, _pack_w_in(w_in[l].T))

        wuq_t = jnp.pad(w_uq[l].reshape(-1, N_MLA_HEADS, MLA_QK), ((0, 0), (0, 0), (0, MLA_QK_PAD - MLA_QK)))
        wuq_t = wuq_t.reshape(-1, N_MLA_HEADS * MLA_QK_PAD).T.astype(BF16)
        wkv = w_ukv[l].reshape(-1, N_MLA_HEADS, MLA_NOPE + MLA_V)
        wk = wkv[:, :, :MLA_NOPE].reshape(-1, N_MLA_HEADS * MLA_NOPE).astype(BF16)
        wv_t = wkv[:, :, MLA_NOPE:].reshape(-1, N_MLA_HEADS * MLA_V).T.astype(BF16)
        g_q, g_k = row(q_norm_g[l]), row(k_norm_g[l])
        q_t, k, v_t = _mlaprep(proj, pos_row, freq, row(q_lora_g[l]), row(kv_lora_g[l]), wuq_t, wk, wv_t,
                               g_q.reshape(-1, 1), _pad_lanes(g_k, MLA_QK_PAD))
        logit_bound = 1.02 * LOG2_E * MLA_QK ** 0.5 * jnp.max(jnp.abs(g_q)) * jnp.max(jnp.abs(g_k))
        attn, (w_out_b, w_ff1_b, w_ff2_b) = lax.cond(
            logit_bound <= SAFE_LOG2,
            functools.partial(_attention, shifted=False), functools.partial(_attention, shifted=True),
            q_t, k, v_t, (w_out[l], w_ff1[l], w_ff2[l]))

        n_qk_tiles = N_MLSTM_HEADS * MLSTM_DH // LANES
        assert (PROJ_Q_M, PROJ_K_M) == (0, n_qk_tiles * LANES)
        q_m = _conv(proj, conv_w[l], row(conv_b[l]), first_tile=0, n_tiles=n_qk_tiles, out_scale=1.0,
                    transpose_out=False)
        k_m_t = _conv(proj, conv_w[l], row(conv_b[l]), first_tile=n_qk_tiles, n_tiles=n_qk_tiles,
                      out_scale=MLSTM_DH ** -0.5, transpose_out=True)
        hf, hb = _mlstm(q_m, k_m_t, proj, gates, _pad_lanes(row(b_gates[l]), LANES))

        x1, h2 = _outproj(attn, hf, hb, proj, xs, row(mlstm_norm_g[l]), w_out_b, mod, row(norm_mlp_g[l]))
        xs = _ffn(h2, w_ff1_b, w_ff2_b, x1, mod)
    return xs[None]
```

```python
import functools

import jax
import jax.numpy as jnp
from jax import lax
from jax.experimental import pallas as pl
from jax.experimental.pallas import tpu as pltpu

F32 = jnp.float32
BF16 = jnp.bfloat16

LANES = 128
N_MLA_HEADS = 8
MLA_NOPE = 128
ROPE_DIM = 64
ROPE_HALF = ROPE_DIM // 2
MLA_QK = MLA_NOPE + ROPE_DIM
MLA_QK_PAD = 256
MLA_V = 128
ATTN_TK = 256
Q_LORA, KV_LORA = 512, 256
PROJ_Q_M, PROJ_K_M, PROJ_V_M, PROJ_O_M = 0, 1024, 2048, 3072
PROJ_C_Q, PROJ_C_KV, PROJ_K_PE, PROJ_GATES = 4096, 4608, 4864, 4992
MOD_SHIFT1, MOD_SCALE1, MOD_GATE1, MOD_SHIFT2, MOD_SCALE2, MOD_GATE2 = range(6)
ROPE_THETA = 10000.0
N_MLSTM_HEADS = 4
MLSTM_DH = 256
MLSTM_CHUNK = 256
MLSTM_AUG = MLSTM_DH + LANES
CONV_WIDTH = 5
CONV_HALO = 8
EPS = 1e-6
M_INIT = -1e30
LOG2_E = 1.4426950408889634
SAFE_LOG2 = 60.0
VMEM_LIMIT = 56 * 1024 * 1024


def _cparams(sem):
    return pltpu.CompilerParams(dimension_semantics=sem, vmem_limit_bytes=VMEM_LIMIT)


def _mod_kernel(c_ref, w_ref, b_ref, o_ref, sb_ref, *, tn):
    @pl.when(pl.program_id(0) == 0)
    def _():
        cc = c_ref[...]
        sb_ref[...] = jnp.broadcast_to(cc * jax.nn.sigmoid(cc), sb_ref.shape)

    sb = sb_ref[...]
    for j in range(tn // LANES):
        sl = slice(j * LANES, (j + 1) * LANES)
        o_ref[:, sl] = jnp.sum(w_ref[:, sl] * sb, axis=0, keepdims=True) + b_ref[:, sl]


def _mod(c_col, w_ada, b_ada, tn=1024):
    d, n = w_ada.shape
    return pl.pallas_call(
        functools.partial(_mod_kernel, tn=tn),
        grid=(n // tn,),
        in_specs=[pl.BlockSpec((d, 1), lambda j: (0, 0)),
                  pl.BlockSpec((d, tn), lambda j: (0, j)),
                  pl.BlockSpec((1, tn), lambda j: (0, j))],
        out_specs=pl.BlockSpec((1, tn), lambda j: (0, j)),
        out_shape=jax.ShapeDtypeStruct((1, n), F32),
        scratch_shapes=[pltpu.VMEM((d, LANES), F32)],
        compiler_params=_cparams(("arbitrary",)),
        name="mod",
    )(c_col, w_ada, b_ada)


def _modulated_norm(x, g, scale, shift):
    ms = jnp.mean(x * x, axis=-1, keepdims=True)
    return (x * lax.rsqrt(ms + EPS) * g) * (1.0 + scale) + shift


def _inproj_kernel(x_ref, g_ref, sc_ref, sh_ref, w_ref, o_ref, gate_ref, h_ref, *, n_last):
    n = pl.program_id(1)
    project = lambda h: lax.dot_general(h, w_ref[...], (((1,), (1,)), ((), ())), preferred_element_type=F32)

    @pl.when(n == 0)
    def _():
        h = _modulated_norm(x_ref[...], g_ref[...], sc_ref[...], sh_ref[...]).astype(BF16)
        h_ref[...] = h
        o_ref[...] = project(h).astype(BF16)

    @pl.when(n > 0)
    def _():
        acc = project(h_ref[...])
        o_ref[...] = acc.astype(BF16)

        @pl.when(n == n_last)
        def _():
            gate_ref[...] = acc[:, -LANES:]


def _inproj(x, g, mod, w_t, tm=1024, tn=1024):
    s, d = x.shape
    n_tot = w_t.shape[0]
    return pl.pallas_call(
        functools.partial(_inproj_kernel, n_last=n_tot // tn - 1),
        grid=(s // tm, n_tot // tn),
        in_specs=[pl.BlockSpec((tm, d), lambda m, n: (m, 0)),
                  pl.BlockSpec((1, d), lambda m, n: (0, 0)),
                  pl.BlockSpec((1, d), lambda m, n: (0, MOD_SCALE1)),
                  pl.BlockSpec((1, d), lambda m, n: (0, MOD_SHIFT1)),
                  pl.BlockSpec((tn, d), lambda m, n: (n, 0))],
        out_specs=[pl.BlockSpec((tm, tn), lambda m, n: (m, n)),
                   pl.BlockSpec((tm, LANES), lambda m, n: (m, 0))],
        out_shape=[jax.ShapeDtypeStruct((s, n_tot), BF16),
                   jax.ShapeDtypeStruct((s, LANES), F32)],
        scratch_shapes=[pltpu.VMEM((tm, d), BF16)],
        compiler_params=_cparams(("arbitrary", "arbitrary")),
        name="inproj",
    )(x, g, mod, mod, w_t)


def _rms_scale(x, n):
    return lax.rsqrt(jnp.sum(x * x, axis=-1, keepdims=True) * (1.0 / n) + EPS)


def _mlaprep_kernel(cq_ref, ckv_ref, kpe_ref, pos_ref, freq_ref, gql_ref, gkvl_ref, wuqt_ref, wk_ref, wvt_ref,
                    gqc_ref, gk_ref, q_ref, k_ref, v_ref):
    tm = cq_ref.shape[0]
    nt = (((1,), (1,)), ((), ()))
    cq = cq_ref[...].astype(F32)
    cqn = (cq * _rms_scale(cq, cq.shape[-1]) * gql_ref[...]).astype(BF16)
    ckv = ckv_ref[...].astype(F32)
    ckvn = (ckv * _rms_scale(ckv, ckv.shape[-1]) * gkvl_ref[...]).astype(BF16)
    qf_t = lax.dot_general(wuqt_ref[...], cqn, nt, preferred_element_type=F32)
    kf = jnp.dot(ckvn, wk_ref[...], preferred_element_type=F32)
    vf_t = lax.dot_general(wvt_ref[...], ckvn, nt, preferred_element_type=F32)

    ang_t = freq_ref[...] * pos_ref[...].astype(F32)
    cos_t, sin_t = jnp.cos(ang_t), jnp.sin(ang_t)

    gq_b = jnp.broadcast_to(gqc_ref[...] * (LOG2_E * MLA_QK ** -0.5), (MLA_QK, tm))
    q_pad = jnp.zeros((MLA_QK_PAD - MLA_QK, tm), F32)
    for h in range(N_MLA_HEADS):
        qh = qf_t[h * MLA_QK_PAD:h * MLA_QK_PAD + MLA_QK]
        r = lax.rsqrt(jnp.sum(qh * qh, axis=0, keepdims=True) * (1.0 / MLA_QK) + EPS)
        qn = qh * r * gq_b
        x1, x2 = qn[MLA_NOPE:MLA_NOPE + ROPE_HALF], qn[MLA_NOPE + ROPE_HALF:]
        q_ref[h] = jnp.concatenate([qn[:MLA_NOPE], x1 * cos_t - x2 * sin_t, x1 * sin_t + x2 * cos_t, q_pad],
                                   axis=0).astype(BF16)

    for h in range(N_MLA_HEADS):
        for c in range(tm // ATTN_TK):
            v_ref[h, c] = vf_t[h * MLA_V:(h + 1) * MLA_V, c * ATTN_TK:(c + 1) * ATTN_TK].astype(BF16)

    z_half, z_pad = jnp.zeros_like(cos_t), jnp.zeros((LANES - ROPE_DIM, tm), F32)
    c_tab = jnp.concatenate([cos_t, cos_t, z_pad], axis=0).T
    s_up = jnp.concatenate([z_half, sin_t, z_pad], axis=0).T
    s_dn = jnp.concatenate([-sin_t, z_half, z_pad], axis=0).T
    gk = gk_ref[...]
    kpe = kpe_ref[...].astype(F32)
    kpe_g = kpe * gk[:, MLA_NOPE:]
    kpe_rot = (kpe_g * c_tab + pltpu.roll(kpe_g, ROPE_HALF, 1) * s_up
               + pltpu.roll(kpe_g, LANES - ROPE_HALF, 1) * s_dn)
    kpe_ssq = jnp.sum(kpe * kpe, axis=-1, keepdims=True)
    for h in range(N_MLA_HEADS):
        kn = kf[:, h * MLA_NOPE:(h + 1) * MLA_NOPE]
        r = lax.rsqrt((jnp.sum(kn * kn, axis=-1, keepdims=True) + kpe_ssq) * (1.0 / MLA_QK) + EPS)
        k_ref[h] = jnp.concatenate([kn * r * gk[:, :MLA_NOPE], kpe_rot * r], axis=-1).astype(BF16)


def _mlaprep(proj, pos_row, freq, gql, gkvl, wuq_t, wk, wv_t, gq_col, gk, tm=1024):
    s = proj.shape[0]
    hq = N_MLA_HEADS
    full = lambda a: pl.BlockSpec(a.shape, lambda i: (0,) * a.ndim)
    return pl.pallas_call(
        _mlaprep_kernel,
        grid=(s // tm,),
        in_specs=[pl.BlockSpec((tm, Q_LORA), lambda i: (i, PROJ_C_Q // Q_LORA)),
                  pl.BlockSpec((tm, KV_LORA), lambda i: (i, PROJ_C_KV // KV_LORA)),
                  pl.BlockSpec((tm, LANES), lambda i: (i, PROJ_K_PE // LANES)),
                  pl.BlockSpec((1, tm), lambda i: (0, i)),
                  full(freq), full(gql), full(gkvl), full(wuq_t), full(wk), full(wv_t), full(gq_col), full(gk)],
        out_specs=[pl.BlockSpec((hq, MLA_QK_PAD, tm), lambda i: (0, 0, i)),
                   pl.BlockSpec((hq, tm, MLA_QK_PAD), lambda i: (0, i, 0)),
                   pl.BlockSpec((hq, tm // ATTN_TK, MLA_V, ATTN_TK), lambda i: (0, i, 0, 0))],
        out_shape=[jax.ShapeDtypeStruct((hq, MLA_QK_PAD, s), BF16),
                   jax.ShapeDtypeStruct((hq, s, MLA_QK_PAD), BF16),
                   jax.ShapeDtypeStruct((hq, s // ATTN_TK, MLA_V, ATTN_TK), BF16)],
        compiler_params=_cparams(("arbitrary",)),
        name="mlaprep",
    )(proj, proj, proj, pos_row, freq, gql, gkvl, wuq_t, wk, wv_t, gq_col, gk)


def _attn_kernel(qt_ref, k_ref, vt_ref, *refs, n_side, shifted, tk, ahead, n_sub):
    side_in, o_ref, side_out = refs[:n_side], refs[n_side], refs[n_side + 1:]
    for src, dst in zip(side_in, side_out):
        dst[...] = src[...].astype(BF16)

    n_chunks = k_ref.shape[1] // tk
    tq = qt_ref.shape[2] // n_sub

    def logits_t(item):
        t, j = item
        return jnp.dot(k_ref[0, j * tk:(j + 1) * tk, :], qt_ref[0, :, t * tq:(t + 1) * tq],
                       preferred_element_type=F32)

    def values_t(j, p_t):
        return jnp.dot(vt_ref[0, j], p_t.astype(BF16), preferred_element_type=F32)

    items = [(t, j) for t in range(n_sub) for j in range(n_chunks)]
    pending = [logits_t(it) for it in items[:ahead]]
    for i, (t, j) in enumerate(items):
        if i + ahead < len(items):
            pending.append(logits_t(items[i + ahead]))
        s2 = pending.pop(0)
        if j == 0:
            m, acc, l = jnp.full((1, tq), -jnp.inf, F32), None, None
        if shifted:
            m_new = jnp.maximum(m, jnp.max(s2, axis=0, keepdims=True))
            p_t, alpha = jnp.exp2(s2 - m_new), jnp.exp2(m - m_new)
            pv, ps = values_t(j, p_t), jnp.sum(p_t, axis=0, keepdims=True)
            acc, l = (pv, ps) if acc is None else (acc * alpha + pv, l * alpha + ps)
            m = m_new
        else:
            p_t = jnp.exp2(s2)
            pv, ps = values_t(j, p_t), jnp.sum(p_t, axis=0, keepdims=True)
            acc, l = (pv, ps) if acc is None else (acc + pv, l + ps)
        if j == n_chunks - 1:
            o_ref[t * tq:(t + 1) * tq, :] = (acc / l).T.astype(BF16)


def _attention(q_t, k, v_t, side_weights, *, shifted, tq=1024, n_sub=2, tk=ATTN_TK, ahead=2):
    hq, s, _ = k.shape
    n_q = s // tq
    n_steps = hq * n_q
    slab = lambda w: pl.BlockSpec((w.shape[0] // n_steps, w.shape[1]), lambda h, i: (h * n_q + i, 0))
    outs = pl.pallas_call(
        functools.partial(_attn_kernel, n_side=len(side_weights), shifted=shifted, tk=tk, ahead=ahead,
                          n_sub=n_sub),
        grid=(hq, n_q),
        in_specs=[pl.BlockSpec((1, MLA_QK_PAD, tq), lambda h, i: (h, 0, i)),
                  pl.BlockSpec((1, s, MLA_QK_PAD), lambda h, i: (h, 0, 0)),
                  pl.BlockSpec((1, s // tk, MLA_V, tk), lambda h, i: (h, 0, 0, 0))]
                 + [slab(w) for w in side_weights],
        out_specs=[pl.BlockSpec((tq, MLA_V), lambda h, i: (i, h))] + [slab(w) for w in side_weights],
        out_shape=[jax.ShapeDtypeStruct((s, hq * MLA_V), BF16)]
                  + [jax.ShapeDtypeStruct(w.shape, BF16) for w in side_weights],
        compiler_params=_cparams(("arbitrary", "arbitrary")),
        name="attn_shifted" if shifted else "attn",
    )(q_t, k, v_t, *side_weights)
    return outs[0], tuple(outs[1:])


def _conv_kernel(x_ref, w_ref, b_ref, o_ref, pad_ref, *, rows, out_scale, transpose_out):
    s = x_ref.shape[0]
    zeros = jnp.zeros((CONV_HALO, LANES), F32)
    pad_ref[0:CONV_HALO, :] = zeros
    pad_ref[CONV_HALO + s:CONV_HALO + s + CONV_HALO, :] = zeros
    pad_ref[CONV_HALO:CONV_HALO + s, :] = x_ref[...].astype(F32)
    w = w_ref[...]
    b = b_ref[...]
    for r in range(s // rows):
        base = r * rows
        acc = b
        for j in range(CONV_WIDTH):
            off = CONV_HALO + j - CONV_WIDTH // 2
            acc = acc + w[j:j + 1, :] * pad_ref[base + off:base + off + rows, :]
        half = acc * (0.5 * out_scale)
        y = half + half * jnp.tanh(acc * 0.5)
        if transpose_out:
            o_ref[:, base:base + rows] = y.T.astype(BF16)
        else:
            o_ref[base:base + rows, :] = y.astype(BF16)


def _conv(proj, conv_w, conv_b, *, first_tile, n_tiles, out_scale, transpose_out, rows=256):
    s = proj.shape[0]
    n_ch = n_tiles * LANES
    return pl.pallas_call(
        functools.partial(_conv_kernel, rows=rows, out_scale=out_scale, transpose_out=transpose_out),
        grid=(n_tiles,),
        in_specs=[pl.BlockSpec((s, LANES), lambda j: (0, first_tile + j)),
                  pl.BlockSpec((CONV_WIDTH, LANES), lambda j: (0, first_tile + j)),
                  pl.BlockSpec((1, LANES), lambda j: (0, first_tile + j))],
        out_specs=pl.BlockSpec((LANES, s), lambda j: (j, 0)) if transpose_out
                  else pl.BlockSpec((s, LANES), lambda j: (0, j)),
        out_shape=jax.ShapeDtypeStruct((n_ch, s) if transpose_out else (s, n_ch), BF16),
        scratch_shapes=[pltpu.VMEM((s + 2 * CONV_HALO, LANES), F32)],
        compiler_params=_cparams(("arbitrary",)),
        name="conv_k" if transpose_out else "conv_q",
    )(proj, conv_w, conv_b)


def _scan_rows(x, row_id, reverse, combine, identity):
    n, s = x.shape[0], 1
    while s < n:
        shifted = pltpu.roll(x, n - s if reverse else s, 0)
        ok = (row_id < n - s) if reverse else (row_id >= s)
        x = combine(x, jnp.where(ok, shifted, identity))
        s *= 2
    return x


def _mlstm_kernel(qf_ref, kf_ref, vf_ref, gf_ref, qb_ref, kb_ref, vb_ref, gb_ref, bg_ref,
                  hf_ref, hb_ref, c_ref, m_ref, *, n_sub):
    @pl.when(pl.program_id(0) == 0)
    def _():
        c_ref[...] = jnp.zeros(c_ref.shape, F32)
        m_ref[...] = jnp.full(m_ref.shape, M_INIT, F32)

    L = MLSTM_CHUNK
    row = lax.broadcasted_iota(jnp.int32, (L, L), 0)
    col = lax.broadcasted_iota(jnp.int32, (L, L), 1)
    row_id = lax.broadcasted_iota(jnp.int32, (L, LANES), 0)
    ones_col = (lax.broadcasted_iota(jnp.int32, (L, LANES), 1) == 0).astype(BF16)
    hm = N_MLSTM_HEADS
    dirs = ((qf_ref, kf_ref, vf_ref, gf_ref, hf_ref), (qb_ref, kb_ref, vb_ref, gb_ref, hb_ref))
    for sub, d in [(sub, d) for sub in range(n_sub) for d in range(2)]:
        q_ref, k_ref, v_ref, g_ref, o_ref = dirs[d]
        backward = d == 1
        first_row = ((n_sub - 1 - sub) if backward else sub) * L
        rows = slice(first_row, first_row + L)
        mask = (row <= col) if backward else (row >= col)
        g = g_ref[rows, :] + bg_ref[...]
        lf = jax.nn.log_sigmoid(g)
        to_lane0 = lambda t, first: t if first == 0 else pltpu.roll(t, LANES - first, 1)
        log_i = to_lane0(g, 2 * d * hm)
        log_f = to_lane0(lf, (2 * d + 1) * hm)
        b = _scan_rows(log_f, row_id, backward, jnp.add, 0.0)
        b_tot = jnp.sum(log_f, axis=0, keepdims=True)
        a = log_i - b
        m_prev = m_ref[d:d + 1, :]
        mx = jnp.maximum(m_prev, _scan_rows(a, row_id, backward, jnp.maximum, -jnp.inf))
        u2 = mx * (-LOG2_E)
        w_inter = jnp.exp(m_prev - mx)
        floor = jnp.exp(-(b + mx))
        mn = jnp.maximum(m_prev, jnp.max(a, axis=0, keepdims=True))
        decay = jnp.exp(m_prev - mn)
        m_ref[d:d + 1, :] = b_tot + mn
        a_t = a.T
        for h in range(hm):
            sl = slice(h * MLSTM_DH, (h + 1) * MLSTM_DH)
            chain = d * hm + h
            q, k_t = q_ref[rows, sl], k_ref[sl, rows]
            v_aug = jnp.concatenate([v_ref[rows, sl], ones_col], axis=-1)
            a_row = a_t[h:h + 1, :]
            d_m = jnp.where(mask, jnp.exp2(u2[:, h:h + 1] + a_row * LOG2_E), 0.0)
            c_aug = c_ref[chain]
            scores = jnp.dot(q, k_t, preferred_element_type=F32) * d_m
            r = (jnp.dot(scores.astype(BF16), v_aug, preferred_element_type=F32)
                 + w_inter[:, h:h + 1] * jnp.dot(q, c_aug.astype(BF16), preferred_element_type=F32))
            num, den = r[:, :MLSTM_DH], r[:, MLSTM_DH:MLSTM_DH + 1]
            o_ref[rows, sl] = (num / jnp.maximum(jnp.abs(den), floor[:, h:h + 1])).astype(o_ref.dtype)

            kw_t = (k_t.astype(F32) * jnp.exp(a_row - mn[:, h:h + 1])).astype(BF16)
            c_ref[chain] = decay[:, h:h + 1] * c_aug + jnp.dot(kw_t, v_aug, preferred_element_type=F32)


def _mlstm(q, k_t, proj, gates, bg, n_sub=4):
    s = q.shape[0]
    rows = MLSTM_CHUNK * n_sub
    nb = s // rows
    w = N_MLSTM_HEADS * MLSTM_DH
    fwd = lambda blk: (lambda i: (i, blk))
    bwd = lambda blk: (lambda i: (nb - 1 - i, blk))
    specs = lambda ix: [pl.BlockSpec((rows, w), ix(0)),
                        pl.BlockSpec((w, rows), lambda i: ix(0)(i)[::-1]),
                        pl.BlockSpec((rows, w), ix(PROJ_V_M // w)),
                        pl.BlockSpec((rows, LANES), ix(0))]
    return pl.pallas_call(
        functools.partial(_mlstm_kernel, n_sub=n_sub),
        grid=(nb,),
        in_specs=specs(fwd) + specs(bwd) + [pl.BlockSpec((1, LANES), lambda i: (0, 0))],
        out_specs=[pl.BlockSpec((rows, w), fwd(0)), pl.BlockSpec((rows, w), bwd(0))],
        out_shape=[jax.ShapeDtypeStruct((s, w), BF16), jax.ShapeDtypeStruct((s, w), BF16)],
        scratch_shapes=[pltpu.VMEM((2 * N_MLSTM_HEADS, MLSTM_DH, MLSTM_AUG), F32),
                        pltpu.VMEM((2 * N_MLSTM_HEADS, LANES), F32)],
        compiler_params=_cparams(("arbitrary",)),
        name="mlstm",
    )(q, k_t, proj, gates, q, k_t, proj, gates, bg)


def _outproj_kernel(attn_ref, hf_ref, hb_ref, om_ref, x_ref, gm_ref, w_ref, gate_ref, g2_ref, sc_ref, sh_ref,
                    x1_ref, h2_ref):
    hsum = hf_ref[...].astype(F32) + hb_ref[...].astype(F32)
    gm = gm_ref[...]
    parts = []
    for h in range(N_MLSTM_HEADS):
        sl = slice(h * MLSTM_DH, (h + 1) * MLSTM_DH)
        seg = hsum[:, sl]
        parts.append(seg * _rms_scale(seg, MLSTM_DH) * gm[:, sl])
    ml = (jax.nn.sigmoid(om_ref[...].astype(F32)) * jnp.concatenate(parts, axis=-1)).astype(BF16)
    n_attn = attn_ref.shape[1]
    mixed = (jnp.dot(attn_ref[...], w_ref[:n_attn, :], preferred_element_type=F32)
             + jnp.dot(ml, w_ref[n_attn:, :], preferred_element_type=F32))
    x1 = x_ref[...] + gate_ref[...] * mixed
    x1_ref[...] = x1
    h2_ref[...] = _modulated_norm(x1, g2_ref[...], sc_ref[...], sh_ref[...]).astype(BF16)


def _outproj(attn, hf, hb, proj, x, gm, w_out, mod, g2, tm=512):
    s, d = x.shape
    wm = hf.shape[1]
    row = lambda blk: pl.BlockSpec((1, d), lambda i: (0, blk))
    return pl.pallas_call(
        _outproj_kernel,
        grid=(s // tm,),
        in_specs=[pl.BlockSpec((tm, attn.shape[1]), lambda i: (i, 0)),
                  pl.BlockSpec((tm, wm), lambda i: (i, 0)),
                  pl.BlockSpec((tm, wm), lambda i: (i, 0)),
                  pl.BlockSpec((tm, wm), lambda i: (i, PROJ_O_M // wm)),
                  pl.BlockSpec((tm, d), lambda i: (i, 0)),
                  pl.BlockSpec((1, wm), lambda i: (0, 0)),
                  pl.BlockSpec(w_out.shape, lambda i: (0, 0), pipeline_mode=pl.Buffered(1)),
                  row(MOD_GATE1), row(0), row(MOD_SCALE2), row(MOD_SHIFT2)],
        out_specs=[pl.BlockSpec((tm, d), lambda i: (i, 0)), pl.BlockSpec((tm, d), lambda i: (i, 0))],
        out_shape=[jax.ShapeDtypeStruct((s, d), F32), jax.ShapeDtypeStruct((s, d), BF16)],
        compiler_params=_cparams(("arbitrary",)),
        name="outproj",
    )(attn, hf, hb, proj, x, gm, w_out, mod, g2, mod, mod)


def _ffn_kernel(h2_ref, w1_ref, w2_ref, x1_hbm, gate_ref, o_ref, x1_buf, x1_sem, *, f_last):
    m, f = pl.program_id(0), pl.program_id(1)
    tm = o_ref.shape[0]
    x1_copy = pltpu.make_async_copy(x1_hbm.at[pl.ds(pl.multiple_of(m * tm, tm), tm), :], x1_buf, x1_sem)

    def hidden_tile():
        a = jnp.maximum(jnp.dot(h2_ref[...], w1_ref[...], preferred_element_type=F32), 0.0)
        return jnp.dot((a * a).astype(BF16), w2_ref[...], preferred_element_type=F32)

    @pl.when(f == 0)
    def _():
        x1_copy.start()
        o_ref[...] = hidden_tile()

    @pl.when(f > 0)
    def _():
        o_ref[...] += hidden_tile()

    @pl.when(f == f_last)
    def _():
        x1_copy.wait()
        o_ref[...] = x1_buf[...] + gate_ref[...] * o_ref[...]


def _ffn(h2, w1, w2, x1, mod, tm=1024, tf=1024):
    s, d = h2.shape
    dff = w1.shape[1]
    return pl.pallas_call(
        functools.partial(_ffn_kernel, f_last=dff // tf - 1),
        grid=(s // tm, dff // tf),
        in_specs=[pl.BlockSpec((tm, d), lambda m, f: (m, 0)),
                  pl.BlockSpec((d, tf), lambda m, f: (0, f)),
                  pl.BlockSpec((tf, d), lambda m, f: (f, 0)),
                  pl.BlockSpec(memory_space=pl.ANY),
                  pl.BlockSpec((1, d), lambda m, f: (0, MOD_GATE2))],
        out_specs=pl.BlockSpec((tm, d), lambda m, f: (m, 0)),
        out_shape=jax.ShapeDtypeStruct((s, d), F32),
        scratch_shapes=[pltpu.VMEM((tm, d), F32), pltpu.SemaphoreType.DMA(())],
        compiler_params=_cparams(("arbitrary", "arbitrary")),
        name="ffn",
    )(h2, w1, w2, x1, mod)


PACK_TILE = 512
PACK_SRC = 64
PACK_NSRC = PACK_TILE // PACK_SRC
W_IN_C_KV_START, W_IN_K_PE_START, W_IN_MIX_START, W_IN_GATES_START = 512, 768, 832, 4928
N_GATE_COLS = 16


def _pack_src_blocks():
    rows = []
    n_mix = 4 * N_MLSTM_HEADS * MLSTM_DH // PACK_TILE
    for n in range(n_mix):
        first = (W_IN_MIX_START + n * PACK_TILE) // PACK_SRC
        rows.append([first + i for i in range(PACK_NSRC)])
    rows.append(list(range(PACK_NSRC)))
    gates_blk = W_IN_GATES_START // PACK_SRC
    c_kv = list(range(W_IN_C_KV_START // PACK_SRC, W_IN_K_PE_START // PACK_SRC))
    k_pe = W_IN_K_PE_START // PACK_SRC
    rows.append(c_kv + [k_pe, k_pe, gates_blk, gates_blk])
    return jnp.asarray(rows, jnp.int32).reshape(-1)


def _pack_kernel(tbl_ref, *refs, n_mix):
    del tbl_ref
    srcs, o_ref = refs[:-1], refs[-1]
    n = pl.program_id(0)
    blk = lambda i: slice(i * PACK_SRC, (i + 1) * PACK_SRC)

    @pl.when(n <= n_mix)
    def _():
        for i, b in enumerate(srcs):
            o_ref[blk(i), :] = b[...].astype(BF16)

    @pl.when(n == n_mix + 1)
    def _():
        zeros = jnp.zeros(srcs[0].shape, BF16)
        for i in range(5):
            o_ref[blk(i), :] = srcs[i][...].astype(BF16)
        o_ref[blk(5), :] = zeros
        rows = lax.broadcasted_iota(jnp.int32, srcs[6].shape, 0)
        o_ref[blk(6), :] = jnp.where(rows < N_GATE_COLS, srcs[6][...], 0.0).astype(BF16)
        o_ref[blk(7), :] = zeros


def _pack_w_in(w_in_t):
    d = w_in_t.shape[1]
    tbl = _pack_src_blocks()
    n_tiles = tbl.shape[0] // PACK_NSRC
    src = lambda i: pl.BlockSpec((PACK_SRC, d), lambda n, t: (t[n * PACK_NSRC + i], 0))
    return pl.pallas_call(
        functools.partial(_pack_kernel, n_mix=n_tiles - 2),
        grid_spec=pltpu.PrefetchScalarGridSpec(
            num_scalar_prefetch=1, grid=(n_tiles,),
            in_specs=[src(i) for i in range(PACK_NSRC)],
            out_specs=pl.BlockSpec((PACK_TILE, d), lambda n, t: (n, 0))),
        out_shape=jax.ShapeDtypeStruct((n_tiles * PACK_TILE, d), BF16),
        compiler_params=_cparams(("arbitrary",)),
        name="packw",
    )(tbl, *([w_in_t] * PACK_NSRC))


def _pad_lanes(v, n):
    return jnp.pad(v, ((0, 0), (0, n - v.shape[1])))


def kernel(x, c, positions, w_ada, b_ada, norm_mix_g, w_in, b_gates, conv_w, conv_b, q_lora_g, w_uq,
           kv_lora_g, w_ukv, q_norm_g, k_norm_g, mlstm_norm_g, w_out, norm_mlp_g, w_ff1, w_ff2):
    bsz, s, d = x.shape
    assert bsz == 1, "kernels are written for a single sequence"
    xs = x[0]
    pos_row = positions.reshape(1, s)
    half = jnp.arange(ROPE_HALF, dtype=F32)
    freq = (ROPE_THETA ** (-half / ROPE_HALF)).reshape(ROPE_HALF, 1)
    row = lambda v: v.reshape(1, -1).astype(F32)

    for l in range(w_ada.shape[0]):
        mod = _mod(c.reshape(d, 1), w_ada[l], row(b_ada[l]))
        proj, gates = _inproj(xs, row(norm_mix_g[l]), mod, _pack_w_in(w_in[l].T))

        wuq_t = jnp.pad(w_uq[l].reshape(-1, N_MLA_HEADS, MLA_QK), ((0, 0), (0, 0), (0, MLA_QK_PAD - MLA_QK)))
        wuq_t = wuq_t.reshape(-1, N_MLA_HEADS * MLA_QK_PAD).T.astype(BF16)
        wkv = w_ukv[l].reshape(-1, N_MLA_HEADS, MLA_NOPE + MLA_V)
        wk = wkv[:, :, :MLA_NOPE].reshape(-1, N_MLA_HEADS * MLA_NOPE).astype(BF16)
        wv_t = wkv[:, :, MLA_NOPE:].reshape(-1, N_MLA_HEADS * MLA_V).T.astype(BF16)
        g_q, g_k = row(q_norm_g[l]), row(k_norm_g[l])
        q_t, k, v_t = _mlaprep(proj, pos_row, freq, row(q_lora_g[l]), row(kv_lora_g[l]), wuq_t, wk, wv_t,
                               g_q.reshape(-1, 1), _pad_lanes(g_k, MLA_QK_PAD))
        logit_bound = 1.02 * LOG2_E * MLA_QK ** 0.5 * jnp.max(jnp.abs(g_q)) * jnp.max(jnp.abs(g_k))
        attn, (w_out_b, w_ff1_b, w_ff2_b) = lax.cond(
            logit_bound <= SAFE_LOG2,
            functools.partial(_attention, shifted=False), functools.partial(_attention, shifted=True),
            q_t, k, v_t, (w_out[l], w_ff1[l], w_ff2[l]))

        n_qk_tiles = N_MLSTM_HEADS * MLSTM_DH // LANES
        assert (PROJ_Q_M, PROJ_K_M) == (0, n_qk_tiles * LANES)
        q_m = _conv(proj, conv_w[l], row(conv_b[l]), first_tile=0, n_tiles=n_qk_tiles, out_scale=1.0,
                    transpose_out=False)
        k_m_t = _conv(proj, conv_w[l], row(conv_b[l]), first_tile=n_qk_tiles, n_tiles=n_qk_tiles,
                      out_scale=MLSTM_DH ** -0.5, transpose_out=True)
        hf, hb = _mlstm(q_m, k_m_t, proj, gates, _pad_lanes(row(b_gates[l]), LANES))

        x1, h2 = _outproj(attn, hf, hb, proj, xs, row(mlstm_norm_g[l]), w_out_b, mod, row(norm_mlp_g[l]))
        xs = _ffn(h2, w_ff1_b, w_ff2_b, x1, mod)
    return xs[None]
```
